```python
import math
import jax, jax.numpy as jnp
from jax import lax
import numpy as np

D_MODEL = 2048
BATCH = 16
SEQ = 2048
DEPTH = 2

CHUNK = 64
N_META = 16
META_PAD = (-N_META) % CHUNK
D_HG = D_MODEL // 2
HG_HEADS = 8
HG_DK = D_HG // HG_HEADS
D_S5 = D_MODEL - D_HG
S5_GROUP = 16
S5_GROUPS = D_S5 // S5_GROUP
S5_STATE = 64
D_IN = 4 * D_HG + D_S5
D_FF = 5632
CONV_W = 3
EPS = 1e-6
F_FLOOR = 1e-6
DT_MIN = 1e-3
DT_MAX = 1e-1

kernel_name = "hymba_hgrn2_s5_convffn_block"


def rmsnorm(x, g):
    xf = x.astype(jnp.float32)
    y = xf * lax.rsqrt(jnp.mean(xf * xf, axis=-1, keepdims=True) + EPS)
    return (y * g.astype(jnp.float32)).astype(x.dtype)


def hgrn2_mixer(q, f_logit, i, g_out, lb, gain):
    f32 = jnp.float32
    b_sz, seq_len, _ = q.shape
    z = f_logit.astype(f32)
    lb = lb.astype(f32)
    f = lb + (1.0 - lb) * jax.nn.sigmoid(z)
    log_f = jnp.log(jnp.maximum(f, F_FLOOR))
    k = (1.0 - lb) * jax.nn.sigmoid(-z)
    qf = jax.nn.silu(q.astype(f32))
    v = i.astype(f32)
    lp = seq_len + META_PAD
    n_chunks = lp // CHUNK

    def to_chunks(t):
        t = jnp.pad(t, ((0, 0), (META_PAD, 0), (0, 0)))
        return t.reshape(b_sz, n_chunks, CHUNK, HG_HEADS, HG_DK).transpose(1, 0, 3, 2, 4)

    causal = jnp.tril(jnp.ones((CHUNK, CHUNK), dtype=bool))[:, :, None]

    def step(state, inp):
        qc, kc, vc, gc = inp
        b = jnp.cumsum(gc, axis=2)
        b_last = b[:, :, -1:, :]
        o_inter = jnp.einsum('bhtd,bhde->bhte', qc * jnp.exp(b), state)
        diff = b[:, :, :, None, :] - b[:, :, None, :, :]
        decay = jnp.exp(jnp.where(causal, diff, -jnp.inf))
        scores = jnp.einsum('bhtd,bhsd,bhtsd->bhts', qc, kc, decay)
        o_intra = jnp.einsum('bhts,bhse->bhte', scores, vc)
        new_state = (jnp.exp(b_last[:, :, 0, :])[..., None] * state
                     + jnp.einsum('bhsd,bhse->bhde', kc * jnp.exp(b_last - b), vc))
        return new_state, o_inter + o_intra

    s0 = jnp.zeros((b_sz, HG_HEADS, HG_DK, HG_DK), f32)
    _, o = lax.scan(step, s0, (to_chunks(qf), to_chunks(k), to_chunks(v), to_chunks(log_f)))
    o = o.transpose(1, 0, 3, 2, 4).reshape(b_sz, lp, HG_HEADS, HG_DK)[:, META_PAD:]
    o = o * lax.rsqrt(jnp.mean(o * o, axis=-1, keepdims=True) + EPS) * gain.astype(f32).reshape(HG_HEADS, HG_DK)
    return o.reshape(b_sz, seq_len, D_HG) * jax.nn.silu(g_out.astype(f32))


def _complex_scan_combine(e1, e2):
    a1r, a1i, b1r, b1i = e1
    a2r, a2i, b2r, b2i = e2
    ar = a2r * a1r - a2i * a1i
    ai = a2r * a1i + a2i * a1r
    a2r_, a2i_ = a2r[:, None], a2i[:, None]
    br = a2r_ * b1r - a2i_ * b1i + b2r
    bi = a2r_ * b1i + a2i_ * b1r + b2i
    return (ar, ai, br, bi)


def s5_mixer(u, lam_re, lam_im, log_step, b_re, b_im, c_re, c_im, d_skip, w_glu, b_glu, gain):
    f32 = jnp.float32
    b_sz, seq_len, _ = u.shape
    uf = u.astype(f32)
    ug = uf.reshape(b_sz, seq_len, S5_GROUPS, S5_GROUP)
    a_re = jnp.minimum(lam_re.astype(f32), -1e-4)
    a_im = lam_im.astype(f32)
    dt = jnp.exp(log_step.astype(f32))[:, None]
    mag = jnp.exp(a_re * dt)
    ab_re = mag * jnp.cos(a_im * dt)
    ab_im = mag * jnp.sin(a_im * dt)
    den = a_re * a_re + a_im * a_im
    x_re, x_im = ab_re - 1.0, ab_im
    z_re = (x_re * a_re + x_im * a_im) / den
    z_im = (x_im * a_re - x_re * a_im) / den
    br, bi = b_re.astype(f32), b_im.astype(f32)
    bb_re = z_re[..., None] * br - z_im[..., None] * bi
    bb_im = z_re[..., None] * bi + z_im[..., None] * br
    bu_re = jnp.einsum('blgh,gph->lbgp', ug, bb_re)
    bu_im = jnp.einsum('blgh,gph->lbgp', ug, bb_im)
    a_seq_re = jnp.broadcast_to(ab_re[None], (seq_len, S5_GROUPS, S5_STATE))
    a_seq_im = jnp.broadcast_to(ab_im[None], (seq_len, S5_GROUPS, S5_STATE))
    _, _, st_re, st_im = lax.associative_scan(_complex_scan_combine, (a_seq_re, a_seq_im, bu_re, bu_im), axis=0)
    y = (jnp.einsum('lbgp,ghp->blgh', st_re, c_re.astype(f32))
         - jnp.einsum('lbgp,ghp->blgh', st_im, c_im.astype(f32)))
    y = y.reshape(b_sz, seq_len, D_S5) + d_skip.astype(f32) * uf
    y = jax.nn.gelu(y)
    y = y * jax.nn.sigmoid(y @ w_glu.astype(f32) + b_glu.astype(f32))
    return rmsnorm(y, gain)


def conv_ffn(h, w_gate, w_up, conv_w, conv_b, w_down):
    a = h @ w_gate
    seq_len = a.shape[1]
    ap = jnp.pad(a, ((0, 0), (CONV_W - 1, 0), (0, 0)))
    conv = conv_b
    for j in range(CONV_W):
        conv = conv + conv_w[j] * ap[:, j:j + seq_len]
    return (jax.nn.silu(conv) * (h @ w_up)) @ w_down


def setup_inputs(seed: int = 0) -> dict:
    key = jax.random.key(seed)
    ks = jax.random.split(key, 32)
    f32 = jnp.float32

    def nrm(k, shape, s):
        return s * jax.random.normal(k, shape, f32)

    lam_im0 = jnp.pi * jnp.arange(S5_STATE, dtype=f32)
    return {
        "x": nrm(ks[0], (BATCH, SEQ, D_MODEL), 1.0),
        "meta_tokens": nrm(ks[1], (N_META, D_MODEL), 1.0),
        "lb_logits": nrm(ks[2], (DEPTH, D_HG), 0.1),
        "norm_mix": 1.0 + nrm(ks[3], (DEPTH, D_MODEL), 0.01),
        "w_in": nrm(ks[4], (DEPTH, D_MODEL, D_IN), D_MODEL ** -0.5),
        "hg_norm": 1.0 + nrm(ks[5], (DEPTH, D_HG), 0.01),
        "s5_lambda_re": -0.5 + nrm(ks[6], (DEPTH, S5_GROUPS, S5_STATE), 0.01),
        "s5_lambda_im": lam_im0 + nrm(ks[7], (DEPTH, S5_GROUPS, S5_STATE), 0.01),
        "s5_log_step": jax.random.uniform(ks[8], (DEPTH, S5_GROUPS), f32, math.log(DT_MIN), math.log(DT_MAX)),
        "s5_b_re": nrm(ks[9], (DEPTH, S5_GROUPS, S5_STATE, S5_GROUP), (2 * S5_GROUP) ** -0.5),
        "s5_b_im": nrm(ks[10], (DEPTH, S5_GROUPS, S5_STATE, S5_GROUP), (2 * S5_GROUP) ** -0.5),
        "s5_c_re": nrm(ks[11], (DEPTH, S5_GROUPS, S5_GROUP, S5_STATE), S5_STATE ** -0.5),
        "s5_c_im": nrm(ks[12], (DEPTH, S5_GROUPS, S5_GROUP, S5_STATE), S5_STATE ** -0.5),
        "s5_d": nrm(ks[13], (DEPTH, D_S5), 0.5),
        "w_glu": nrm(ks[14], (DEPTH, D_S5, D_S5), D_S5 ** -0.5),
        "b_glu": nrm(ks[15], (DEPTH, D_S5), 0.01),
        "s5_norm": 1.0 + nrm(ks[16], (DEPTH, D_S5), 0.01),
        "w_out": nrm(ks[17], (DEPTH, D_HG + D_S5, D_MODEL), (D_HG + D_S5) ** -0.5),
        "norm_ffn": 1.0 + nrm(ks[18], (DEPTH, D_MODEL), 0.01),
        "w_ffn_gate": nrm(ks[19], (DEPTH, D_MODEL, D_FF), D_MODEL ** -0.5),
        "w_ffn_up": nrm(ks[20], (DEPTH, D_MODEL, D_FF), D_MODEL ** -0.5),
        "ffn_conv_w": nrm(ks[21], (DEPTH, CONV_W, D_FF), CONV_W ** -0.5),
        "ffn_conv_b": nrm(ks[22], (DEPTH, D_FF), 0.01),
        "w_ffn_down": nrm(ks[23], (DEPTH, D_FF, D_MODEL), D_FF ** -0.5),
        "final_norm": 1.0 + nrm(ks[24], (D_MODEL,), 0.01),
    }


def reference(x, meta_tokens, lb_logits, norm_mix, w_in, hg_norm, s5_lambda_re, s5_lambda_im,
              s5_log_step, s5_b_re, s5_b_im, s5_c_re, s5_c_im, s5_d, w_glu, b_glu, s5_norm,
              w_out, norm_ffn, w_ffn_gate, w_ffn_up, ffn_conv_w, ffn_conv_b, w_ffn_down, final_norm):
    b_sz = x.shape[0]
    meta = jnp.broadcast_to(meta_tokens.astype(x.dtype)[None], (b_sz, N_META, D_MODEL))
    h = jnp.concatenate([meta, x], axis=1)
    sm = jax.nn.softmax(lb_logits.astype(jnp.float32), axis=0)
    lb_all = jnp.cumsum(sm, axis=0) - sm[0:1]
    for l in range(DEPTH):
        xn = rmsnorm(h, norm_mix[l])
        proj = xn @ w_in[l]
        q, f_logit, i_in, g_out, u = jnp.split(proj, [D_HG, 2 * D_HG, 3 * D_HG, 4 * D_HG], axis=-1)
        o_hg = hgrn2_mixer(q, f_logit, i_in, g_out, lb_all[l], hg_norm[l])
        o_s5 = s5_mixer(u, s5_lambda_re[l], s5_lambda_im[l], s5_log_step[l], s5_b_re[l], s5_b_im[l],
                        s5_c_re[l], s5_c_im[l], s5_d[l], w_glu[l], b_glu[l], s5_norm[l])
        mix = jnp.concatenate([o_hg, o_s5.astype(jnp.float32)], axis=-1).astype(h.dtype)
        h = h + mix @ w_out[l]
        h = h + conv_ffn(rmsnorm(h, norm_ffn[l]), w_ffn_gate[l], w_ffn_up[l], ffn_conv_w[l],
                         ffn_conv_b[l], w_ffn_down[l])
    return rmsnorm(h[:, N_META:], final_norm)
```

```python
import functools

import jax
import jax.numpy as jnp
from jax import lax
from jax.experimental import pallas as pl
from jax.experimental.pallas import tpu as pltpu

F32 = jnp.float32
BF16 = jnp.bfloat16

D_MODEL = 2048
DEPTH = 2
CHUNK = 64
N_META = 16
D_HG = 1024
HG_HEADS = 8
HG_DK = 128
D_S5 = 1024
S5_GROUP = 16
S5_GROUPS = 64
S5_STATE = 64
D_IN = 4 * D_HG + D_S5
D_FF = 5632
EPS = 1e-6
F_FLOOR = 1e-6

SEQ_PAD = CHUNK - N_META
VMEM_LIMIT = 56 * 1024 * 1024

S5_GT = 4
S5_CH = D_S5 // S5_GT
S5_NS = S5_GROUPS // S5_GT * S5_STATE
S5_TT = 64
S5_LN = 512


def _rms(x, g):
    ms = jnp.mean(x * x, axis=-1, keepdims=True)
    return x * lax.rsqrt(ms + EPS) * g


def _sigmoid(x):
    return 1.0 / (1.0 + jnp.exp(-x))


def _inproj_kernel(h_ref, g_ref, w_ref, o_ref, xn_ref):
    @pl.when(pl.program_id(1) == 0)
    def _():
        xn_ref[...] = _rms(h_ref[...], g_ref[...]).astype(BF16)

    o_ref[...] = jnp.dot(xn_ref[...], w_ref[...], preferred_element_type=F32)


def _inproj(h, g, w, tm=512, tn=1024):
    t, d = h.shape
    n = w.shape[1]
    return pl.pallas_call(
        _inproj_kernel,
        grid=(t // tm, n // tn),
        in_specs=[
            pl.BlockSpec((tm, d), lambda i, j: (i, 0)),
            pl.BlockSpec((1, d), lambda i, j: (0, 0)),
            pl.BlockSpec((d, tn), lambda i, j: (0, j)),
        ],
        out_specs=pl.BlockSpec((tm, tn), lambda i, j: (i, j)),
        out_shape=jax.ShapeDtypeStruct((t, n), F32),
        scratch_shapes=[pltpu.VMEM((tm, d), BF16)],
        compiler_params=pltpu.CompilerParams(
            dimension_semantics=("arbitrary", "arbitrary"), vmem_limit_bytes=VMEM_LIMIT),
        name="inproj",
    )(h, g.reshape(1, d), w)


def _block_rows(x, w, r):
    n = x.shape[1]
    x3 = x.reshape(CHUNK // w, w, n)
    return jnp.broadcast_to(x3[:, r:r + 1, :], x3.shape).reshape(CHUNK, n)


def _dot_nt(a, b):
    return lax.dot_general(a, b, (((1,), (1,)), ((), ())), preferred_element_type=F32)


def _dot_tn(a, b):
    return lax.dot_general(a, b, (((0,), (0,)), ((), ())), preferred_element_type=F32)


def _hgrn2_kernel(p_ref, lb_ref, gain_ref, o_ref, st_ref):
    @pl.when(pl.program_id(1) == 0)
    def _():
        st_ref[...] = jnp.zeros_like(st_ref)

    q = p_ref[:, 0:D_HG]
    z = p_ref[:, D_HG:2 * D_HG]
    v = p_ref[:, 2 * D_HG:3 * D_HG].astype(BF16)
    lb = lb_ref[...]

    ez = jnp.exp(-jnp.abs(z))
    rz = 1.0 / (1.0 + ez)
    sig_pos = jnp.where(z >= 0, rz, ez * rz)
    sig_neg = jnp.where(z >= 0, ez * rz, rz)
    gl = jnp.log(jnp.maximum(lb + (1.0 - lb) * sig_pos, F_FLOOR))
    kk = (1.0 - lb) * sig_neg
    qf = q * _sigmoid(q)

    hi = gl.astype(BF16)
    r1 = gl - hi.astype(F32)
    mid = r1.astype(BF16)
    lo = (r1 - mid.astype(F32)).astype(BF16)
    row = lax.broadcasted_iota(jnp.int32, (CHUNK, 3 * CHUNK), 0)
    col = lax.broadcasted_iota(jnp.int32, (CHUNK, 3 * CHUNK), 1) & (CHUNK - 1)
    tri3 = (row >= col).astype(BF16)
    b = jnp.dot(tri3, jnp.concatenate([hi, mid, lo], axis=0), preferred_element_type=F32)
    e = b - gl

    levels = (64, 32, 16, 8)
    qh = {}
    kh = {}
    for w in levels:
        b_start = _block_rows(e, w, 0)
        b_last = _block_rows(b, w, w - 1)
        qh[w] = (qf * jnp.exp(b - b_start)).astype(BF16)
        kh[w] = (kk * jnp.exp(b_last - b)).astype(BF16)
    b_mid = _block_rows(b, 8, 3)
    qh[0] = (qf * jnp.exp(b - b_mid)).astype(BF16)
    kh[0] = (kk * jnp.exp(b_mid - b)).astype(BF16)
    dec_all = jnp.exp(b[CHUNK - 1:CHUNK, :])

    ti = lax.broadcasted_iota(jnp.int32, (CHUNK, CHUNK), 0)
    si = lax.broadcasted_iota(jnp.int32, (CHUNK, CHUNK), 1)
    masks = {}
    for w, sh in ((32, 5), (16, 4), (8, 3)):
        masks[w] = ((ti >> sh) == (si >> sh) + 1) & ((ti >> (sh + 1)) == (si >> (sh + 1)))
    masks[0] = ((ti >> 3) == (si >> 3)) & (si <= ti)

    gate = p_ref[:, 3 * D_HG:4 * D_HG]
    gate = gate * _sigmoid(gate)
    gain = gain_ref[...]

    for h in range(HG_HEADS):
        sl = slice(h * HG_DK, (h + 1) * HG_DK)
        st = st_ref[h]
        o = _dot_nt(qh[64][:, sl], st.astype(BF16))
        sc = jnp.zeros((CHUNK, CHUNK), F32)
        for w in (32, 16, 8, 0):
            sc = jnp.where(masks[w], _dot_nt(qh[w][:, sl], kh[w][:, sl]), sc)
        vh = v[:, sl]
        o = o + jnp.dot(sc.astype(BF16), vh, preferred_element_type=F32)
        st_ref[h] = dec_all[:, sl] * st + _dot_tn(vh, kh[64][:, sl])
        ms = jnp.mean(o * o, axis=-1, keepdims=True)
        o = o * lax.rsqrt(ms + EPS) * gain[:, sl]
        o_ref[:, sl] = (o * gate[:, sl]).astype(BF16)


def _hgrn2(p3, lb, gain):
    bsz, lp, _ = p3.shape
    return pl.pallas_call(
        _hgrn2_kernel,
        grid=(bsz, lp // CHUNK),
        in_specs=[
            pl.BlockSpec((None, CHUNK, 4 * D_HG), lambda b, c: (b, c, 0)),
            pl.BlockSpec((1, D_HG), lambda b, c: (0, 0)),
            pl.BlockSpec((1, D_HG), lambda b, c: (0, 0)),
        ],
        out_specs=pl.BlockSpec((None, CHUNK, D_HG), lambda b, c: (b, c, 0)),
        out_shape=jax.ShapeDtypeStruct((bsz, lp, D_HG), BF16),
        scratch_shapes=[pltpu.VMEM((HG_HEADS, HG_DK, HG_DK), F32)],
        compiler_params=pltpu.CompilerParams(
            dimension_semantics=("arbitrary", "arbitrary"), vmem_limit_bytes=VMEM_LIMIT),
        name="hgrn2",
    )(p3, lb.reshape(1, D_HG), gain.reshape(1, D_HG))


def _gelu_tanh(x):
    return 0.5 * x * (1.0 + jnp.tanh(0.7978845608028654 * (x + 0.044715 * (x * x * x))))


def _s5_kernel(u_ref, ab_ref, bblk_ref, cblk_ref, d_ref, wglu_ref, bglu_ref, gain_ref, o_ref,
               ut_ref, bu_ref, sv_ref, yt_ref, ybuf_ref, xs_ref):
    i = pl.program_id(0)
    gt = pl.program_id(1)
    bsz, tt, _ = u_ref.shape

    for t in range(tt):
        ut_ref[t * bsz:(t + 1) * bsz, :] = u_ref[:, t, :].astype(BF16)

    bu = jnp.dot(ut_ref[...], bblk_ref[0], preferred_element_type=F32)
    bu_ref[...] = bu.reshape(tt, bsz, 2 * S5_NS)

    @pl.when(i == 0)
    def _():
        xs_ref[gt] = jnp.zeros((bsz, 2 * S5_NS), F32)

    for j in range(S5_NS // S5_LN):
        re_sl = pl.ds(j * S5_LN, S5_LN)
        im_sl = pl.ds(S5_NS + j * S5_LN, S5_LN)
        a_re = jnp.broadcast_to(ab_ref[0, 0:1, j * S5_LN:(j + 1) * S5_LN], (bsz, S5_LN))
        a_im = jnp.broadcast_to(ab_ref[0, 1:2, j * S5_LN:(j + 1) * S5_LN], (bsz, S5_LN))

        def step(t, carry):
            x_re, x_im = carry
            n_re = a_re * x_re - a_im * x_im + bu_ref[t, :, re_sl]
            n_im = a_re * x_im + a_im * x_re + bu_ref[t, :, im_sl]
            sv_ref[t, :, re_sl] = n_re.astype(BF16)
            sv_ref[t, :, im_sl] = n_im.astype(BF16)
            return n_re, n_im

        x_re, x_im = lax.fori_loop(0, tt, step, (xs_ref[gt, :, re_sl], xs_ref[gt, :, im_sl]))
        xs_ref[gt, :, re_sl] = x_re
        xs_ref[gt, :, im_sl] = x_im

    yt = jnp.dot(sv_ref[...].reshape(tt * bsz, 2 * S5_NS), cblk_ref[0],
                 preferred_element_type=F32)
    for k in range(S5_CH // 128):
        yt_ref[k] = yt[:, k * 128:(k + 1) * 128]

    for b in range(bsz):
        y = jnp.concatenate([yt_ref[k, pl.ds(b, tt, stride=bsz), :] for k in range(S5_CH // 128)],
                            axis=1)
        y = y + d_ref[...] * u_ref[b]
        ybuf_ref[gt, b * tt:(b + 1) * tt, :] = _gelu_tanh(y)

    @pl.when(gt == S5_GT - 1)
    def _():
        y = jnp.concatenate([ybuf_ref[k] for k in range(S5_GT)], axis=1)
        g = jnp.dot(y.astype(BF16), wglu_ref[...], preferred_element_type=F32) + bglu_ref[...]
        y = y * _sigmoid(g)
        o_ref[...] = _rms(y, gain_ref[...]).reshape(bsz, tt, D_S5).astype(BF16)


def _s5(p3, ab, bblk, cblk, d_skip, wglu, bglu, gain):
    bsz, lp, _ = p3.shape
    tt = S5_TT
    u_col0 = 4 * D_HG // S5_CH
    return pl.pallas_call(
        _s5_kernel,
        grid=(lp // tt, S5_GT),
        in_specs=[
            pl.BlockSpec((bsz, tt, S5_CH), lambda i, g: (0, i, u_col0 + g)),
            pl.BlockSpec((1, 2, S5_NS), lambda i, g: (g, 0, 0)),
            pl.BlockSpec((1, S5_CH, 2 * S5_NS), lambda i, g: (g, 0, 0)),
            pl.BlockSpec((1, 2 * S5_NS, S5_CH), lambda i, g: (g, 0, 0)),
            pl.BlockSpec((1, S5_CH), lambda i, g: (0, g)),
            pl.BlockSpec((D_S5, D_S5), lambda i, g: (0, 0)),
            pl.BlockSpec((1, D_S5), lambda i, g: (0, 0)),
            pl.BlockSpec((1, D_S5), lambda i, g: (0, 0)),
        ],
        out_specs=pl.BlockSpec((bsz, tt, D_S5), lambda i, g: (0, i, 0)),
        out_shape=jax.ShapeDtypeStruct((bsz, lp, D_S5), BF16),
        scratch_shapes=[
            pltpu.VMEM((tt * bsz, S5_CH), BF16),
            pltpu.VMEM((tt, bsz, 2 * S5_NS), F32),
            pltpu.VMEM((tt, bsz, 2 * S5_NS), BF16),
            pltpu.VMEM((S5_CH // 128, tt * bsz, 128), F32),
            pltpu.VMEM((S5_GT, bsz * tt, S5_CH), F32),
            pltpu.VMEM((S5_GT, bsz, 2 * S5_NS), F32),
        ],
        compiler_params=pltpu.CompilerParams(
            dimension_semantics=("arbitrary", "arbitrary"), vmem_limit_bytes=VMEM_LIMIT),
        name="s5",
    )(p3, ab, bblk, cblk, d_skip.reshape(1, D_S5), wglu, bglu.reshape(1, D_S5),
      gain.reshape(1, D_S5))


def _s5_params(lam_re, lam_im, log_step, b_re, b_im, c_re, c_im):
    a_re = jnp.minimum(lam_re.astype(F32), -1e-4)
    a_im = lam_im.astype(F32)
    dt = jnp.exp(log_step.astype(F32))[:, None]
    mag = jnp.exp(a_re * dt)
    ab_re = mag * jnp.cos(a_im * dt)
    ab_im = mag * jnp.sin(a_im * dt)
    den = a_re * a_re + a_im * a_im
    x_re, x_im = ab_re - 1.0, ab_im
    z_re = (x_re * a_re + x_im * a_im) / den
    z_im = (x_im * a_re - x_re * a_im) / den
    br, bi = b_re.astype(F32), b_im.astype(F32)
    bb_re = z_re[..., None] * br - z_im[..., None] * bi
    bb_im = z_re[..., None] * bi + z_im[..., None] * br
    gl = S5_GROUPS // S5_GT
    eye = jnp.eye(gl, dtype=F32)

    def pack_b(m):
        m4 = m.reshape(S5_GT, gl, S5_STATE, S5_GROUP)
        return jnp.einsum('ab,tapc->tacbp', eye, m4).reshape(S5_GT, S5_CH, S5_NS)

    def pack_c(m):
        m4 = m.reshape(S5_GT, gl, S5_GROUP, S5_STATE)
        return jnp.einsum('ab,tahp->tapbh', eye, m4).reshape(S5_GT, S5_NS, S5_CH)

    bblk = jnp.concatenate([pack_b(bb_re), pack_b(bb_im)], axis=2).astype(BF16)
    cblk = jnp.concatenate([pack_c(c_re.astype(F32)), -pack_c(c_im.astype(F32))], axis=1).astype(BF16)
    ab = jnp.stack([ab_re.reshape(S5_GT, S5_NS), ab_im.reshape(S5_GT, S5_NS)], axis=1)
    return ab, bblk, cblk


def _outproj_kernel(h_ref, a_ref, b_ref, w_ref, o_ref):
    o_ref[...] = (h_ref[...]
                  + jnp.dot(a_ref[...], w_ref[0], preferred_element_type=F32)
                  + jnp.dot(b_ref[...], w_ref[1], preferred_element_type=F32))


def _outproj(h, a, b, w2, tm=512):
    t, d = h.shape
    k = a.shape[1]
    return pl.pallas_call(
        _outproj_kernel,
        grid=(t // tm,),
        in_specs=[
            pl.BlockSpec((tm, d), lambda i: (i, 0)),
            pl.BlockSpec((tm, k), lambda i: (i, 0)),
            pl.BlockSpec((tm, k), lambda i: (i, 0)),
            pl.BlockSpec((2, k, d), lambda i: (0, 0, 0)),
        ],
        out_specs=pl.BlockSpec((tm, d), lambda i: (i, 0)),
        out_shape=jax.ShapeDtypeStruct((t, d), F32),
        compiler_params=pltpu.CompilerParams(
            dimension_semantics=("arbitrary",), vmem_limit_bytes=VMEM_LIMIT),
        name="outproj",
    )(h, a, b, w2)


FFN_HALO = 16


def _ffn_kernel(h_ref, halo_ref, g_ref, wg_ref, wu_ref, cw_ref, cb_ref, wd_ref, fg_ref, o_ref,
                hn_ref, *, final):
    f = pl.program_id(1)

    @pl.when(f == 0)
    def _():
        x = h_ref[...]
        hn_ref[FFN_HALO:, :] = _rms(x, g_ref[...]).astype(BF16)
        hn_ref[0:FFN_HALO, :] = _rms(halo_ref[...], g_ref[...]).astype(BF16)
        o_ref[...] = x

    a = jnp.dot(hn_ref[...], wg_ref[...], preferred_element_type=F32)
    up = jnp.dot(hn_ref[FFN_HALO:, :], wu_ref[...], preferred_element_type=F32)
    cw = cw_ref[...]
    conv = (cb_ref[...]
            + cw[0:1, :] * pltpu.roll(a, 2, 0)[FFN_HALO:, :]
            + cw[1:2, :] * pltpu.roll(a, 1, 0)[FFN_HALO:, :]
            + cw[2:3, :] * a[FFN_HALO:, :])
    hid = (conv * _sigmoid(conv) * up).astype(BF16)
    o_ref[...] += jnp.dot(hid, wd_ref[...], preferred_element_type=F32)

    if final:
        @pl.when(f == pl.num_programs(1) - 1)
        def _():
            o_ref[...] = _rms(o_ref[...], fg_ref[...])


def _ffn(h, g, wg, wu, cw, cb, wd, fg, final, tm=512, tf=512):
    t, d = h.shape
    ff = wg.shape[1]
    hb = tm // FFN_HALO
    return pl.pallas_call(
        functools.partial(_ffn_kernel, final=final),
        grid=(t // tm, ff // tf),
        in_specs=[
            pl.BlockSpec((tm, d), lambda i, f: (i, 0)),
            pl.BlockSpec((FFN_HALO, d), lambda i, f: (jnp.maximum(i * hb - 1, 0), 0)),
            pl.BlockSpec((1, d), lambda i, f: (0, 0)),
            pl.BlockSpec((d, tf), lambda i, f: (0, f)),
            pl.BlockSpec((d, tf), lambda i, f: (0, f)),
            pl.BlockSpec((3, tf), lambda i, f: (0, f)),
            pl.BlockSpec((1, tf), lambda i, f: (0, f)),
            pl.BlockSpec((tf, d), lambda i, f: (f, 0)),
            pl.BlockSpec((1, d), lambda i, f: (0, 0)),
        ],
        out_specs=pl.BlockSpec((tm, d), lambda i, f: (i, 0)),
        out_shape=jax.ShapeDtypeStruct((t, d), F32),
        scratch_shapes=[pltpu.VMEM((tm + FFN_HALO, d), BF16)],
        compiler_params=pltpu.CompilerParams(
            dimension_semantics=("arbitrary", "arbitrary"), vmem_limit_bytes=VMEM_LIMIT),
        name="ffn_final" if final else "ffn",
    )(h, h, g.reshape(1, d), wg, wu, cw, cb.reshape(1, ff), wd, fg.reshape(1, d))


def kernel(x, meta_tokens, lb_logits, norm_mix, w_in, hg_norm, s5_lambda_re, s5_lambda_im,
           s5_log_step, s5_b_re, s5_b_im, s5_c_re, s5_c_im, s5_d, w_glu, b_glu, s5_norm, w_out,
           norm_ffn, w_ffn_gate, w_ffn_up, ffn_conv_w, ffn_conv_b, w_ffn_down, final_norm):
    bsz, seq, d = x.shape
    lp = seq + CHUNK
    meta = jnp.broadcast_to(meta_tokens.astype(x.dtype)[None], (bsz, N_META, d))
    pad = jnp.zeros((bsz, SEQ_PAD, d), x.dtype)
    h = jnp.concatenate([pad, meta, x], axis=1).reshape(bsz * lp, d)

    sm = jax.nn.softmax(lb_logits.astype(F32), axis=0)
    lb_all = jnp.cumsum(sm, axis=0) - sm[0:1]

    for l in range(DEPTH):
        proj = _inproj(h, norm_mix[l], w_in[l].astype(BF16))
        p3 = proj.reshape(bsz, lp, D_IN)
        o_hg = _hgrn2(p3, lb_all[l], hg_norm[l])
        ab, bblk, cblk = _s5_params(s5_lambda_re[l], s5_lambda_im[l], s5_log_step[l],
                                    s5_b_re[l], s5_b_im[l], s5_c_re[l], s5_c_im[l])
        o_s5 = _s5(p3, ab, bblk, cblk, s5_d[l], w_glu[l].astype(BF16), b_glu[l], s5_norm[l])
        h = _outproj(h, o_hg.reshape(bsz * lp, D_HG), o_s5.reshape(bsz * lp, D_S5),
                     w_out[l].astype(BF16).reshape(2, D_HG, d))
        h = _ffn(h, norm_ffn[l], w_ffn_gate[l].astype(BF16), w_ffn_up[l].astype(BF16),
                 ffn_conv_w[l], ffn_conv_b[l], w_ffn_down[l].astype(BF16), final_norm,
                 final=(l == DEPTH - 1))
    return h.reshape(bsz, lp, d)[:, CHUNK:, :]
```

```python
import functools

import jax
import jax.numpy as jnp
from jax import lax
from jax.experimental import pallas as pl
from jax.experimental.pallas import tpu as pltpu

F32 = jnp.float32
BF16 = jnp.bfloat16

D_MODEL = 2048
DEPTH = 2
CHUNK = 64
N_META = 16
D_HG = 1024
HG_HEADS = 8
HG_DK = 128
D_S5 = 1024
S5_GROUP = 16
S5_GROUPS = 64
S5_STATE = 64
D_IN = 4 * D_HG + D_S5
D_FF = 5632
EPS = 1e-6
F_FLOOR = 1e-6
LOG2E = 1.4426950408889634

SEQ_PAD = CHUNK - N_META
VMEM_LIMIT = 56 * 1024 * 1024

S5_GT = 4
S5_CH = D_S5 // S5_GT
S5_NS = S5_GROUPS // S5_GT * S5_STATE
S5_TT = 88
S5_YP = 24
S5_LN = 512


def _rms(x, g):
    ms = jnp.mean(x * x, axis=-1, keepdims=True)
    return x * lax.rsqrt(ms + EPS) * g


def _sigmoid(x):
    return 1.0 / (1.0 + jnp.exp(-x))


def _inproj_kernel(h_ref, g_ref, w_ref, o_ref, xn_ref):
    @pl.when(pl.program_id(1) == 0)
    def _():
        xn_ref[...] = _rms(h_ref[...], g_ref[...]).astype(BF16)

    o_ref[...] = jnp.dot(xn_ref[...], w_ref[...], preferred_element_type=F32)


def _inproj(h, g, w, tm=1024, tn=1024):
    t, d = h.shape
    n = w.shape[1]
    return pl.pallas_call(
        _inproj_kernel,
        grid=(t // tm, n // tn),
        in_specs=[
            pl.BlockSpec((tm, d), lambda i, j: (i, 0)),
            pl.BlockSpec((1, d), lambda i, j: (0, 0)),
            pl.BlockSpec((d, tn), lambda i, j: (0, j)),
        ],
        out_specs=pl.BlockSpec((tm, tn), lambda i, j: (i, j)),
        out_shape=jax.ShapeDtypeStruct((t, n), F32),
        scratch_shapes=[pltpu.VMEM((tm, d), BF16)],
        compiler_params=pltpu.CompilerParams(
            dimension_semantics=("arbitrary", "arbitrary"), vmem_limit_bytes=VMEM_LIMIT),
        name="inproj",
    )(h, g.reshape(1, d), w)


def _pick_rows(x, m, first):
    j = lax.broadcasted_iota(jnp.int32, x.shape, 0)
    out = None
    for blk in range(8 // m - 1, -1, -1):
        r = blk * m + (0 if first else m - 1)
        row = jnp.broadcast_to(x[r:r + 1, :], x.shape)
        out = row if out is None else jnp.where(j < (blk + 1) * m, row, out)
    return out


def _expand_rows(x):
    return jnp.concatenate(
        [jnp.broadcast_to(x[j:j + 1, :], (8, x.shape[1])) for j in range(8)], axis=0)


def _dot_nt(a, b):
    return lax.dot_general(a, b, (((1,), (1,)), ((), ())), preferred_element_type=F32)


def _dot_tn(a, b):
    return lax.dot_general(a, b, (((0,), (0,)), ((), ())), preferred_element_type=F32)


def _hgrn2_kernel(p_ref, lb_ref, gain_ref, o_ref, st_ref):
    @pl.when(pl.program_id(1) == 0)
    def _():
        st_ref[...] = jnp.zeros_like(st_ref)

    q = p_ref[:, 0:D_HG]
    z = p_ref[:, D_HG:2 * D_HG]
    v = p_ref[:, 2 * D_HG:3 * D_HG].astype(BF16)
    lb = lb_ref[...]

    ez = jnp.exp2(jnp.abs(z) * (-LOG2E))
    rz = 1.0 / (1.0 + ez)
    sig_pos = jnp.where(z >= 0, rz, ez * rz)
    sig_neg = jnp.where(z >= 0, ez * rz, rz)
    gl = jnp.log2(jnp.maximum(lb + (1.0 - lb) * sig_pos, F_FLOOR))
    kk = (1.0 - lb) * sig_neg
    qf = q * _sigmoid(q)

    hi = gl.astype(BF16)
    r1 = gl - hi.astype(F32)
    mid = r1.astype(BF16)
    lo = (r1 - mid.astype(F32)).astype(BF16)
    nsel = CHUNK + 16
    row = lax.broadcasted_iota(jnp.int32, (nsel, 3 * CHUNK), 0)
    col = lax.broadcasted_iota(jnp.int32, (nsel, 3 * CHUNK), 1) & (CHUNK - 1)
    tgt = jnp.where(row < CHUNK, row,
                    jnp.where(row < CHUNK + 8, (row - CHUNK) * 8 + 7, (row - CHUNK - 8) * 8 + 3))
    tri3 = (tgt >= col).astype(BF16)
    cum = jnp.dot(tri3, jnp.concatenate([hi, mid, lo], axis=0), preferred_element_type=F32)
    b = cum[0:CHUNK]
    l8 = cum[CHUNK:CHUNK + 8]
    m8 = cum[CHUNK + 8:CHUNK + 16]
    j8 = lax.broadcasted_iota(jnp.int32, l8.shape, 0)
    s8 = jnp.where(j8 == 0, 0.0, pltpu.roll(l8, 1, 0))

    qe = qf * jnp.exp2(b - _expand_rows(s8))
    kf = kk * jnp.exp2(_expand_rows(l8) - b)
    qh = {8: qe.astype(BF16)}
    kh = {8: kf.astype(BF16)}
    for w, m in ((16, 2), (32, 4), (64, 8)):
        cq = jnp.exp2(s8 - _pick_rows(s8, m, True))
        ck = jnp.exp2(_pick_rows(l8, m, False) - l8)
        qh[w] = (qe * _expand_rows(cq)).astype(BF16)
        kh[w] = (kf * _expand_rows(ck)).astype(BF16)
    b_mid = _expand_rows(m8)
    qh[0] = (qf * jnp.exp2(b - b_mid)).astype(BF16)
    kh[0] = (kk * jnp.exp2(b_mid - b)).astype(BF16)
    dec_all = jnp.exp2(l8[7:8, :])

    ti = lax.broadcasted_iota(jnp.int32, (CHUNK, CHUNK), 0)
    si = lax.broadcasted_iota(jnp.int32, (CHUNK, CHUNK), 1)
    masks = {}
    for w, sh in ((32, 5), (16, 4), (8, 3)):
        masks[w] = ((ti >> sh) == (si >> sh) + 1) & ((ti >> (sh + 1)) == (si >> (sh + 1)))
    masks[0] = ((ti >> 3) == (si >> 3)) & (si <= ti)

    gate = p_ref[:, 3 * D_HG:4 * D_HG]
    gate = gate * _sigmoid(gate)
    gain = gain_ref[...]

    heads = [slice(h * HG_DK, (h + 1) * HG_DK) for h in range(HG_HEADS)]
    sts = [st_ref[h] for h in range(HG_HEADS)]
    o_inter = [_dot_nt(qh[64][:, sl], st.astype(BF16)) for sl, st in zip(heads, sts)]
    scores = []
    for sl in heads:
        sc = jnp.zeros((CHUNK, CHUNK), F32)
        for w in (32, 16, 8, 0):
            sc = jnp.where(masks[w], _dot_nt(qh[w][:, sl], kh[w][:, sl]), sc)
        scores.append(sc.astype(BF16))
    outs = [oi + jnp.dot(sc, v[:, sl], preferred_element_type=F32)
            for oi, sc, sl in zip(o_inter, scores, heads)]
    for h, (sl, st) in enumerate(zip(heads, sts)):
        st_ref[h] = dec_all[:, sl] * st + _dot_tn(v[:, sl], kh[64][:, sl])
    for o, sl in zip(outs, heads):
        ms = jnp.mean(o * o, axis=-1, keepdims=True)
        o = o * lax.rsqrt(ms + EPS) * gain[:, sl]
        o_ref[:, sl] = (o * gate[:, sl]).astype(BF16)


def _hgrn2(p3, lb, gain):
    bsz, lp, _ = p3.shape
    return pl.pallas_call(
        _hgrn2_kernel,
        grid=(bsz, lp // CHUNK),
        in_specs=[
            pl.BlockSpec((None, CHUNK, 4 * D_HG), lambda b, c: (b, c, 0)),
            pl.BlockSpec((1, D_HG), lambda b, c: (0, 0)),
            pl.BlockSpec((1, D_HG), lambda b, c: (0, 0)),
        ],
        out_specs=pl.BlockSpec((None, CHUNK, D_HG), lambda b, c: (b, c, 0)),
        out_shape=jax.ShapeDtypeStruct((bsz, lp, D_HG), BF16),
        scratch_shapes=[pltpu.VMEM((HG_HEADS, HG_DK, HG_DK), F32)],
        compiler_params=pltpu.CompilerParams(
            dimension_semantics=("arbitrary", "arbitrary"), vmem_limit_bytes=VMEM_LIMIT),
        name="hgrn2",
    )(p3, lb.reshape(1, D_HG), gain.reshape(1, D_HG))


def _gelu_tanh(x):
    return 0.5 * x * (1.0 + jnp.tanh(0.7978845608028654 * (x + 0.044715 * (x * x * x))))


def _s5_kernel(u0_ref, u1_ref, ab_ref, bblk_ref, cblk_ref, d_ref, wglu_ref, bglu_ref, gain_ref,
               o_ref, ut_ref, bu_ref, sv_ref, yt_ref, ybuf_ref, xs_ref):
    i = pl.program_id(0)
    gt = pl.program_id(1)
    bsz, tt, _ = u0_ref.shape
    u_refs = (u0_ref, u1_ref)

    for k, u_ref in enumerate(u_refs):
        u_rows = u_ref.reshape(bsz * tt, 128)
        for t in range(tt):
            ut_ref[t * bsz:(t + 1) * bsz, k * 128:(k + 1) * 128] = (
                u_rows[pl.ds(t, bsz, stride=tt), :].astype(BF16))

    bu = jnp.dot(ut_ref[...], bblk_ref[0], preferred_element_type=F32)
    bu_ref[...] = bu.reshape(tt, bsz, 2 * S5_NS)

    @pl.when(i == 0)
    def _():
        xs_ref[gt] = jnp.zeros((bsz, 2 * S5_NS), F32)

    for j in range(S5_NS // S5_LN):
        re_sl = pl.ds(j * S5_LN, S5_LN)
        im_sl = pl.ds(S5_NS + j * S5_LN, S5_LN)
        a_re = jnp.broadcast_to(ab_ref[0, 0:1, j * S5_LN:(j + 1) * S5_LN], (bsz, S5_LN))
        a_im = jnp.broadcast_to(ab_ref[0, 1:2, j * S5_LN:(j + 1) * S5_LN], (bsz, S5_LN))

        def step(t, carry):
            x_re, x_im = carry
            n_re = a_re * x_re - a_im * x_im + bu_ref[t, :, re_sl]
            n_im = a_re * x_im + a_im * x_re + bu_ref[t, :, im_sl]
            sv_ref[t, :, re_sl] = n_re.astype(BF16)
            sv_ref[t, :, im_sl] = n_im.astype(BF16)
            return n_re, n_im

        x_re, x_im = lax.fori_loop(0, tt, step, (xs_ref[gt, :, re_sl], xs_ref[gt, :, im_sl]))
        xs_ref[gt, :, re_sl] = x_re
        xs_ref[gt, :, im_sl] = x_im

    yt = jnp.dot(sv_ref[...].reshape(tt * bsz, 2 * S5_NS), cblk_ref[0],
                 preferred_element_type=F32)
    for k, u_ref in enumerate(u_refs):
        lanes = slice(k * 128, (k + 1) * 128)
        for t in range(tt):
            yt_ref[k, t * S5_YP:t * S5_YP + bsz, :] = yt[t * bsz:(t + 1) * bsz, lanes]
        for b in range(bsz):
            y = yt_ref[k, pl.ds(b, tt, stride=S5_YP), :] + d_ref[:, lanes] * u_ref[b]
            ybuf_ref[gt, b * tt:(b + 1) * tt, lanes] = _gelu_tanh(y)

    @pl.when(gt == S5_GT - 1)
    def _():
        y = jnp.concatenate([ybuf_ref[k] for k in range(S5_GT)], axis=1)
        g = jnp.dot(y.astype(BF16), wglu_ref[...], preferred_element_type=F32) + bglu_ref[...]
        y = y * _sigmoid(g)
        o_ref[...] = _rms(y, gain_ref[...]).reshape(bsz, tt, D_S5).astype(BF16)


def _s5(p3, ab, bblk, cblk, d_skip, wglu, bglu, gain):
    bsz, lp, _ = p3.shape
    tt = S5_TT
    u_col0 = 4 * D_HG // 128
    return pl.pallas_call(
        _s5_kernel,
        grid=(lp // tt, S5_GT),
        in_specs=[
            pl.BlockSpec((bsz, tt, 128), lambda i, g: (0, i, u_col0 + 2 * g)),
            pl.BlockSpec((bsz, tt, 128), lambda i, g: (0, i, u_col0 + 2 * g + 1)),
            pl.BlockSpec((1, 2, S5_NS), lambda i, g: (g, 0, 0)),
            pl.BlockSpec((1, S5_CH, 2 * S5_NS), lambda i, g: (g, 0, 0)),
            pl.BlockSpec((1, 2 * S5_NS, S5_CH), lambda i, g: (g, 0, 0)),
            pl.BlockSpec((1, S5_CH), lambda i, g: (0, g)),
            pl.BlockSpec((D_S5, D_S5), lambda i, g: (0, 0)),
            pl.BlockSpec((1, D_S5), lambda i, g: (0, 0)),
            pl.BlockSpec((1, D_S5), lambda i, g: (0, 0)),
        ],
        out_specs=pl.BlockSpec((bsz, tt, D_S5), lambda i, g: (0, i, 0)),
        out_shape=jax.ShapeDtypeStruct((bsz, lp, D_S5), BF16),
        scratch_shapes=[
            pltpu.VMEM((tt * bsz, S5_CH), BF16),
            pltpu.VMEM((tt, bsz, 2 * S5_NS), F32),
            pltpu.VMEM((tt, bsz, 2 * S5_NS), BF16),
            pltpu.VMEM((S5_CH // 128, tt * S5_YP, 128), F32),
            pltpu.VMEM((S5_GT, bsz * tt, S5_CH), F32),
            pltpu.VMEM((S5_GT, bsz, 2 * S5_NS), F32),
        ],
        compiler_params=pltpu.CompilerParams(
            dimension_semantics=("arbitrary", "arbitrary"), vmem_limit_bytes=VMEM_LIMIT),
        name="s5",
    )(p3, p3, ab, bblk, cblk, d_skip.reshape(1, D_S5), wglu, bglu.reshape(1, D_S5),
      gain.reshape(1, D_S5))


def _s5_params(lam_re, lam_im, log_step, b_re, b_im, c_re, c_im):
    a_re = jnp.minimum(lam_re.astype(F32), -1e-4)
    a_im = lam_im.astype(F32)
    dt = jnp.exp(log_step.astype(F32))[:, None]
    mag = jnp.exp(a_re * dt)
    ab_re = mag * jnp.cos(a_im * dt)
    ab_im = mag * jnp.sin(a_im * dt)
    den = a_re * a_re + a_im * a_im
    x_re, x_im = ab_re - 1.0, ab_im
    z_re = (x_re * a_re + x_im * a_im) / den
    z_im = (x_im * a_re - x_re * a_im) / den
    br, bi = b_re.astype(F32), b_im.astype(F32)
    bb_re = z_re[..., None] * br - z_im[..., None] * bi
    bb_im = z_re[..., None] * bi + z_im[..., None] * br
    gl = S5_GROUPS // S5_GT
    eye = jnp.eye(gl, dtype=F32)

    def pack_b(m):
        m4 = m.reshape(S5_GT, gl, S5_STATE, S5_GROUP)
        return jnp.einsum('ab,tapc->tacbp', eye, m4).reshape(S5_GT, S5_CH, S5_NS)

    def pack_c(m):
        m4 = m.reshape(S5_GT, gl, S5_GROUP, S5_STATE)
        return jnp.einsum('ab,tahp->tapbh', eye, m4).reshape(S5_GT, S5_NS, S5_CH)

    bblk = jnp.concatenate([pack_b(bb_re), pack_b(bb_im)], axis=2).astype(BF16)
    cblk = jnp.concatenate([pack_c(c_re.astype(F32)), -pack_c(c_im.astype(F32))], axis=1).astype(BF16)
    ab = jnp.stack([ab_re.reshape(S5_GT, S5_NS), ab_im.reshape(S5_GT, S5_NS)], axis=1)
    return ab, bblk, cblk


def _outproj_kernel(h_ref, a_ref, b_ref, w_ref, o_ref):
    o_ref[...] = (h_ref[...]
                  + jnp.dot(a_ref[...], w_ref[0], preferred_element_type=F32)
                  + jnp.dot(b_ref[...], w_ref[1], preferred_element_type=F32))


def _outproj(h, a, b, w2, tm=512):
    t, d = h.shape
    k = a.shape[1]
    return pl.pallas_call(
        _outproj_kernel,
        grid=(t // tm,),
        in_specs=[
            pl.BlockSpec((tm, d), lambda i: (i, 0)),
            pl.BlockSpec((tm, k), lambda i: (i, 0)),
            pl.BlockSpec((tm, k), lambda i: (i, 0)),
            pl.BlockSpec((2, k, d), lambda i: (0, 0, 0)),
        ],
        out_specs=pl.BlockSpec((tm, d), lambda i: (i, 0)),
        out_shape=jax.ShapeDtypeStruct((t, d), F32),
        compiler_params=pltpu.CompilerParams(
            dimension_semantics=("arbitrary",), vmem_limit_bytes=VMEM_LIMIT),
        name="outproj",
    )(h, a, b, w2)


FFN_HALO = 16


def _ffn_kernel(h_ref, halo_ref, g_ref, wg_ref, wu_ref, cw_ref, cb_ref, wd_ref, fg_ref, o_ref,
                hn_ref, *, final):
    f = pl.program_id(1)

    @pl.when(f == 0)
    def _():
        x = h_ref[...]
        hn_ref[FFN_HALO:, :] = _rms(x, g_ref[...]).astype(BF16)
        hn_ref[0:FFN_HALO, :] = _rms(halo_ref[...], g_ref[...]).astype(BF16)
        o_ref[...] = x

    a = jnp.dot(hn_ref[...], wg_ref[...], preferred_element_type=F32)
    up = jnp.dot(hn_ref[FFN_HALO:, :], wu_ref[...], preferred_element_type=F32)
    cw = cw_ref[...]
    conv = (cb_ref[...]
            + cw[0:1, :] * pltpu.roll(a, 2, 0)[FFN_HALO:, :]
            + cw[1:2, :] * pltpu.roll(a, 1, 0)[FFN_HALO:, :]
            + cw[2:3, :] * a[FFN_HALO:, :])
    hid = (conv * _sigmoid(conv) * up).astype(BF16)
    o_ref[...] += jnp.dot(hid, wd_ref[...], preferred_element_type=F32)

    if final:
        @pl.when(f == pl.num_programs(1) - 1)
        def _():
            o_ref[...] = _rms(o_ref[...], fg_ref[...])


def _ffn(h, g, wg, wu, cw, cb, wd, fg, final, tm=512, tf=512):
    t, d = h.shape
    ff = wg.shape[1]
    hb = tm // FFN_HALO
    return pl.pallas_call(
        functools.partial(_ffn_kernel, final=final),
        grid=(t // tm, ff // tf),
        in_specs=[
            pl.BlockSpec((tm, d), lambda i, f: (i, 0)),
            pl.BlockSpec((FFN_HALO, d), lambda i, f: (jnp.maximum(i * hb - 1, 0), 0)),
            pl.BlockSpec((1, d), lambda i, f: (0, 0)),
            pl.BlockSpec((d, tf), lambda i, f: (0, f)),
            pl.BlockSpec((d, tf), lambda i, f: (0, f)),
            pl.BlockSpec((3, tf), lambda i, f: (0, f)),
            pl.BlockSpec((1, tf), lambda i, f: (0, f)),
            pl.BlockSpec((tf, d), lambda i, f: (f, 0)),
            pl.BlockSpec((1, d), lambda i, f: (0, 0)),
        ],
        out_specs=pl.BlockSpec((tm, d), lambda i, f: (i, 0)),
        out_shape=jax.ShapeDtypeStruct((t, d), F32),
        scratch_shapes=[pltpu.VMEM((tm + FFN_HALO, d), BF16)],
        compiler_params=pltpu.CompilerParams(
            dimension_semantics=("arbitrary", "arbitrary"), vmem_limit_bytes=VMEM_LIMIT),
        name="ffn_final" if final else "ffn",
    )(h, h, g.reshape(1, d), wg, wu, cw, cb.reshape(1, ff), wd, fg.reshape(1, d))


def kernel(x, meta_tokens, lb_logits, norm_mix, w_in, hg_norm, s5_lambda_re, s5_lambda_im,
           s5_log_step, s5_b_re, s5_b_im, s5_c_re, s5_c_im, s5_d, w_glu, b_glu, s5_norm, w_out,
           norm_ffn, w_ffn_gate, w_ffn_up, ffn_conv_w, ffn_conv_b, w_ffn_down, final_norm):
    bsz, seq, d = x.shape
    lp = seq + CHUNK
    meta = jnp.broadcast_to(meta_tokens.astype(x.dtype)[None], (bsz, N_META, d))
    pad = jnp.zeros((bsz, SEQ_PAD, d), x.dtype)
    h = jnp.concatenate([pad, meta, x], axis=1).reshape(bsz * lp, d)

    sm = jax.nn.softmax(lb_logits.astype(F32), axis=0)
    lb_all = jnp.cumsum(sm, axis=0) - sm[0:1]

    for l in range(DEPTH):
        proj = _inproj(h, norm_mix[l], w_in[l].astype(BF16))
        p3 = proj.reshape(bsz, lp, D_IN)
        o_hg = _hgrn2(p3, lb_all[l], hg_norm[l])
        ab, bblk, cblk = _s5_params(s5_lambda_re[l], s5_lambda_im[l], s5_log_step[l],
                                    s5_b_re[l], s5_b_im[l], s5_c_re[l], s5_c_im[l])
        o_s5 = _s5(p3, ab, bblk, cblk, s5_d[l], w_glu[l].astype(BF16), b_glu[l], s5_norm[l])
        h = _outproj(h, o_hg.reshape(bsz * lp, D_HG), o_s5.reshape(bsz * lp, D_S5),
                     w_out[l].astype(BF16).reshape(2, D_HG, d))
        h = _ffn(h, norm_ffn[l], w_ffn_gate[l].astype(BF16), w_ffn_up[l].astype(BF16),
                 ffn_conv_w[l], ffn_conv_b[l], w_ffn_down[l].astype(BF16), final_norm,
                 final=(l == DEPTH - 1))
    return h.reshape(bsz, lp, d)[:, CHUNK:, :]
```

```python
import functools

import jax
import jax.numpy as jnp
from jax import lax
from jax.experimental import pallas as pl
from jax.experimental.pallas import tpu as pltpu

F32 = jnp.float32
BF16 = jnp.bfloat16

D_MODEL = 2048
DEPTH = 2
CHUNK = 64
N_META = 16
D_HG = 1024
HG_HEADS = 8
HG_DK = 128
D_S5 = 1024
S5_GROUP = 16
S5_GROUPS = 64
S5_STATE = 64
D_IN = 4 * D_HG + D_S5
D_FF = 5632
EPS = 1e-6
F_FLOOR = 1e-6
LOG2E = 1.4426950408889634

SEQ_PAD = CHUNK - N_META
VMEM_LIMIT = 56 * 1024 * 1024

S5_R = 16
S5_K = S5_R * S5_GROUP
S5_PAIRS = S5_GROUPS // 2


def _rms(x, g):
    ms = jnp.mean(x * x, axis=-1, keepdims=True)
    return x * lax.rsqrt(ms + EPS) * g


def _sigmoid(x):
    return 1.0 / (1.0 + jnp.exp(-x))


def _inproj_kernel(h_ref, g_ref, w_ref, o_ref, xn_ref):
    @pl.when(pl.program_id(1) == 0)
    def _():
        xn_ref[...] = _rms(h_ref[...], g_ref[...]).astype(BF16)

    o_ref[...] = jnp.dot(xn_ref[...], w_ref[...], preferred_element_type=F32)


def _inproj(h, g, w, tm=1024, tn=1024):
    t, d = h.shape
    n = w.shape[1]
    return pl.pallas_call(
        _inproj_kernel,
        grid=(t // tm, n // tn),
        in_specs=[
            pl.BlockSpec((tm, d), lambda i, j: (i, 0)),
            pl.BlockSpec((1, d), lambda i, j: (0, 0)),
            pl.BlockSpec((d, tn), lambda i, j: (0, j)),
        ],
        out_specs=pl.BlockSpec((tm, tn), lambda i, j: (i, j)),
        out_shape=jax.ShapeDtypeStruct((t, n), F32),
        scratch_shapes=[pltpu.VMEM((tm, d), BF16)],
        compiler_params=pltpu.CompilerParams(
            dimension_semantics=("arbitrary", "arbitrary"), vmem_limit_bytes=VMEM_LIMIT),
        name="inproj",
    )(h, g.reshape(1, d), w)


def _pick_rows(x, m, first):
    j = lax.broadcasted_iota(jnp.int32, x.shape, 0)
    out = None
    for blk in range(8 // m - 1, -1, -1):
        r = blk * m + (0 if first else m - 1)
        row = jnp.broadcast_to(x[r:r + 1, :], x.shape)
        out = row if out is None else jnp.where(j < (blk + 1) * m, row, out)
    return out


def _expand_rows(x):
    return jnp.concatenate(
        [jnp.broadcast_to(x[j:j + 1, :], (8, x.shape[1])) for j in range(8)], axis=0)


def _dot_nt(a, b):
    return lax.dot_general(a, b, (((1,), (1,)), ((), ())), preferred_element_type=F32)


def _dot_tn(a, b):
    return lax.dot_general(a, b, (((0,), (0,)), ((), ())), preferred_element_type=F32)


def _hgrn2_kernel(p_ref, lb_ref, gain_ref, o_ref, st_ref):
    @pl.when(pl.program_id(1) == 0)
    def _():
        st_ref[...] = jnp.zeros_like(st_ref)

    q = p_ref[:, 0:D_HG]
    z = p_ref[:, D_HG:2 * D_HG]
    v = p_ref[:, 2 * D_HG:3 * D_HG].astype(BF16)
    lb = lb_ref[...]

    ez = jnp.exp2(jnp.abs(z) * (-LOG2E))
    rz = 1.0 / (1.0 + ez)
    sig_pos = jnp.where(z >= 0, rz, ez * rz)
    sig_neg = jnp.where(z >= 0, ez * rz, rz)
    gl = jnp.log2(jnp.maximum(lb + (1.0 - lb) * sig_pos, F_FLOOR))
    kk = (1.0 - lb) * sig_neg
    qf = q * _sigmoid(q)

    hi = gl.astype(BF16)
    r1 = gl - hi.astype(F32)
    mid = r1.astype(BF16)
    lo = (r1 - mid.astype(F32)).astype(BF16)
    nsel = CHUNK + 16
    row = lax.broadcasted_iota(jnp.int32, (nsel, 3 * CHUNK), 0)
    col = lax.broadcasted_iota(jnp.int32, (nsel, 3 * CHUNK), 1) & (CHUNK - 1)
    tgt = jnp.where(row < CHUNK, row,
                    jnp.where(row < CHUNK + 8, (row - CHUNK) * 8 + 7, (row - CHUNK - 8) * 8 + 3))
    tri3 = (tgt >= col).astype(BF16)
    cum = jnp.dot(tri3, jnp.concatenate([hi, mid, lo], axis=0), preferred_element_type=F32)
    b = cum[0:CHUNK]
    l8 = cum[CHUNK:CHUNK + 8]
    m8 = cum[CHUNK + 8:CHUNK + 16]
    j8 = lax.broadcasted_iota(jnp.int32, l8.shape, 0)
    s8 = jnp.where(j8 == 0, 0.0, pltpu.roll(l8, 1, 0))

    qe = qf * jnp.exp2(b - _expand_rows(s8))
    kf = kk * jnp.exp2(_expand_rows(l8) - b)
    qh = {8: qe.astype(BF16)}
    kh = {8: kf.astype(BF16)}
    for w, m in ((16, 2), (32, 4), (64, 8)):
        cq = jnp.exp2(s8 - _pick_rows(s8, m, True))
        ck = jnp.exp2(_pick_rows(l8, m, False) - l8)
        qh[w] = (qe * _expand_rows(cq)).astype(BF16)
        kh[w] = (kf * _expand_rows(ck)).astype(BF16)
    b_mid = _expand_rows(m8)
    qh[0] = (qf * jnp.exp2(b - b_mid)).astype(BF16)
    kh[0] = (kk * jnp.exp2(b_mid - b)).astype(BF16)
    dec_all = jnp.exp2(l8[7:8, :])

    ti = lax.broadcasted_iota(jnp.int32, (CHUNK, CHUNK), 0)
    si = lax.broadcasted_iota(jnp.int32, (CHUNK, CHUNK), 1)
    masks = {}
    for w, sh in ((32, 5), (16, 4), (8, 3)):
        masks[w] = ((ti >> sh) == (si >> sh) + 1) & ((ti >> (sh + 1)) == (si >> (sh + 1)))
    masks[0] = ((ti >> 3) == (si >> 3)) & (si <= ti)

    gate = p_ref[:, 3 * D_HG:4 * D_HG]
    gate = gate * _sigmoid(gate)
    gain = gain_ref[...]

    heads = [slice(h * HG_DK, (h + 1) * HG_DK) for h in range(HG_HEADS)]
    sts = [st_ref[h] for h in range(HG_HEADS)]
    o_inter = [_dot_nt(qh[64][:, sl], st.astype(BF16)) for sl, st in zip(heads, sts)]
    scores = []
    for sl in heads:
        sc = jnp.zeros((CHUNK, CHUNK), F32)
        for w in (32, 16, 8, 0):
            sc = jnp.where(masks[w], _dot_nt(qh[w][:, sl], kh[w][:, sl]), sc)
        scores.append(sc.astype(BF16))
    outs = [oi + jnp.dot(sc, v[:, sl], preferred_element_type=F32)
            for oi, sc, sl in zip(o_inter, scores, heads)]
    for h, (sl, st) in enumerate(zip(heads, sts)):
        st_ref[h] = dec_all[:, sl] * st + _dot_tn(v[:, sl], kh[64][:, sl])
    for o, sl in zip(outs, heads):
        ms = jnp.mean(o * o, axis=-1, keepdims=True)
        o = o * lax.rsqrt(ms + EPS) * gain[:, sl]
        o_ref[:, sl] = (o * gate[:, sl]).astype(BF16)


def _hgrn2(p3, lb, gain):
    bsz, lp, _ = p3.shape
    return pl.pallas_call(
        _hgrn2_kernel,
        grid=(bsz, lp // CHUNK),
        in_specs=[
            pl.BlockSpec((None, CHUNK, 4 * D_HG), lambda b, c: (b, c, 0)),
            pl.BlockSpec((1, D_HG), lambda b, c: (0, 0)),
            pl.BlockSpec((1, D_HG), lambda b, c: (0, 0)),
        ],
        out_specs=pl.BlockSpec((None, CHUNK, D_HG), lambda b, c: (b, c, 0)),
        out_shape=jax.ShapeDtypeStruct((bsz, lp, D_HG), BF16),
        scratch_shapes=[pltpu.VMEM((HG_HEADS, HG_DK, HG_DK), F32)],
        compiler_params=pltpu.CompilerParams(
            dimension_semantics=("arbitrary", "arbitrary"), vmem_limit_bytes=VMEM_LIMIT),
        name="hgrn2",
    )(p3, lb.reshape(1, D_HG), gain.reshape(1, D_HG))


def _gelu_tanh(x):
    return 0.5 * x * (1.0 + jnp.tanh(0.7978845608028654 * (x + 0.044715 * (x * x * x))))


def _s5core_kernel(u_ref, t_ref, h_ref, g_ref, a_ref, y_ref, v_ref, xp_ref, *, bsz):
    nchunks = u_ref.shape[0] // bsz
    u = u_ref[...]
    v_ref[...] = jnp.dot(u, h_ref[0], preferred_element_type=F32)
    a_re = jnp.broadcast_to(a_ref[0, 0:1, :], (bsz, 128))
    a_im = jnp.broadcast_to(a_ref[0, 1:2, :], (bsz, 128))

    def step(n, carry):
        x_re, x_im = carry
        rows = pl.ds(pl.multiple_of(n * bsz, bsz), bsz)
        xp_ref[rows, 0:128] = x_re.astype(BF16)
        xp_ref[rows, 128:256] = x_im.astype(BF16)
        return (a_re * x_re - a_im * x_im + v_ref[rows, 0:128],
                a_re * x_im + a_im * x_re + v_ref[rows, 128:256])

    zero = jnp.zeros((bsz, 128), F32)
    lax.fori_loop(0, nchunks, step, (zero, zero))

    ys = jnp.dot(xp_ref[...], g_ref[0], preferred_element_type=F32)
    for k in range(2):
        cols = slice(k * S5_K, (k + 1) * S5_K)
        y_ref[:, cols] = ys[:, cols] + jnp.dot(u[:, cols], t_ref[0, k], preferred_element_type=F32)


def _s5core(uc, tmat, hmat, gmat, a2, bsz):
    rows = uc.shape[0]
    return pl.pallas_call(
        functools.partial(_s5core_kernel, bsz=bsz),
        grid=(S5_PAIRS,),
        in_specs=[
            pl.BlockSpec((rows, 2 * S5_K), lambda p: (0, p)),
            pl.BlockSpec((1, 2, S5_K, S5_K), lambda p: (p, 0, 0, 0)),
            pl.BlockSpec((1, 2 * S5_K, 4 * S5_STATE), lambda p: (p, 0, 0)),
            pl.BlockSpec((1, 4 * S5_STATE, 2 * S5_K), lambda p: (p, 0, 0)),
            pl.BlockSpec((1, 2, 2 * S5_STATE), lambda p: (p, 0, 0)),
        ],
        out_specs=pl.BlockSpec((rows, 2 * S5_K), lambda p: (0, p)),
        out_shape=jax.ShapeDtypeStruct((rows, S5_GROUPS * S5_K), F32),
        scratch_shapes=[
            pltpu.VMEM((rows, 4 * S5_STATE), F32),
            pltpu.VMEM((rows, 4 * S5_STATE), BF16),
        ],
        compiler_params=pltpu.CompilerParams(
            dimension_semantics=("arbitrary",), vmem_limit_bytes=VMEM_LIMIT),
        name="s5core",
    )(uc, tmat, hmat, gmat, a2)


def _s5glu_kernel(y_ref, u_ref, d_ref, w_ref, b_ref, gain_ref, o_ref):
    y = _gelu_tanh(y_ref[...] + d_ref[...] * u_ref[...])
    g = jnp.dot(y.astype(BF16), w_ref[...], preferred_element_type=F32) + b_ref[...]
    o_ref[...] = _rms(y * _sigmoid(g), gain_ref[...]).astype(BF16)


def _s5glu(y, proj, d_skip, wglu, bglu, gain, tm=1024):
    t = y.shape[0]
    u_blk = 4 * D_HG // D_S5
    return pl.pallas_call(
        _s5glu_kernel,
        grid=(t // tm,),
        in_specs=[
            pl.BlockSpec((tm, D_S5), lambda i: (i, 0)),
            pl.BlockSpec((tm, D_S5), lambda i: (i, u_blk)),
            pl.BlockSpec((1, D_S5), lambda i: (0, 0)),
            pl.BlockSpec((D_S5, D_S5), lambda i: (0, 0)),
            pl.BlockSpec((1, D_S5), lambda i: (0, 0)),
            pl.BlockSpec((1, D_S5), lambda i: (0, 0)),
        ],
        out_specs=pl.BlockSpec((tm, D_S5), lambda i: (i, 0)),
        out_shape=jax.ShapeDtypeStruct((t, D_S5), BF16),
        compiler_params=pltpu.CompilerParams(
            dimension_semantics=("arbitrary",), vmem_limit_bytes=VMEM_LIMIT),
        name="s5glu",
    )(y, proj, d_skip.reshape(1, D_S5), wglu, bglu.reshape(1, D_S5), gain.reshape(1, D_S5))


def _to_chunk_rows(u3):
    b, l, _ = u3.shape
    n = l // S5_R
    x = u3.reshape(b, n, S5_R, S5_GROUPS, S5_GROUP).transpose(1, 0, 3, 2, 4)
    return x.reshape(n * b, S5_GROUPS * S5_K).astype(BF16)


def _from_chunk_rows(yc, b):
    n = yc.shape[0] // b
    y = yc.reshape(n, b, S5_GROUPS, S5_R, S5_GROUP).transpose(1, 0, 3, 2, 4)
    return y.reshape(b * n * S5_R, D_S5)


def _s5_params(lam_re, lam_im, log_step, b_re, b_im, c_re, c_im):
    hp = lax.Precision.HIGHEST
    a_re = jnp.minimum(lam_re.astype(F32), -1e-4)
    a_im = lam_im.astype(F32)
    dt = jnp.exp(log_step.astype(F32))[:, None]
    mag = jnp.exp(a_re * dt)
    ab_re = mag * jnp.cos(a_im * dt)
    ab_im = mag * jnp.sin(a_im * dt)
    den = a_re * a_re + a_im * a_im
    x_re, x_im = ab_re - 1.0, ab_im
    z_re = (x_re * a_re + x_im * a_im) / den
    z_im = (x_im * a_re - x_re * a_im) / den
    br, bi = b_re.astype(F32), b_im.astype(F32)
    bb_re = z_re[..., None] * br - z_im[..., None] * bi
    bb_im = z_re[..., None] * bi + z_im[..., None] * br
    cr, ci = c_re.astype(F32), c_im.astype(F32)

    pw_re, pw_im = [jnp.ones_like(ab_re)], [jnp.zeros_like(ab_re)]
    for _ in range(S5_R):
        r, i = pw_re[-1], pw_im[-1]
        pw_re.append(r * ab_re - i * ab_im)
        pw_im.append(r * ab_im + i * ab_re)
    p_re, p_im = jnp.stack(pw_re), jnp.stack(pw_im)

    q_re, q_im = p_re[:S5_R, :, :, None], p_im[:S5_R, :, :, None]
    m_re = q_re * bb_re - q_im * bb_im
    m_im = q_re * bb_im + q_im * bb_re
    kern = (jnp.einsum('gop,lgph->lgoh', cr, m_re, precision=hp)
            - jnp.einsum('gop,lgph->lgoh', ci, m_im, precision=hp))
    lag = jnp.arange(S5_R)[None, :] - jnp.arange(S5_R)[:, None]
    tk = jnp.where((lag >= 0)[:, :, None, None, None], kern[jnp.maximum(lag, 0)], 0.0)
    tmat = tk.transpose(2, 0, 4, 1, 3).reshape(S5_PAIRS, 2, S5_K, S5_K)

    def pair_rows(m):
        return m.reshape(S5_PAIRS, 2, S5_K, S5_STATE)

    h_re = pair_rows(m_re[::-1].transpose(1, 0, 3, 2).reshape(S5_GROUPS, S5_K, S5_STATE))
    h_im = pair_rows(m_im[::-1].transpose(1, 0, 3, 2).reshape(S5_GROUPS, S5_K, S5_STATE))
    zh = jnp.zeros_like(h_re[:, 0])
    hmat = jnp.concatenate([
        jnp.concatenate([h_re[:, 0], zh, h_im[:, 0], zh], axis=-1),
        jnp.concatenate([zh, h_re[:, 1], zh, h_im[:, 1]], axis=-1)], axis=1)

    e_re, e_im = p_re[1:, :, None, :], p_im[1:, :, None, :]
    g_re = (cr * e_re - ci * e_im).transpose(1, 3, 0, 2).reshape(S5_PAIRS, 2, S5_STATE, S5_K)
    g_im = -(cr * e_im + ci * e_re).transpose(1, 3, 0, 2).reshape(S5_PAIRS, 2, S5_STATE, S5_K)
    zg = jnp.zeros_like(g_re[:, 0])
    gmat = jnp.concatenate([
        jnp.concatenate([g_re[:, 0], zg], axis=-1), jnp.concatenate([zg, g_re[:, 1]], axis=-1),
        jnp.concatenate([g_im[:, 0], zg], axis=-1), jnp.concatenate([zg, g_im[:, 1]], axis=-1)],
        axis=1)

    a2 = jnp.stack([p_re[S5_R].reshape(S5_PAIRS, 2 * S5_STATE),
                    p_im[S5_R].reshape(S5_PAIRS, 2 * S5_STATE)], axis=1)
    return tmat.astype(BF16), hmat.astype(BF16), gmat.astype(BF16), a2


def _s5(proj, bsz, params, d_skip, wglu, bglu, gain):
    tmat, hmat, gmat, a2 = params
    lp = proj.shape[0] // bsz
    uc = _to_chunk_rows(proj[:, 4 * D_HG:].reshape(bsz, lp, D_S5))
    yc = _s5core(uc, tmat, hmat, gmat, a2, bsz)
    return _s5glu(_from_chunk_rows(yc, bsz), proj, d_skip, wglu, bglu, gain)


def _outproj_kernel(h_ref, a_ref, b_ref, w_ref, o_ref):
    o_ref[...] = (h_ref[...]
                  + jnp.dot(a_ref[...], w_ref[0], preferred_element_type=F32)
                  + jnp.dot(b_ref[...], w_ref[1], preferred_element_type=F32))


def _outproj(h, a, b, w2, tm=512):
    t, d = h.shape
    k = a.shape[1]
    return pl.pallas_call(
        _outproj_kernel,
        grid=(t // tm,),
        in_specs=[
            pl.BlockSpec((tm, d), lambda i: (i, 0)),
            pl.BlockSpec((tm, k), lambda i: (i, 0)),
            pl.BlockSpec((tm, k), lambda i: (i, 0)),
            pl.BlockSpec((2, k, d), lambda i: (0, 0, 0)),
        ],
        out_specs=pl.BlockSpec((tm, d), lambda i: (i, 0)),
        out_shape=jax.ShapeDtypeStruct((t, d), F32),
        compiler_params=pltpu.CompilerParams(
            dimension_semantics=("arbitrary",), vmem_limit_bytes=VMEM_LIMIT),
        name="outproj",
    )(h, a, b, w2)


FFN_HALO = 16


def _ffn_kernel(h_ref, halo_ref, g_ref, wg_ref, wu_ref, cw_ref, cb_ref, wd_ref, fg_ref, o_ref,
                hn_ref, *, final):
    f = pl.program_id(1)

    @pl.when(f == 0)
    def _():
        x = h_ref[...]
        hn_ref[FFN_HALO:, :] = _rms(x, g_ref[...]).astype(BF16)
        hn_ref[0:FFN_HALO, :] = _rms(halo_ref[...], g_ref[...]).astype(BF16)
        o_ref[...] = x

    a = jnp.dot(hn_ref[...], wg_ref[...], preferred_element_type=F32)
    up = jnp.dot(hn_ref[FFN_HALO:, :], wu_ref[...], preferred_element_type=F32)
    cw = cw_ref[...]
    conv = (cb_ref[...]
            + cw[0:1, :] * pltpu.roll(a, 2, 0)[FFN_HALO:, :]
            + cw[1:2, :] * pltpu.roll(a, 1, 0)[FFN_HALO:, :]
            + cw[2:3, :] * a[FFN_HALO:, :])
    hid = (conv * _sigmoid(conv) * up).astype(BF16)
    o_ref[...] += jnp.dot(hid, wd_ref[...], preferred_element_type=F32)

    if final:
        @pl.when(f == pl.num_programs(1) - 1)
        def _():
            o_ref[...] = _rms(o_ref[...], fg_ref[...])


def _ffn(h, g, wg, wu, cw, cb, wd, fg, final, tm=512, tf=512):
    t, d = h.shape
    ff = wg.shape[1]
    hb = tm // FFN_HALO
    return pl.pallas_call(
        functools.partial(_ffn_kernel, final=final),
        grid=(t // tm, ff // tf),
        in_specs=[
            pl.BlockSpec((tm, d), lambda i, f: (i, 0)),
            pl.BlockSpec((FFN_HALO, d), lambda i, f: (jnp.maximum(i * hb - 1, 0), 0)),
            pl.BlockSpec((1, d), lambda i, f: (0, 0)),
            pl.BlockSpec((d, tf), lambda i, f: (0, f)),
            pl.BlockSpec((d, tf), lambda i, f: (0, f)),
            pl.BlockSpec((3, tf), lambda i, f: (0, f)),
            pl.BlockSpec((1, tf), lambda i, f: (0, f)),
            pl.BlockSpec((tf, d), lambda i, f: (f, 0)),
            pl.BlockSpec((1, d), lambda i, f: (0, 0)),
        ],
        out_specs=pl.BlockSpec((tm, d), lambda i, f: (i, 0)),
        out_shape=jax.ShapeDtypeStruct((t, d), F32),
        scratch_shapes=[pltpu.VMEM((tm + FFN_HALO, d), BF16)],
        compiler_params=pltpu.CompilerParams(
            dimension_semantics=("arbitrary", "arbitrary"), vmem_limit_bytes=VMEM_LIMIT),
        name="ffn_final" if final else "ffn",
    )(h, h, g.reshape(1, d), wg, wu, cw, cb.reshape(1, ff), wd, fg.reshape(1, d))


def kernel(x, meta_tokens, lb_logits, norm_mix, w_in, hg_norm, s5_lambda_re, s5_lambda_im,
           s5_log_step, s5_b_re, s5_b_im, s5_c_re, s5_c_im, s5_d, w_glu, b_glu, s5_norm, w_out,
           norm_ffn, w_ffn_gate, w_ffn_up, ffn_conv_w, ffn_conv_b, w_ffn_down, final_norm):
    bsz, seq, d = x.shape
    lp = seq + CHUNK
    meta = jnp.broadcast_to(meta_tokens.astype(x.dtype)[None], (bsz, N_META, d))
    pad = jnp.zeros((bsz, SEQ_PAD, d), x.dtype)
    h = jnp.concatenate([pad, meta, x], axis=1).reshape(bsz * lp, d)

    sm = jax.nn.softmax(lb_logits.astype(F32), axis=0)
    lb_all = jnp.cumsum(sm, axis=0) - sm[0:1]

    for l in range(DEPTH):
        proj = _inproj(h, norm_mix[l], w_in[l].astype(BF16))
        p3 = proj.reshape(bsz, lp, D_IN)
        o_hg = _hgrn2(p3, lb_all[l], hg_norm[l])
        s5_params = _s5_params(s5_lambda_re[l], s5_lambda_im[l], s5_log_step[l],
                               s5_b_re[l], s5_b_im[l], s5_c_re[l], s5_c_im[l])
        o_s5 = _s5(proj, bsz, s5_params, s5_d[l], w_glu[l].astype(BF16), b_glu[l], s5_norm[l])
        h = _outproj(h, o_hg.reshape(bsz * lp, D_HG), o_s5,
                     w_out[l].astype(BF16).reshape(2, D_HG, d))
        h = _ffn(h, norm_ffn[l], w_ffn_gate[l].astype(BF16), w_ffn_up[l].astype(BF16),
                 ffn_conv_w[l], ffn_conv_b[l], w_ffn_down[l].astype(BF16), final_norm,
                 final=(l == DEPTH - 1))
    return h.reshape(bsz, lp, d)[:, CHUNK:, :]
```

```python
import functools

import jax
import jax.numpy as jnp
from jax import lax
from jax.experimental import pallas as pl
from jax.experimental.pallas import tpu as pltpu

F32 = jnp.float32
BF16 = jnp.bfloat16

D_MODEL = 2048
DEPTH = 2
CHUNK = 64
N_META = 16
D_HG = 1024
HG_HEADS = 8
HG_DK = 128
D_S5 = 1024
S5_GROUP = 16
S5_GROUPS = 64
S5_STATE = 64
D_IN = 4 * D_HG + D_S5
D_FF = 5632
EPS = 1e-6
F_FLOOR = 1e-6
LOG2E = 1.4426950408889634

SEQ_PAD = CHUNK - N_META
VMEM_LIMIT = 56 * 1024 * 1024

S5_R = 16
S5_K = S5_R * S5_GROUP
S5_PAIRS = S5_GROUPS // 2


def _rms(x, g):
    ms = jnp.mean(x * x, axis=-1, keepdims=True)
    return x * lax.rsqrt(ms + EPS) * g


def _sigmoid(x):
    return 1.0 / (1.0 + jnp.exp(-x))


def _inproj_kernel(h_ref, g_ref, w_ref, o_ref, xn_ref):
    @pl.when(pl.program_id(1) == 0)
    def _():
        xn_ref[...] = _rms(h_ref[...], g_ref[...]).astype(BF16)

    o_ref[...] = jnp.dot(xn_ref[...], w_ref[...], preferred_element_type=F32)


def _inproj(h, g, w, tm=1024, tn=1024):
    t, d = h.shape
    n = w.shape[1]
    return pl.pallas_call(
        _inproj_kernel,
        grid=(t // tm, n // tn),
        in_specs=[
            pl.BlockSpec((tm, d), lambda i, j: (i, 0)),
            pl.BlockSpec((1, d), lambda i, j: (0, 0)),
            pl.BlockSpec((d, tn), lambda i, j: (0, j)),
        ],
        out_specs=pl.BlockSpec((tm, tn), lambda i, j: (i, j)),
        out_shape=jax.ShapeDtypeStruct((t, n), F32),
        scratch_shapes=[pltpu.VMEM((tm, d), BF16)],
        compiler_params=pltpu.CompilerParams(
            dimension_semantics=("arbitrary", "arbitrary"), vmem_limit_bytes=VMEM_LIMIT),
        name="inproj",
    )(h, g.reshape(1, d), w)


def _pick_rows(x, m, first):
    j = lax.broadcasted_iota(jnp.int32, x.shape, 0)
    out = None
    for blk in range(8 // m - 1, -1, -1):
        r = blk * m + (0 if first else m - 1)
        row = jnp.broadcast_to(x[r:r + 1, :], x.shape)
        out = row if out is None else jnp.where(j < (blk + 1) * m, row, out)
    return out


def _expand_rows(x):
    return jnp.concatenate(
        [jnp.broadcast_to(x[j:j + 1, :], (8, x.shape[1])) for j in range(8)], axis=0)


def _dot_nt(a, b):
    return lax.dot_general(a, b, (((1,), (1,)), ((), ())), preferred_element_type=F32)


def _dot_tn(a, b):
    return lax.dot_general(a, b, (((0,), (0,)), ((), ())), preferred_element_type=F32)


def _hgrn2_kernel(p_ref, lb_ref, gain_ref, o_ref, st_ref):
    @pl.when(pl.program_id(1) == 0)
    def _():
        st_ref[...] = jnp.zeros_like(st_ref)

    q = p_ref[:, 0:D_HG]
    z = p_ref[:, D_HG:2 * D_HG]
    v = p_ref[:, 2 * D_HG:3 * D_HG].astype(BF16)
    lb = lb_ref[...]

    ez = jnp.exp2(jnp.abs(z) * (-LOG2E))
    rz = 1.0 / (1.0 + ez)
    sig_pos = jnp.where(z >= 0, rz, ez * rz)
    sig_neg = jnp.where(z >= 0, ez * rz, rz)
    gl = jnp.log2(jnp.maximum(lb + (1.0 - lb) * sig_pos, F_FLOOR))
    kk = (1.0 - lb) * sig_neg
    qf = q * _sigmoid(q)

    hi = gl.astype(BF16)
    r1 = gl - hi.astype(F32)
    mid = r1.astype(BF16)
    lo = (r1 - mid.astype(F32)).astype(BF16)
    nsel = CHUNK + 16
    row = lax.broadcasted_iota(jnp.int32, (nsel, 3 * CHUNK), 0)
    col = lax.broadcasted_iota(jnp.int32, (nsel, 3 * CHUNK), 1) & (CHUNK - 1)
    tgt = jnp.where(row < CHUNK, row,
                    jnp.where(row < CHUNK + 8, (row - CHUNK) * 8 + 7, (row - CHUNK - 8) * 8 + 3))
    tri3 = (tgt >= col).astype(BF16)
    cum = jnp.dot(tri3, jnp.concatenate([hi, mid, lo], axis=0), preferred_element_type=F32)
    b = cum[0:CHUNK]
    l8 = cum[CHUNK:CHUNK + 8]
    m8 = cum[CHUNK + 8:CHUNK + 16]
    j8 = lax.broadcasted_iota(jnp.int32, l8.shape, 0)
    s8 = jnp.where(j8 == 0, 0.0, pltpu.roll(l8, 1, 0))

    qe = qf * jnp.exp2(b - _expand_rows(s8))
    kf = kk * jnp.exp2(_expand_rows(l8) - b)
    qh = {8: qe.astype(BF16)}
    kh = {8: kf.astype(BF16)}
    for w, m in ((16, 2), (32, 4), (64, 8)):
        cq = jnp.exp2(s8 - _pick_rows(s8, m, True))
        ck = jnp.exp2(_pick_rows(l8, m, False) - l8)
        qh[w] = (qe * _expand_rows(cq)).astype(BF16)
        kh[w] = (kf * _expand_rows(ck)).astype(BF16)
    b_mid = _expand_rows(m8)
    qh[0] = (qf * jnp.exp2(b - b_mid)).astype(BF16)
    kh[0] = (kk * jnp.exp2(b_mid - b)).astype(BF16)
    dec_all = jnp.exp2(l8[7:8, :])

    ti = lax.broadcasted_iota(jnp.int32, (CHUNK, CHUNK), 0)
    si = lax.broadcasted_iota(jnp.int32, (CHUNK, CHUNK), 1)
    masks = {}
    for w, sh in ((32, 5), (16, 4), (8, 3)):
        masks[w] = ((ti >> sh) == (si >> sh) + 1) & ((ti >> (sh + 1)) == (si >> (sh + 1)))
    masks[0] = ((ti >> 3) == (si >> 3)) & (si <= ti)

    gate = p_ref[:, 3 * D_HG:4 * D_HG]
    gate = gate * _sigmoid(gate)
    gain = gain_ref[...]

    heads = [slice(h * HG_DK, (h + 1) * HG_DK) for h in range(HG_HEADS)]
    sts = [st_ref[h] for h in range(HG_HEADS)]
    o_inter = [_dot_nt(qh[64][:, sl], st.astype(BF16)) for sl, st in zip(heads, sts)]
    scores = []
    for sl in heads:
        sc = jnp.zeros((CHUNK, CHUNK), F32)
        for w in (32, 16, 8, 0):
            sc = jnp.where(masks[w], _dot_nt(qh[w][:, sl], kh[w][:, sl]), sc)
        scores.append(sc.astype(BF16))
    outs = [oi + jnp.dot(sc, v[:, sl], preferred_element_type=F32)
            for oi, sc, sl in zip(o_inter, scores, heads)]
    for h, (sl, st) in enumerate(zip(heads, sts)):
        st_ref[h] = dec_all[:, sl] * st + _dot_tn(v[:, sl], kh[64][:, sl])
    for o, sl in zip(outs, heads):
        ms = jnp.mean(o * o, axis=-1, keepdims=True)
        o = o * lax.rsqrt(ms + EPS) * gain[:, sl]
        o_ref[:, sl] = (o * gate[:, sl]).astype(BF16)


def _hgrn2(p3, lb, gain):
    bsz, lp, _ = p3.shape
    return pl.pallas_call(
        _hgrn2_kernel,
        grid=(bsz, lp // CHUNK),
        in_specs=[
            pl.BlockSpec((None, CHUNK, 4 * D_HG), lambda b, c: (b, c, 0)),
            pl.BlockSpec((1, D_HG), lambda b, c: (0, 0)),
            pl.BlockSpec((1, D_HG), lambda b, c: (0, 0)),
        ],
        out_specs=pl.BlockSpec((None, CHUNK, D_HG), lambda b, c: (b, c, 0)),
        out_shape=jax.ShapeDtypeStruct((bsz, lp, D_HG), BF16),
        scratch_shapes=[pltpu.VMEM((HG_HEADS, HG_DK, HG_DK), F32)],
        compiler_params=pltpu.CompilerParams(
            dimension_semantics=("arbitrary", "arbitrary"), vmem_limit_bytes=VMEM_LIMIT),
        name="hgrn2",
    )(p3, lb.reshape(1, D_HG), gain.reshape(1, D_HG))


def _gelu_tanh(x):
    return 0.5 * x * (1.0 + jnp.tanh(0.7978845608028654 * (x + 0.044715 * (x * x * x))))


def _s5core_kernel(u_ref, t_ref, h_ref, g_ref, a_ref, y_ref, v_ref, xp_ref, *, bsz):
    nchunks = u_ref.shape[0] // bsz
    u = u_ref[...]
    v_ref[...] = jnp.dot(u, h_ref[0], preferred_element_type=F32)
    a_re = jnp.broadcast_to(a_ref[0, 0:1, :], (bsz, 128))
    a_im = jnp.broadcast_to(a_ref[0, 1:2, :], (bsz, 128))

    def step(n, carry):
        x_re, x_im = carry
        rows = pl.ds(pl.multiple_of(n * bsz, bsz), bsz)
        xp_ref[rows, 0:128] = x_re.astype(BF16)
        xp_ref[rows, 128:256] = x_im.astype(BF16)
        return (a_re * x_re - a_im * x_im + v_ref[rows, 0:128],
                a_re * x_im + a_im * x_re + v_ref[rows, 128:256])

    zero = jnp.zeros((bsz, 128), F32)
    lax.fori_loop(0, nchunks, step, (zero, zero))

    ys = jnp.dot(xp_ref[...], g_ref[0], preferred_element_type=F32)
    for k in range(2):
        cols = slice(k * S5_K, (k + 1) * S5_K)
        y_ref[:, cols] = ys[:, cols] + jnp.dot(u[:, cols], t_ref[0, k], preferred_element_type=F32)


def _s5core(uc, tmat, hmat, gmat, a2, bsz):
    rows = uc.shape[0]
    return pl.pallas_call(
        functools.partial(_s5core_kernel, bsz=bsz),
        grid=(S5_PAIRS,),
        in_specs=[
            pl.BlockSpec((rows, 2 * S5_K), lambda p: (0, p)),
            pl.BlockSpec((1, 2, S5_K, S5_K), lambda p: (p, 0, 0, 0)),
            pl.BlockSpec((1, 2 * S5_K, 4 * S5_STATE), lambda p: (p, 0, 0)),
            pl.BlockSpec((1, 4 * S5_STATE, 2 * S5_K), lambda p: (p, 0, 0)),
            pl.BlockSpec((1, 2, 2 * S5_STATE), lambda p: (p, 0, 0)),
        ],
        out_specs=pl.BlockSpec((rows, 2 * S5_K), lambda p: (0, p)),
        out_shape=jax.ShapeDtypeStruct((rows, S5_GROUPS * S5_K), F32),
        scratch_shapes=[
            pltpu.VMEM((rows, 4 * S5_STATE), F32),
            pltpu.VMEM((rows, 4 * S5_STATE), BF16),
        ],
        compiler_params=pltpu.CompilerParams(
            dimension_semantics=("arbitrary",), vmem_limit_bytes=VMEM_LIMIT),
        name="s5core",
    )(uc, tmat, hmat, gmat, a2)


def _atom_transpose(groups):
    lane = lax.broadcasted_iota(jnp.int32, groups[0][0].shape, 1)
    groups = [list(xs) for xs in groups]
    for s in range(3):
        d = 1 << s
        keep = ((lane >> (4 + s)) & 1) == 0
        for xs in groups:
            for i in range(8):
                if i & d:
                    continue
                lo, hi = xs[i], xs[i + d]
                xs[i] = jnp.where(keep, lo, pltpu.roll(hi, S5_GROUP * d, 1))
                xs[i + d] = jnp.where(keep, pltpu.roll(lo, 128 - S5_GROUP * d, 1), hi)
    return groups


def _to_chunks_kernel(u_ref, o_ref, rows_ref):
    bsz, tt, _ = u_ref.shape
    pitch = rows_ref.shape[0] // bsz
    for b in range(bsz):
        rows_ref[b * pitch:b * pitch + tt, :] = u_ref[b]
    halves = [(n, k) for n in range(tt // S5_R) for k in range(2)]
    groups = _atom_transpose(
        [[rows_ref[pl.ds(n * S5_R + 8 * k + j, bsz, stride=pitch), :] for j in range(8)]
         for n, k in halves])
    for (n, k), ys in zip(halves, groups):
        for g, y in enumerate(ys):
            lane0 = g * S5_K + k * 128
            o_ref[n * bsz:(n + 1) * bsz, lane0:lane0 + 128] = y.astype(BF16)


def _to_chunks(p3, tt=176):
    bsz, lp, _ = p3.shape
    u_blk0 = 4 * D_HG // 128
    return pl.pallas_call(
        _to_chunks_kernel,
        grid=(lp // tt, D_S5 // 128),
        in_specs=[pl.BlockSpec((bsz, tt, 128), lambda i, v: (0, i, u_blk0 + v))],
        out_specs=pl.BlockSpec((tt, 8 * S5_K), lambda i, v: (i, v)),
        out_shape=jax.ShapeDtypeStruct((lp // S5_R * bsz, S5_GROUPS * S5_K), BF16),
        scratch_shapes=[pltpu.VMEM((bsz * (tt + 8), 128), F32)],
        compiler_params=pltpu.CompilerParams(
            dimension_semantics=("arbitrary", "arbitrary"), vmem_limit_bytes=VMEM_LIMIT),
        name="s5_to_chunks",
    )(p3)


def _s5glu_kernel(yc_ref, u_ref, d_ref, w_ref, b_ref, gain_ref, o_ref, yn_ref):
    bsz, tt, _ = u_ref.shape
    for n in range(tt // S5_R):
        halves = [(v, k) for v in range(D_S5 // 128) for k in range(2)]
        groups = _atom_transpose(
            [[yc_ref[n * bsz:(n + 1) * bsz,
                     (8 * v + g) * S5_K + k * 128:(8 * v + g) * S5_K + (k + 1) * 128]
              for g in range(8)] for v, k in halves])
        for (v, k), xs in zip(halves, groups):
            for j, x in enumerate(xs):
                yn_ref[v, pl.ds(n * S5_R + 8 * k + j, bsz, stride=tt), :] = x
    y = jnp.concatenate([yn_ref[v] for v in range(D_S5 // 128)], axis=1)
    y = _gelu_tanh(y + d_ref[...] * u_ref[...].reshape(bsz * tt, D_S5))
    g = jnp.dot(y.astype(BF16), w_ref[...], preferred_element_type=F32) + b_ref[...]
    o_ref[...] = _rms(y * _sigmoid(g), gain_ref[...]).reshape(bsz, tt, D_S5).astype(BF16)


def _s5glu(yc, p3, d_skip, wglu, bglu, gain, tt=48):
    bsz, lp, _ = p3.shape
    u_blk = 4 * D_HG // D_S5
    return pl.pallas_call(
        _s5glu_kernel,
        grid=(lp // tt,),
        in_specs=[
            pl.BlockSpec((tt // S5_R * bsz, S5_GROUPS * S5_K), lambda i: (i, 0)),
            pl.BlockSpec((bsz, tt, D_S5), lambda i: (0, i, u_blk)),
            pl.BlockSpec((1, D_S5), lambda i: (0, 0)),
            pl.BlockSpec((D_S5, D_S5), lambda i: (0, 0)),
            pl.BlockSpec((1, D_S5), lambda i: (0, 0)),
            pl.BlockSpec((1, D_S5), lambda i: (0, 0)),
        ],
        out_specs=pl.BlockSpec((bsz, tt, D_S5), lambda i: (0, i, 0)),
        out_shape=jax.ShapeDtypeStruct((bsz, lp, D_S5), BF16),
        scratch_shapes=[pltpu.VMEM((D_S5 // 128, bsz * tt, 128), F32)],
        compiler_params=pltpu.CompilerParams(
            dimension_semantics=("arbitrary",), vmem_limit_bytes=VMEM_LIMIT),
        name="s5glu",
    )(yc, p3, d_skip.reshape(1, D_S5), wglu, bglu.reshape(1, D_S5), gain.reshape(1, D_S5))


def _s5_params(lam_re, lam_im, log_step, b_re, b_im, c_re, c_im):
    hp = lax.Precision.HIGHEST
    a_re = jnp.minimum(lam_re.astype(F32), -1e-4)
    a_im = lam_im.astype(F32)
    dt = jnp.exp(log_step.astype(F32))[:, None]
    mag = jnp.exp(a_re * dt)
    ab_re = mag * jnp.cos(a_im * dt)
    ab_im = mag * jnp.sin(a_im * dt)
    den = a_re * a_re + a_im * a_im
    x_re, x_im = ab_re - 1.0, ab_im
    z_re = (x_re * a_re + x_im * a_im) / den
    z_im = (x_im * a_re - x_re * a_im) / den
    br, bi = b_re.astype(F32), b_im.astype(F32)
    bb_re = z_re[..., None] * br - z_im[..., None] * bi
    bb_im = z_re[..., None] * bi + z_im[..., None] * br
    cr, ci = c_re.astype(F32), c_im.astype(F32)

    pw_re, pw_im = [jnp.ones_like(ab_re)], [jnp.zeros_like(ab_re)]
    for _ in range(S5_R):
        r, i = pw_re[-1], pw_im[-1]
        pw_re.append(r * ab_re - i * ab_im)
        pw_im.append(r * ab_im + i * ab_re)
    p_re, p_im = jnp.stack(pw_re), jnp.stack(pw_im)

    q_re, q_im = p_re[:S5_R, :, :, None], p_im[:S5_R, :, :, None]
    m_re = q_re * bb_re - q_im * bb_im
    m_im = q_re * bb_im + q_im * bb_re
    kern = (jnp.einsum('gop,lgph->lgoh', cr, m_re, precision=hp)
            - jnp.einsum('gop,lgph->lgoh', ci, m_im, precision=hp))
    lag = jnp.arange(S5_R)[None, :] - jnp.arange(S5_R)[:, None]
    tk = jnp.where((lag >= 0)[:, :, None, None, None], kern[jnp.maximum(lag, 0)], 0.0)
    tmat = tk.transpose(2, 0, 4, 1, 3).reshape(S5_PAIRS, 2, S5_K, S5_K)

    def pair_rows(m):
        return m.reshape(S5_PAIRS, 2, S5_K, S5_STATE)

    h_re = pair_rows(m_re[::-1].transpose(1, 0, 3, 2).reshape(S5_GROUPS, S5_K, S5_STATE))
    h_im = pair_rows(m_im[::-1].transpose(1, 0, 3, 2).reshape(S5_GROUPS, S5_K, S5_STATE))
    zh = jnp.zeros_like(h_re[:, 0])
    hmat = jnp.concatenate([
        jnp.concatenate([h_re[:, 0], zh, h_im[:, 0], zh], axis=-1),
        jnp.concatenate([zh, h_re[:, 1], zh, h_im[:, 1]], axis=-1)], axis=1)

    e_re, e_im = p_re[1:, :, None, :], p_im[1:, :, None, :]
    g_re = (cr * e_re - ci * e_im).transpose(1, 3, 0, 2).reshape(S5_PAIRS, 2, S5_STATE, S5_K)
    g_im = -(cr * e_im + ci * e_re).transpose(1, 3, 0, 2).reshape(S5_PAIRS, 2, S5_STATE, S5_K)
    zg = jnp.zeros_like(g_re[:, 0])
    gmat = jnp.concatenate([
        jnp.concatenate([g_re[:, 0], zg], axis=-1), jnp.concatenate([zg, g_re[:, 1]], axis=-1),
        jnp.concatenate([g_im[:, 0], zg], axis=-1), jnp.concatenate([zg, g_im[:, 1]], axis=-1)],
        axis=1)

    a2 = jnp.stack([p_re[S5_R].reshape(S5_PAIRS, 2 * S5_STATE),
                    p_im[S5_R].reshape(S5_PAIRS, 2 * S5_STATE)], axis=1)
    return tmat.astype(BF16), hmat.astype(BF16), gmat.astype(BF16), a2


def _s5(p3, params, d_skip, wglu, bglu, gain):
    tmat, hmat, gmat, a2 = params
    yc = _s5core(_to_chunks(p3), tmat, hmat, gmat, a2, p3.shape[0])
    return _s5glu(yc, p3, d_skip, wglu, bglu, gain)


def _outproj_kernel(h_ref, a_ref, b_ref, w_ref, o_ref):
    o_ref[...] = (h_ref[...]
                  + jnp.dot(a_ref[...], w_ref[0], preferred_element_type=F32)
                  + jnp.dot(b_ref[...], w_ref[1], preferred_element_type=F32))


def _outproj(h, a, b, w2, tm=512):
    t, d = h.shape
    k = a.shape[1]
    return pl.pallas_call(
        _outproj_kernel,
        grid=(t // tm,),
        in_specs=[
            pl.BlockSpec((tm, d), lambda i: (i, 0)),
            pl.BlockSpec((tm, k), lambda i: (i, 0)),
            pl.BlockSpec((tm, k), lambda i: (i, 0)),
            pl.BlockSpec((2, k, d), lambda i: (0, 0, 0)),
        ],
        out_specs=pl.BlockSpec((tm, d), lambda i: (i, 0)),
        out_shape=jax.ShapeDtypeStruct((t, d), F32),
        compiler_params=pltpu.CompilerParams(
            dimension_semantics=("arbitrary",), vmem_limit_bytes=VMEM_LIMIT),
        name="outproj",
    )(h, a, b, w2)


FFN_HALO = 16


def _ffn_kernel(h_ref, halo_ref, g_ref, wg_ref, wu_ref, cw_ref, cb_ref, wd_ref, fg_ref, o_ref,
                hn_ref, *, final):
    f = pl.program_id(1)

    @pl.when(f == 0)
    def _():
        x = h_ref[...]
        hn_ref[FFN_HALO:, :] = _rms(x, g_ref[...]).astype(BF16)
        hn_ref[0:FFN_HALO, :] = _rms(halo_ref[...], g_ref[...]).astype(BF16)
        o_ref[...] = x

    a = jnp.dot(hn_ref[...], wg_ref[...], preferred_element_type=F32)
    up = jnp.dot(hn_ref[FFN_HALO:, :], wu_ref[...], preferred_element_type=F32)
    cw = cw_ref[...]
    conv = (cb_ref[...]
            + cw[0:1, :] * pltpu.roll(a, 2, 0)[FFN_HALO:, :]
            + cw[1:2, :] * pltpu.roll(a, 1, 0)[FFN_HALO:, :]
            + cw[2:3, :] * a[FFN_HALO:, :])
    hid = (conv * _sigmoid(conv) * up).astype(BF16)
    o_ref[...] += jnp.dot(hid, wd_ref[...], preferred_element_type=F32)

    if final:
        @pl.when(f == pl.num_programs(1) - 1)
        def _():
            o_ref[...] = _rms(o_ref[...], fg_ref[...])


def _ffn(h, g, wg, wu, cw, cb, wd, fg, final, tm=512, tf=512):
    t, d = h.shape
    ff = wg.shape[1]
    hb = tm // FFN_HALO
    return pl.pallas_call(
        functools.partial(_ffn_kernel, final=final),
        grid=(t // tm, ff // tf),
        in_specs=[
            pl.BlockSpec((tm, d), lambda i, f: (i, 0)),
            pl.BlockSpec((FFN_HALO, d), lambda i, f: (jnp.maximum(i * hb - 1, 0), 0)),
            pl.BlockSpec((1, d), lambda i, f: (0, 0)),
            pl.BlockSpec((d, tf), lambda i, f: (0, f)),
            pl.BlockSpec((d, tf), lambda i, f: (0, f)),
            pl.BlockSpec((3, tf), lambda i, f: (0, f)),
            pl.BlockSpec((1, tf), lambda i, f: (0, f)),
            pl.BlockSpec((tf, d), lambda i, f: (f, 0)),
            pl.BlockSpec((1, d), lambda i, f: (0, 0)),
        ],
        out_specs=pl.BlockSpec((tm, d), lambda i, f: (i, 0)),
        out_shape=jax.ShapeDtypeStruct((t, d), F32),
        scratch_shapes=[pltpu.VMEM((tm + FFN_HALO, d), BF16)],
        compiler_params=pltpu.CompilerParams(
            dimension_semantics=("arbitrary", "arbitrary"), vmem_limit_bytes=VMEM_LIMIT),
        name="ffn_final" if final else "ffn",
    )(h, h, g.reshape(1, d), wg, wu, cw, cb.reshape(1, ff), wd, fg.reshape(1, d))


def kernel(x, meta_tokens, lb_logits, norm_mix, w_in, hg_norm, s5_lambda_re, s5_lambda_im,
           s5_log_step, s5_b_re, s5_b_im, s5_c_re, s5_c_im, s5_d, w_glu, b_glu, s5_norm, w_out,
           norm_ffn, w_ffn_gate, w_ffn_up, ffn_conv_w, ffn_conv_b, w_ffn_down, final_norm):
    bsz, seq, d = x.shape
    lp = seq + CHUNK
    meta = jnp.broadcast_to(meta_tokens.astype(x.dtype)[None], (bsz, N_META, d))
    pad = jnp.zeros((bsz, SEQ_PAD, d), x.dtype)
    h = jnp.concatenate([pad, meta, x], axis=1).reshape(bsz * lp, d)

    sm = jax.nn.softmax(lb_logits.astype(F32), axis=0)
    lb_all = jnp.cumsum(sm, axis=0) - sm[0:1]

    for l in range(DEPTH):
        proj = _inproj(h, norm_mix[l], w_in[l].astype(BF16))
        p3 = proj.reshape(bsz, lp, D_IN)
        o_hg = _hgrn2(p3, lb_all[l], hg_norm[l])
        s5_params = _s5_params(s5_lambda_re[l], s5_lambda_im[l], s5_log_step[l],
                               s5_b_re[l], s5_b_im[l], s5_c_re[l], s5_c_im[l])
        o_s5 = _s5(p3, s5_params, s5_d[l], w_glu[l].astype(BF16), b_glu[l], s5_norm[l])
        h = _outproj(h, o_hg.reshape(bsz * lp, D_HG), o_s5.reshape(bsz * lp, D_S5),
                     w_out[l].astype(BF16).reshape(2, D_HG, d))
        h = _ffn(h, norm_ffn[l], w_ffn_gate[l].astype(BF16), w_ffn_up[l].astype(BF16),
                 ffn_conv_w[l], ffn_conv_b[l], w_ffn_down[l].astype(BF16), final_norm,
                 final=(l == DEPTH - 1))
    return h.reshape(bsz, lp, d)[:, CHUNK:, :]
```

```python
import functools

import jax
import jax.numpy as jnp
from jax import lax
from jax.experimental import pallas as pl
from jax.experimental.pallas import tpu as pltpu

F32 = jnp.float32
BF16 = jnp.bfloat16

D_MODEL = 2048
DEPTH = 2
CHUNK = 64
N_META = 16
D_HG = 1024
HG_HEADS = 8
HG_DK = 128
D_S5 = 1024
S5_GROUP = 16
S5_GROUPS = 64
S5_STATE = 64
D_IN = 4 * D_HG + D_S5
D_FF = 5632
EPS = 1e-6
F_FLOOR = 1e-6
LOG2E = 1.4426950408889634

SEQ_PAD = CHUNK - N_META
VMEM_LIMIT = 56 * 1024 * 1024

S5_R = 16
S5_K = S5_R * S5_GROUP
S5_PAIRS = S5_GROUPS // 2


def _rms(x, g):
    ms = jnp.mean(x * x, axis=-1, keepdims=True)
    return x * lax.rsqrt(ms + EPS) * g


def _sigmoid(x):
    return 1.0 / (1.0 + jnp.exp(-x))


def _inproj_kernel(h_ref, g_ref, w_ref, o_ref, xn_ref):
    @pl.when(pl.program_id(1) == 0)
    def _():
        xn_ref[...] = _rms(h_ref[...], g_ref[...]).astype(BF16)

    o_ref[...] = jnp.dot(xn_ref[...], w_ref[...], preferred_element_type=F32)


def _inproj(h, g, w, tm=1024, tn=1024):
    t, d = h.shape
    n = w.shape[1]
    return pl.pallas_call(
        _inproj_kernel,
        grid=(t // tm, n // tn),
        in_specs=[
            pl.BlockSpec((tm, d), lambda i, j: (i, 0)),
            pl.BlockSpec((1, d), lambda i, j: (0, 0)),
            pl.BlockSpec((d, tn), lambda i, j: (0, j)),
        ],
        out_specs=pl.BlockSpec((tm, tn), lambda i, j: (i, j)),
        out_shape=jax.ShapeDtypeStruct((t, n), F32),
        scratch_shapes=[pltpu.VMEM((tm, d), BF16)],
        compiler_params=pltpu.CompilerParams(
            dimension_semantics=("arbitrary", "arbitrary"), vmem_limit_bytes=VMEM_LIMIT),
        name="inproj",
    )(h, g.reshape(1, d), w)


def _pick_rows(x, m, first):
    j = lax.broadcasted_iota(jnp.int32, x.shape, 0)
    out = None
    for blk in range(8 // m - 1, -1, -1):
        r = blk * m + (0 if first else m - 1)
        row = jnp.broadcast_to(x[r:r + 1, :], x.shape)
        out = row if out is None else jnp.where(j < (blk + 1) * m, row, out)
    return out


def _expand_rows(x):
    return jnp.concatenate(
        [jnp.broadcast_to(x[j:j + 1, :], (8, x.shape[1])) for j in range(8)], axis=0)


def _dot_nt(a, b):
    return lax.dot_general(a, b, (((1,), (1,)), ((), ())), preferred_element_type=F32)


def _dot_tn(a, b):
    return lax.dot_general(a, b, (((0,), (0,)), ((), ())), preferred_element_type=F32)


def _hgrn2_kernel(p_ref, lb_ref, gain_ref, o_ref, st_ref):
    @pl.when(pl.program_id(1) == 0)
    def _():
        st_ref[...] = jnp.zeros_like(st_ref)

    q = p_ref[:, 0:D_HG]
    z = p_ref[:, D_HG:2 * D_HG]
    v = p_ref[:, 2 * D_HG:3 * D_HG].astype(BF16)
    lb = lb_ref[...]

    ez = jnp.exp2(jnp.abs(z) * (-LOG2E))
    rz = 1.0 / (1.0 + ez)
    sig_pos = jnp.where(z >= 0, rz, ez * rz)
    sig_neg = jnp.where(z >= 0, ez * rz, rz)
    gl = jnp.log2(jnp.maximum(lb + (1.0 - lb) * sig_pos, F_FLOOR))
    kk = (1.0 - lb) * sig_neg
    qf = q * _sigmoid(q)

    hi = gl.astype(BF16)
    r1 = gl - hi.astype(F32)
    mid = r1.astype(BF16)
    lo = (r1 - mid.astype(F32)).astype(BF16)
    nsel = CHUNK + 16
    row = lax.broadcasted_iota(jnp.int32, (nsel, 3 * CHUNK), 0)
    col = lax.broadcasted_iota(jnp.int32, (nsel, 3 * CHUNK), 1) & (CHUNK - 1)
    tgt = jnp.where(row < CHUNK, row,
                    jnp.where(row < CHUNK + 8, (row - CHUNK) * 8 + 7, (row - CHUNK - 8) * 8 + 3))
    tri3 = (tgt >= col).astype(BF16)
    cum = jnp.dot(tri3, jnp.concatenate([hi, mid, lo], axis=0), preferred_element_type=F32)
    b = cum[0:CHUNK]
    l8 = cum[CHUNK:CHUNK + 8]
    m8 = cum[CHUNK + 8:CHUNK + 16]
    j8 = lax.broadcasted_iota(jnp.int32, l8.shape, 0)
    s8 = jnp.where(j8 == 0, 0.0, pltpu.roll(l8, 1, 0))

    qe = qf * jnp.exp2(b - _expand_rows(s8))
    kf = kk * jnp.exp2(_expand_rows(l8) - b)
    qh = {8: qe.astype(BF16)}
    kh = {8: kf.astype(BF16)}
    for w, m in ((16, 2), (32, 4), (64, 8)):
        cq = jnp.exp2(s8 - _pick_rows(s8, m, True))
        ck = jnp.exp2(_pick_rows(l8, m, False) - l8)
        qh[w] = (qe * _expand_rows(cq)).astype(BF16)
        kh[w] = (kf * _expand_rows(ck)).astype(BF16)
    b_mid = _expand_rows(m8)
    qh[0] = (qf * jnp.exp2(b - b_mid)).astype(BF16)
    kh[0] = (kk * jnp.exp2(b_mid - b)).astype(BF16)
    dec_all = jnp.exp2(l8[7:8, :])

    ti = lax.broadcasted_iota(jnp.int32, (CHUNK, CHUNK), 0)
    si = lax.broadcasted_iota(jnp.int32, (CHUNK, CHUNK), 1)
    masks = {}
    for w, sh in ((32, 5), (16, 4), (8, 3)):
        masks[w] = ((ti >> sh) == (si >> sh) + 1) & ((ti >> (sh + 1)) == (si >> (sh + 1)))
    masks[0] = ((ti >> 3) == (si >> 3)) & (si <= ti)

    gate = p_ref[:, 3 * D_HG:4 * D_HG]
    gate = gate * _sigmoid(gate)
    gain = gain_ref[...]

    heads = [slice(h * HG_DK, (h + 1) * HG_DK) for h in range(HG_HEADS)]
    sts = [st_ref[h] for h in range(HG_HEADS)]
    o_inter = [_dot_nt(qh[64][:, sl], st.astype(BF16)) for sl, st in zip(heads, sts)]
    scores = []
    for sl in heads:
        sc = jnp.zeros((CHUNK, CHUNK), F32)
        for w in (32, 16, 8, 0):
            sc = jnp.where(masks[w], _dot_nt(qh[w][:, sl], kh[w][:, sl]), sc)
        scores.append(sc.astype(BF16))
    outs = [oi + jnp.dot(sc, v[:, sl], preferred_element_type=F32)
            for oi, sc, sl in zip(o_inter, scores, heads)]
    for h, (sl, st) in enumerate(zip(heads, sts)):
        st_ref[h] = dec_all[:, sl] * st + _dot_tn(v[:, sl], kh[64][:, sl])
    for o, sl in zip(outs, heads):
        ms = jnp.mean(o * o, axis=-1, keepdims=True)
        o = o * lax.rsqrt(ms + EPS) * gain[:, sl]
        o_ref[:, sl] = (o * gate[:, sl]).astype(BF16)


def _hgrn2(p3, lb, gain):
    bsz, lp, _ = p3.shape
    return pl.pallas_call(
        _hgrn2_kernel,
        grid=(bsz, lp // CHUNK),
        in_specs=[
            pl.BlockSpec((None, CHUNK, 4 * D_HG), lambda b, c: (b, c, 0)),
            pl.BlockSpec((1, D_HG), lambda b, c: (0, 0)),
            pl.BlockSpec((1, D_HG), lambda b, c: (0, 0)),
        ],
        out_specs=pl.BlockSpec((None, CHUNK, D_HG), lambda b, c: (b, c, 0)),
        out_shape=jax.ShapeDtypeStruct((bsz, lp, D_HG), BF16),
        scratch_shapes=[pltpu.VMEM((HG_HEADS, HG_DK, HG_DK), F32)],
        compiler_params=pltpu.CompilerParams(
            dimension_semantics=("arbitrary", "arbitrary"), vmem_limit_bytes=VMEM_LIMIT),
        name="hgrn2",
    )(p3, lb.reshape(1, D_HG), gain.reshape(1, D_HG))


def _gelu_tanh(x):
    return 0.5 * x * (1.0 + jnp.tanh(0.7978845608028654 * (x + 0.044715 * (x * x * x))))


def _s5core_kernel(u_ref, t_ref, h_ref, g_ref, a_ref, y_ref, v_ref, xp_ref, *, bsz):
    nchunks = u_ref.shape[0] // bsz
    u = u_ref[...]
    v_ref[...] = jnp.dot(u, h_ref[0], preferred_element_type=F32)
    a_re = jnp.broadcast_to(a_ref[0, 0:1, :], (bsz, 128))
    a_im = jnp.broadcast_to(a_ref[0, 1:2, :], (bsz, 128))

    def step(n, carry):
        x_re, x_im = carry
        rows = pl.ds(pl.multiple_of(n * bsz, bsz), bsz)
        xp_ref[rows, 0:128] = x_re.astype(BF16)
        xp_ref[rows, 128:256] = x_im.astype(BF16)
        return (a_re * x_re - a_im * x_im + v_ref[rows, 0:128],
                a_re * x_im + a_im * x_re + v_ref[rows, 128:256])

    zero = jnp.zeros((bsz, 128), F32)
    lax.fori_loop(0, nchunks, step, (zero, zero))

    ys = jnp.dot(xp_ref[...], g_ref[0], preferred_element_type=F32)
    for k in range(2):
        cols = slice(k * S5_K, (k + 1) * S5_K)
        y_ref[:, cols] = ys[:, cols] + jnp.dot(u[:, cols], t_ref[0, k], preferred_element_type=F32)


def _s5core(uc, tmat, hmat, gmat, a2, bsz):
    rows = uc.shape[0]
    return pl.pallas_call(
        functools.partial(_s5core_kernel, bsz=bsz),
        grid=(S5_PAIRS,),
        in_specs=[
            pl.BlockSpec((rows, 2 * S5_K), lambda p: (0, p)),
            pl.BlockSpec((1, 2, S5_K, S5_K), lambda p: (p, 0, 0, 0)),
            pl.BlockSpec((1, 2 * S5_K, 4 * S5_STATE), lambda p: (p, 0, 0)),
            pl.BlockSpec((1, 4 * S5_STATE, 2 * S5_K), lambda p: (p, 0, 0)),
            pl.BlockSpec((1, 2, 2 * S5_STATE), lambda p: (p, 0, 0)),
        ],
        out_specs=pl.BlockSpec((rows, 2 * S5_K), lambda p: (0, p)),
        out_shape=jax.ShapeDtypeStruct((rows, S5_GROUPS * S5_K), F32),
        scratch_shapes=[
            pltpu.VMEM((rows, 4 * S5_STATE), F32),
            pltpu.VMEM((rows, 4 * S5_STATE), BF16),
        ],
        compiler_params=pltpu.CompilerParams(
            dimension_semantics=("arbitrary",), vmem_limit_bytes=VMEM_LIMIT),
        name="s5core",
    )(uc, tmat, hmat, gmat, a2)


def _atom_transpose(groups):
    lane = lax.broadcasted_iota(jnp.int32, groups[0][0].shape, 1)
    groups = [list(xs) for xs in groups]
    for s in range(3):
        d = 1 << s
        keep = ((lane >> (4 + s)) & 1) == 0
        for xs in groups:
            for i in range(8):
                if i & d:
                    continue
                lo, hi = xs[i], xs[i + d]
                xs[i] = jnp.where(keep, lo, pltpu.roll(hi, S5_GROUP * d, 1))
                xs[i + d] = jnp.where(keep, pltpu.roll(lo, 128 - S5_GROUP * d, 1), hi)
    return groups


def _to_chunks_kernel(u_ref, o_ref, rows_ref):
    bsz, tt, _ = u_ref.shape
    pitch = rows_ref.shape[0] // bsz
    for b in range(bsz):
        rows_ref[b * pitch:b * pitch + tt, :] = u_ref[b]
    halves = [(n, k) for n in range(tt // S5_R) for k in range(2)]
    groups = _atom_transpose(
        [[rows_ref[pl.ds(n * S5_R + 8 * k + j, bsz, stride=pitch), :] for j in range(8)]
         for n, k in halves])
    for (n, k), ys in zip(halves, groups):
        for g, y in enumerate(ys):
            lane0 = g * S5_K + k * 128
            o_ref[n * bsz:(n + 1) * bsz, lane0:lane0 + 128] = y.astype(BF16)


def _to_chunks(p3, tt=176):
    bsz, lp, _ = p3.shape
    u_blk0 = 4 * D_HG // 128
    return pl.pallas_call(
        _to_chunks_kernel,
        grid=(lp // tt, D_S5 // 128),
        in_specs=[pl.BlockSpec((bsz, tt, 128), lambda i, v: (0, i, u_blk0 + v))],
        out_specs=pl.BlockSpec((tt, 8 * S5_K), lambda i, v: (i, v)),
        out_shape=jax.ShapeDtypeStruct((lp // S5_R * bsz, S5_GROUPS * S5_K), BF16),
        scratch_shapes=[pltpu.VMEM((bsz * (tt + 8), 128), F32)],
        compiler_params=pltpu.CompilerParams(
            dimension_semantics=("arbitrary", "arbitrary"), vmem_limit_bytes=VMEM_LIMIT),
        name="s5_to_chunks",
    )(p3)


def _s5glu_kernel(yc_ref, u_ref, d_ref, w_ref, b_ref, gain_ref, o_ref, yn_ref):
    bsz, tt, _ = u_ref.shape
    for n in range(tt // S5_R):
        halves = [(v, k) for v in range(D_S5 // 128) for k in range(2)]
        groups = _atom_transpose(
            [[yc_ref[n * bsz:(n + 1) * bsz,
                     (8 * v + g) * S5_K + k * 128:(8 * v + g) * S5_K + (k + 1) * 128]
              for g in range(8)] for v, k in halves])
        for (v, k), xs in zip(halves, groups):
            for j, x in enumerate(xs):
                yn_ref[v, pl.ds(n * S5_R + 8 * k + j, bsz, stride=tt), :] = x
    y = jnp.concatenate([yn_ref[v] for v in range(D_S5 // 128)], axis=1)
    y = _gelu_tanh(y + d_ref[...] * u_ref[...].reshape(bsz * tt, D_S5))
    g = jnp.dot(y.astype(BF16), w_ref[...], preferred_element_type=F32) + b_ref[...]
    o_ref[...] = _rms(y * _sigmoid(g), gain_ref[...]).reshape(bsz, tt, D_S5).astype(BF16)


def _s5glu(yc, p3, d_skip, wglu, bglu, gain, tt=48):
    bsz, lp, _ = p3.shape
    u_blk = 4 * D_HG // D_S5
    return pl.pallas_call(
        _s5glu_kernel,
        grid=(lp // tt,),
        in_specs=[
            pl.BlockSpec((tt // S5_R * bsz, S5_GROUPS * S5_K), lambda i: (i, 0)),
            pl.BlockSpec((bsz, tt, D_S5), lambda i: (0, i, u_blk)),
            pl.BlockSpec((1, D_S5), lambda i: (0, 0)),
            pl.BlockSpec((D_S5, D_S5), lambda i: (0, 0)),
            pl.BlockSpec((1, D_S5), lambda i: (0, 0)),
            pl.BlockSpec((1, D_S5), lambda i: (0, 0)),
        ],
        out_specs=pl.BlockSpec((bsz, tt, D_S5), lambda i: (0, i, 0)),
        out_shape=jax.ShapeDtypeStruct((bsz, lp, D_S5), BF16),
        scratch_shapes=[pltpu.VMEM((D_S5 // 128, bsz * tt, 128), F32)],
        compiler_params=pltpu.CompilerParams(
            dimension_semantics=("arbitrary",), vmem_limit_bytes=VMEM_LIMIT),
        name="s5glu",
    )(yc, p3, d_skip.reshape(1, D_S5), wglu, bglu.reshape(1, D_S5), gain.reshape(1, D_S5))


def _s5_params(lam_re, lam_im, log_step, b_re, b_im, c_re, c_im):
    hp = lax.Precision.HIGHEST
    a_re = jnp.minimum(lam_re.astype(F32), -1e-4)
    a_im = lam_im.astype(F32)
    dt = jnp.exp(log_step.astype(F32))[:, None]
    mag = jnp.exp(a_re * dt)
    ab_re = mag * jnp.cos(a_im * dt)
    ab_im = mag * jnp.sin(a_im * dt)
    den = a_re * a_re + a_im * a_im
    x_re, x_im = ab_re - 1.0, ab_im
    z_re = (x_re * a_re + x_im * a_im) / den
    z_im = (x_im * a_re - x_re * a_im) / den
    br, bi = b_re.astype(F32), b_im.astype(F32)
    bb_re = z_re[..., None] * br - z_im[..., None] * bi
    bb_im = z_re[..., None] * bi + z_im[..., None] * br
    cr, ci = c_re.astype(F32), c_im.astype(F32)

    pw_re, pw_im = [jnp.ones_like(ab_re)], [jnp.zeros_like(ab_re)]
    for _ in range(S5_R):
        r, i = pw_re[-1], pw_im[-1]
        pw_re.append(r * ab_re - i * ab_im)
        pw_im.append(r * ab_im + i * ab_re)
    p_re, p_im = jnp.stack(pw_re), jnp.stack(pw_im)

    q_re, q_im = p_re[:S5_R, :, :, None], p_im[:S5_R, :, :, None]
    m_re = q_re * bb_re - q_im * bb_im
    m_im = q_re * bb_im + q_im * bb_re
    kern = (jnp.einsum('gop,lgph->ghlo', cr, m_re, precision=hp)
            - jnp.einsum('gop,lgph->ghlo', ci, m_im, precision=hp))
    kz = jnp.concatenate([jnp.zeros((S5_GROUPS, S5_GROUP, S5_K), F32),
                          kern.reshape(S5_GROUPS, S5_GROUP, S5_K)], axis=-1)
    tmat = jnp.stack([kz[:, :, (S5_R - s) * S5_GROUP:(S5_R - s) * S5_GROUP + S5_K]
                      for s in range(S5_R)], axis=1).reshape(S5_PAIRS, 2, S5_K, S5_K)

    bt_re, bt_im = bb_re.transpose(0, 2, 1)[:, None], bb_im.transpose(0, 2, 1)[:, None]
    rp_re = jnp.moveaxis(p_re[S5_R - 1::-1], 0, 1)[:, :, None, :]
    rp_im = jnp.moveaxis(p_im[S5_R - 1::-1], 0, 1)[:, :, None, :]
    h_re = (rp_re * bt_re - rp_im * bt_im).reshape(S5_PAIRS, 2, S5_K, S5_STATE)
    h_im = (rp_re * bt_im + rp_im * bt_re).reshape(S5_PAIRS, 2, S5_K, S5_STATE)
    zh = jnp.zeros_like(h_re[:, 0])
    hmat = jnp.concatenate([
        jnp.concatenate([h_re[:, 0], zh, h_im[:, 0], zh], axis=-1),
        jnp.concatenate([zh, h_re[:, 1], zh, h_im[:, 1]], axis=-1)], axis=1)

    ct_re, ct_im = cr.transpose(0, 2, 1)[:, :, None, :], ci.transpose(0, 2, 1)[:, :, None, :]
    e_re = jnp.moveaxis(p_re[1:], 0, 2)[..., None]
    e_im = jnp.moveaxis(p_im[1:], 0, 2)[..., None]
    g_re = (ct_re * e_re - ct_im * e_im).reshape(S5_PAIRS, 2, S5_STATE, S5_K)
    g_im = -(ct_re * e_im + ct_im * e_re).reshape(S5_PAIRS, 2, S5_STATE, S5_K)
    zg = jnp.zeros_like(g_re[:, 0])
    gmat = jnp.concatenate([
        jnp.concatenate([g_re[:, 0], zg], axis=-1), jnp.concatenate([zg, g_re[:, 1]], axis=-1),
        jnp.concatenate([g_im[:, 0], zg], axis=-1), jnp.concatenate([zg, g_im[:, 1]], axis=-1)],
        axis=1)

    a2 = jnp.stack([p_re[S5_R].reshape(S5_PAIRS, 2 * S5_STATE),
                    p_im[S5_R].reshape(S5_PAIRS, 2 * S5_STATE)], axis=1)
    return tmat.astype(BF16), hmat.astype(BF16), gmat.astype(BF16), a2


def _s5(p3, params, d_skip, wglu, bglu, gain):
    tmat, hmat, gmat, a2 = params
    yc = _s5core(_to_chunks(p3), tmat, hmat, gmat, a2, p3.shape[0])
    return _s5glu(yc, p3, d_skip, wglu, bglu, gain)


def _outproj_kernel(h_ref, a_ref, b_ref, w_ref, o_ref):
    o_ref[...] = (h_ref[...]
                  + jnp.dot(a_ref[...], w_ref[0], preferred_element_type=F32)
                  + jnp.dot(b_ref[...], w_ref[1], preferred_element_type=F32))


def _outproj(h, a, b, w2, tm=512):
    t, d = h.shape
    k = a.shape[1]
    return pl.pallas_call(
        _outproj_kernel,
        grid=(t // tm,),
        in_specs=[
            pl.BlockSpec((tm, d), lambda i: (i, 0)),
            pl.BlockSpec((tm, k), lambda i: (i, 0)),
            pl.BlockSpec((tm, k), lambda i: (i, 0)),
            pl.BlockSpec((2, k, d), lambda i: (0, 0, 0)),
        ],
        out_specs=pl.BlockSpec((tm, d), lambda i: (i, 0)),
        out_shape=jax.ShapeDtypeStruct((t, d), F32),
        compiler_params=pltpu.CompilerParams(
            dimension_semantics=("arbitrary",), vmem_limit_bytes=VMEM_LIMIT),
        name="outproj",
    )(h, a, b, w2)


FFN_HALO = 16


def _ffn_kernel(h_ref, halo_ref, g_ref, wg_ref, wu_ref, cw_ref, cb_ref, wd_ref, fg_ref, o_ref,
                hn_ref, *, final):
    f_axis = 2 if final else 1
    f = pl.program_id(f_axis)

    @pl.when(f == 0)
    def _():
        x = h_ref[...]
        hn_ref[FFN_HALO:, :] = _rms(x, g_ref[...]).astype(BF16)
        hn_ref[0:FFN_HALO, :] = _rms(halo_ref[...], g_ref[...]).astype(BF16)
        o_ref[...] = x

    a = jnp.dot(hn_ref[...], wg_ref[...], preferred_element_type=F32)
    up = jnp.dot(hn_ref[FFN_HALO:, :], wu_ref[...], preferred_element_type=F32)
    cw = cw_ref[...]
    conv = (cb_ref[...]
            + cw[0:1, :] * pltpu.roll(a, 2, 0)[FFN_HALO:, :]
            + cw[1:2, :] * pltpu.roll(a, 1, 0)[FFN_HALO:, :]
            + cw[2:3, :] * a[FFN_HALO:, :])
    hid = (conv * _sigmoid(conv) * up).astype(BF16)
    o_ref[...] += jnp.dot(hid, wd_ref[...], preferred_element_type=F32)

    if final:
        @pl.when(f == pl.num_programs(f_axis) - 1)
        def _():
            o_ref[...] = _rms(o_ref[...], fg_ref[...])


def _ffn(h, g, wg, wu, cw, cb, wd, fg, final, bsz, tm=512, tf=512):
    t, d = h.shape
    ff = wg.shape[1]
    hb = tm // FFN_HALO
    lp = t // bsz
    if final:
        grid = (bsz, (lp - CHUNK) // tm, ff // tf)
        row0 = lambda b, j, back: pl.multiple_of(b * lp + CHUNK + j * tm - back, FFN_HALO)
        row_specs = [
            pl.BlockSpec((pl.Element(tm), pl.Element(d)), lambda b, j, f: (row0(b, j, 0), 0)),
            pl.BlockSpec((pl.Element(FFN_HALO), pl.Element(d)),
                         lambda b, j, f: (row0(b, j, FFN_HALO), 0)),
        ]
        out_spec = pl.BlockSpec((tm, d), lambda b, j, f: (b * ((lp - CHUNK) // tm) + j, 0))
        out_rows = bsz * (lp - CHUNK)
        wmap = lambda fn: (lambda b, j, f: fn(f))
    else:
        grid = (t // tm, ff // tf)
        row_specs = [
            pl.BlockSpec((tm, d), lambda i, f: (i, 0)),
            pl.BlockSpec((FFN_HALO, d), lambda i, f: (jnp.maximum(i * hb - 1, 0), 0)),
        ]
        out_spec = pl.BlockSpec((tm, d), lambda i, f: (i, 0))
        out_rows = t
        wmap = lambda fn: (lambda i, f: fn(f))
    return pl.pallas_call(
        functools.partial(_ffn_kernel, final=final),
        grid=grid,
        in_specs=row_specs + [
            pl.BlockSpec((1, d), wmap(lambda f: (0, 0))),
            pl.BlockSpec((d, tf), wmap(lambda f: (0, f))),
            pl.BlockSpec((d, tf), wmap(lambda f: (0, f))),
            pl.BlockSpec((3, tf), wmap(lambda f: (0, f))),
            pl.BlockSpec((1, tf), wmap(lambda f: (0, f))),
            pl.BlockSpec((tf, d), wmap(lambda f: (f, 0))),
            pl.BlockSpec((1, d), wmap(lambda f: (0, 0))),
        ],
        out_specs=out_spec,
        out_shape=jax.ShapeDtypeStruct((out_rows, d), F32),
        scratch_shapes=[pltpu.VMEM((tm + FFN_HALO, d), BF16)],
        compiler_params=pltpu.CompilerParams(
            dimension_semantics=("arbitrary",) * len(grid), vmem_limit_bytes=VMEM_LIMIT),
        name="ffn_final" if final else "ffn",
    )(h, h, g.reshape(1, d), wg, wu, cw, cb.reshape(1, ff), wd, fg.reshape(1, d))


def kernel(x, meta_tokens, lb_logits, norm_mix, w_in, hg_norm, s5_lambda_re, s5_lambda_im,
           s5_log_step, s5_b_re, s5_b_im, s5_c_re, s5_c_im, s5_d, w_glu, b_glu, s5_norm, w_out,
           norm_ffn, w_ffn_gate, w_ffn_up, ffn_conv_w, ffn_conv_b, w_ffn_down, final_norm):
    bsz, seq, d = x.shape
    lp = seq + CHUNK
    meta = jnp.broadcast_to(meta_tokens.astype(x.dtype)[None], (bsz, N_META, d))
    pad = jnp.zeros((bsz, SEQ_PAD, d), x.dtype)
    h = jnp.concatenate([pad, meta, x], axis=1).reshape(bsz * lp, d)

    sm = jax.nn.softmax(lb_logits.astype(F32), axis=0)
    lb_all = jnp.cumsum(sm, axis=0) - sm[0:1]

    for l in range(DEPTH):
        proj = _inproj(h, norm_mix[l], w_in[l].astype(BF16))
        p3 = proj.reshape(bsz, lp, D_IN)
        o_hg = _hgrn2(p3, lb_all[l], hg_norm[l])
        s5_params = _s5_params(s5_lambda_re[l], s5_lambda_im[l], s5_log_step[l],
                               s5_b_re[l], s5_b_im[l], s5_c_re[l], s5_c_im[l])
        o_s5 = _s5(p3, s5_params, s5_d[l], w_glu[l].astype(BF16), b_glu[l], s5_norm[l])
        h = _outproj(h, o_hg.reshape(bsz * lp, D_HG), o_s5.reshape(bsz * lp, D_S5),
                     w_out[l].astype(BF16).reshape(2, D_HG, d))
        h = _ffn(h, norm_ffn[l], w_ffn_gate[l].astype(BF16), w_ffn_up[l].astype(BF16),
                 ffn_conv_w[l], ffn_conv_b[l], w_ffn_down[l].astype(BF16), final_norm,
                 final=(l == DEPTH - 1), bsz=bsz)
    return h.reshape(bsz, seq, d)
```

```python
import functools

import jax
import jax.numpy as jnp
from jax import lax
from jax.experimental import pallas as pl
from jax.experimental.pallas import tpu as pltpu

F32 = jnp.float32
BF16 = jnp.bfloat16

D_MODEL = 2048
DEPTH = 2
CHUNK = 64
N_META = 16
D_HG = 1024
HG_HEADS = 8
HG_DK = 128
D_S5 = 1024
S5_GROUP = 16
S5_GROUPS = 64
S5_STATE = 64
D_IN = 4 * D_HG + D_S5
D_FF = 5632
EPS = 1e-6
F_FLOOR = 1e-6
LOG2E = 1.4426950408889634

SEQ_PAD = CHUNK - N_META
VMEM_LIMIT = 56 * 1024 * 1024

S5_R = 16
S5_K = S5_R * S5_GROUP
S5_PAIRS = S5_GROUPS // 2


def _rms(x, g):
    ms = jnp.mean(x * x, axis=-1, keepdims=True)
    return x * lax.rsqrt(ms + EPS) * g


def _sigmoid(x):
    return 1.0 / (1.0 + jnp.exp(-x))


def _inproj_kernel(h_ref, g_ref, w_ref, o_ref, xn_ref):
    @pl.when(pl.program_id(1) == 0)
    def _():
        xn_ref[...] = _rms(h_ref[...], g_ref[...]).astype(BF16)

    o_ref[...] = jnp.dot(xn_ref[...], w_ref[...], preferred_element_type=F32)


def _inproj(h, g, w, tm=1024, tn=1024):
    t, d = h.shape
    n = w.shape[1]
    return pl.pallas_call(
        _inproj_kernel,
        grid=(t // tm, n // tn),
        in_specs=[
            pl.BlockSpec((tm, d), lambda i, j: (i, 0)),
            pl.BlockSpec((1, d), lambda i, j: (0, 0)),
            pl.BlockSpec((d, tn), lambda i, j: (0, j)),
        ],
        out_specs=pl.BlockSpec((tm, tn), lambda i, j: (i, j)),
        out_shape=jax.ShapeDtypeStruct((t, n), F32),
        scratch_shapes=[pltpu.VMEM((tm, d), BF16)],
        compiler_params=pltpu.CompilerParams(
            dimension_semantics=("arbitrary", "arbitrary"), vmem_limit_bytes=VMEM_LIMIT),
        name="inproj",
    )(h, g.reshape(1, d), w)


def _pick_rows(x, m, first):
    j = lax.broadcasted_iota(jnp.int32, x.shape, 0)
    out = None
    for blk in range(8 // m - 1, -1, -1):
        r = blk * m + (0 if first else m - 1)
        row = jnp.broadcast_to(x[r:r + 1, :], x.shape)
        out = row if out is None else jnp.where(j < (blk + 1) * m, row, out)
    return out


def _expand_rows(ref, i, x):
    nv = x.shape[1] // 128
    for v in range(nv):
        ref[i, v] = x[:, v * 128:(v + 1) * 128]
    return jnp.concatenate(
        [jnp.concatenate([ref[i, v, pl.ds(j, 8, stride=0), :] for v in range(nv)], axis=1)
         for j in range(8)], axis=0)


def _dot_nt(a, b):
    return lax.dot_general(a, b, (((1,), (1,)), ((), ())), preferred_element_type=F32)


def _dot_tn(a, b):
    return lax.dot_general(a, b, (((0,), (0,)), ((), ())), preferred_element_type=F32)


def _hgrn2_kernel(p_ref, lb_ref, gain_ref, o_ref, st_ref, bc_ref):
    @pl.when(pl.program_id(1) == 0)
    def _():
        st_ref[...] = jnp.zeros_like(st_ref)

    def chunk(c, carry):
        _hgrn2_chunk(p_ref, lb_ref, gain_ref, o_ref, st_ref, bc_ref,
                     pl.ds(pl.multiple_of(c * CHUNK, CHUNK), CHUNK))
        return carry

    lax.fori_loop(0, p_ref.shape[0] // CHUNK, chunk, 0)


def _hgrn2_chunk(p_ref, lb_ref, gain_ref, o_ref, st_ref, bc_ref, rows):
    q = p_ref[rows, 0:D_HG]
    z = p_ref[rows, D_HG:2 * D_HG]
    v = p_ref[rows, 2 * D_HG:3 * D_HG].astype(BF16)
    lb = lb_ref[...]

    ez = jnp.exp2(jnp.abs(z) * (-LOG2E))
    rz = 1.0 / (1.0 + ez)
    sig_pos = jnp.where(z >= 0, rz, ez * rz)
    sig_neg = jnp.where(z >= 0, ez * rz, rz)
    gl = jnp.log2(jnp.maximum(lb + (1.0 - lb) * sig_pos, F_FLOOR))
    kk = (1.0 - lb) * sig_neg
    qf = q * _sigmoid(q)

    hi = gl.astype(BF16)
    r1 = gl - hi.astype(F32)
    mid = r1.astype(BF16)
    lo = (r1 - mid.astype(F32)).astype(BF16)
    nsel = CHUNK + 16
    row = lax.broadcasted_iota(jnp.int32, (nsel, 3 * CHUNK), 0)
    col = lax.broadcasted_iota(jnp.int32, (nsel, 3 * CHUNK), 1) & (CHUNK - 1)
    tgt = jnp.where(row < CHUNK, row,
                    jnp.where(row < CHUNK + 8, (row - CHUNK) * 8 + 7, (row - CHUNK - 8) * 8 + 3))
    tri3 = (tgt >= col).astype(BF16)
    cum = jnp.dot(tri3, jnp.concatenate([hi, mid, lo], axis=0), preferred_element_type=F32)
    b = cum[0:CHUNK]
    l8 = cum[CHUNK:CHUNK + 8]
    m8 = cum[CHUNK + 8:CHUNK + 16]
    j8 = lax.broadcasted_iota(jnp.int32, l8.shape, 0)
    s8 = jnp.where(j8 == 0, 0.0, pltpu.roll(l8, 1, 0))

    qe = qf * jnp.exp2(b - _expand_rows(bc_ref, 0, s8))
    kf = kk * jnp.exp2(_expand_rows(bc_ref, 1, l8) - b)
    qh = {8: qe.astype(BF16)}
    kh = {8: kf.astype(BF16)}
    for lvl, (w, m) in enumerate(((16, 2), (32, 4), (64, 8))):
        cq = jnp.exp2(s8 - _pick_rows(s8, m, True))
        ck = jnp.exp2(_pick_rows(l8, m, False) - l8)
        qh[w] = (qe * _expand_rows(bc_ref, 3 + 2 * lvl, cq)).astype(BF16)
        kh[w] = (kf * _expand_rows(bc_ref, 4 + 2 * lvl, ck)).astype(BF16)
    b_mid = _expand_rows(bc_ref, 2, m8)
    qh[0] = (qf * jnp.exp2(b - b_mid)).astype(BF16)
    kh[0] = (kk * jnp.exp2(b_mid - b)).astype(BF16)
    dec_all = jnp.exp2(l8[7:8, :])

    ti = lax.broadcasted_iota(jnp.int32, (CHUNK, CHUNK), 0)
    si = lax.broadcasted_iota(jnp.int32, (CHUNK, CHUNK), 1)
    masks = {}
    for w, sh in ((32, 5), (16, 4), (8, 3)):
        masks[w] = ((ti >> sh) == (si >> sh) + 1) & ((ti >> (sh + 1)) == (si >> (sh + 1)))
    masks[0] = ((ti >> 3) == (si >> 3)) & (si <= ti)

    gate = p_ref[rows, 3 * D_HG:4 * D_HG]
    gate = gate * _sigmoid(gate)
    gain = gain_ref[...]

    heads = [slice(h * HG_DK, (h + 1) * HG_DK) for h in range(HG_HEADS)]
    sts = [st_ref[h] for h in range(HG_HEADS)]
    o_inter = [_dot_nt(qh[64][:, sl], st.astype(BF16)) for sl, st in zip(heads, sts)]
    scores = []
    for sl in heads:
        sc = jnp.zeros((CHUNK, CHUNK), F32)
        for w in (32, 16, 8, 0):
            sc = jnp.where(masks[w], _dot_nt(qh[w][:, sl], kh[w][:, sl]), sc)
        scores.append(sc.astype(BF16))
    outs = [oi + jnp.dot(sc, v[:, sl], preferred_element_type=F32)
            for oi, sc, sl in zip(o_inter, scores, heads)]
    for h, (sl, st) in enumerate(zip(heads, sts)):
        st_ref[h] = dec_all[:, sl] * st + _dot_tn(v[:, sl], kh[64][:, sl])
    for o, sl in zip(outs, heads):
        ms = jnp.mean(o * o, axis=-1, keepdims=True)
        o = o * lax.rsqrt(ms + EPS) * gain[:, sl]
        o_ref[rows, sl] = (o * gate[:, sl]).astype(BF16)


def _hgrn2(p3, lb, gain, chunks_per_step=11):
    bsz, lp, _ = p3.shape
    rows = chunks_per_step * CHUNK
    return pl.pallas_call(
        _hgrn2_kernel,
        grid=(bsz, lp // rows),
        in_specs=[
            pl.BlockSpec((None, rows, 4 * D_HG), lambda b, c: (b, c, 0)),
            pl.BlockSpec((1, D_HG), lambda b, c: (0, 0)),
            pl.BlockSpec((1, D_HG), lambda b, c: (0, 0)),
        ],
        out_specs=pl.BlockSpec((None, rows, D_HG), lambda b, c: (b, c, 0)),
        out_shape=jax.ShapeDtypeStruct((bsz, lp, D_HG), BF16),
        scratch_shapes=[
            pltpu.VMEM((HG_HEADS, HG_DK, HG_DK), F32),
            pltpu.VMEM((9, D_HG // 128, 8, 128), F32),
        ],
        compiler_params=pltpu.CompilerParams(
            dimension_semantics=("arbitrary", "arbitrary"), vmem_limit_bytes=VMEM_LIMIT),
        name="hgrn2",
    )(p3, lb.reshape(1, D_HG), gain.reshape(1, D_HG))


def _gelu_tanh(x):
    c = 0.7978845608028654
    half = 0.5 * x
    return half + half * jnp.tanh(x * (c + (c * 0.044715) * (x * x)))


def _s5core_kernel(u_ref, t_ref, h_ref, g_ref, a_ref, y_ref, v_ref, xp_ref, *, bsz):
    nchunks = u_ref.shape[0] // bsz
    u = u_ref[...]
    v_ref[...] = jnp.dot(u, h_ref[0], preferred_element_type=F32)
    a_re = jnp.broadcast_to(a_ref[0, 0:1, :], (bsz, 128))
    a_im = jnp.broadcast_to(a_ref[0, 1:2, :], (bsz, 128))

    def step(n, carry):
        x_re, x_im = carry
        rows = pl.ds(pl.multiple_of(n * bsz, bsz), bsz)
        xp_ref[rows, 0:128] = x_re.astype(BF16)
        xp_ref[rows, 128:256] = x_im.astype(BF16)
        return (a_re * x_re - a_im * x_im + v_ref[rows, 0:128],
                a_re * x_im + a_im * x_re + v_ref[rows, 128:256])

    zero = jnp.zeros((bsz, 128), F32)
    lax.fori_loop(0, nchunks, step, (zero, zero))

    ys = jnp.dot(xp_ref[...], g_ref[0], preferred_element_type=F32)
    for k in range(2):
        cols = slice(k * S5_K, (k + 1) * S5_K)
        y_ref[:, cols] = ys[:, cols] + jnp.dot(u[:, cols], t_ref[0, k], preferred_element_type=F32)


def _s5core(uc, tmat, hmat, gmat, a2, bsz):
    rows = uc.shape[0]
    return pl.pallas_call(
        functools.partial(_s5core_kernel, bsz=bsz),
        grid=(S5_PAIRS,),
        in_specs=[
            pl.BlockSpec((rows, 2 * S5_K), lambda p: (0, p)),
            pl.BlockSpec((1, 2, S5_K, S5_K), lambda p: (p, 0, 0, 0)),
            pl.BlockSpec((1, 2 * S5_K, 4 * S5_STATE), lambda p: (p, 0, 0)),
            pl.BlockSpec((1, 4 * S5_STATE, 2 * S5_K), lambda p: (p, 0, 0)),
            pl.BlockSpec((1, 2, 2 * S5_STATE), lambda p: (p, 0, 0)),
        ],
        out_specs=pl.BlockSpec((rows, 2 * S5_K), lambda p: (0, p)),
        out_shape=jax.ShapeDtypeStruct((rows, S5_GROUPS * S5_K), F32),
        scratch_shapes=[
            pltpu.VMEM((rows, 4 * S5_STATE), F32),
            pltpu.VMEM((rows, 4 * S5_STATE), BF16),
        ],
        compiler_params=pltpu.CompilerParams(
            dimension_semantics=("arbitrary",), vmem_limit_bytes=VMEM_LIMIT),
        name="s5core",
    )(uc, tmat, hmat, gmat, a2)


def _atom_transpose(groups):
    lane = lax.broadcasted_iota(jnp.int32, groups[0][0].shape, 1)
    groups = [list(xs) for xs in groups]
    for s in range(3):
        d = 1 << s
        keep = ((lane >> (4 + s)) & 1) == 0
        for xs in groups:
            for i in range(8):
                if i & d:
                    continue
                lo, hi = xs[i], xs[i + d]
                xs[i] = jnp.where(keep, lo, pltpu.roll(hi, S5_GROUP * d, 1))
                xs[i + d] = jnp.where(keep, pltpu.roll(lo, 128 - S5_GROUP * d, 1), hi)
    return groups


def _to_chunks_kernel(u_ref, o_ref, rows_ref):
    bsz, tt, width = u_ref.shape
    pitch = rows_ref.shape[1] // bsz
    for v in range(width // 128):
        for b in range(bsz):
            rows_ref[v, b * pitch:b * pitch + tt, :] = u_ref[b, :, v * 128:(v + 1) * 128]
    halves = [(v, n, k) for v in range(width // 128) for n in range(tt // S5_R) for k in range(2)]
    groups = _atom_transpose(
        [[rows_ref[v, pl.ds(n * S5_R + 8 * k + j, bsz, stride=pitch), :] for j in range(8)]
         for v, n, k in halves])
    for (v, n, k), ys in zip(halves, groups):
        for g, y in enumerate(ys):
            lane0 = (8 * v + g) * S5_K + k * 128
            o_ref[n * bsz:(n + 1) * bsz, lane0:lane0 + 128] = y.astype(BF16)


def _to_chunks(p3, tt=176, width=512):
    bsz, lp, _ = p3.shape
    u_blk0 = 4 * D_HG // width
    return pl.pallas_call(
        _to_chunks_kernel,
        grid=(lp // tt, D_S5 // width),
        in_specs=[pl.BlockSpec((bsz, tt, width), lambda i, v: (0, i, u_blk0 + v))],
        out_specs=pl.BlockSpec((tt, width // S5_GROUP * S5_K), lambda i, v: (i, v)),
        out_shape=jax.ShapeDtypeStruct((lp // S5_R * bsz, S5_GROUPS * S5_K), BF16),
        scratch_shapes=[pltpu.VMEM((width // 128, bsz * (tt + 8), 128), F32)],
        compiler_params=pltpu.CompilerParams(
            dimension_semantics=("arbitrary", "arbitrary"), vmem_limit_bytes=VMEM_LIMIT),
        name="s5_to_chunks",
    )(p3)


def _s5glu_kernel(yc_ref, u_ref, d_ref, w_ref, b_ref, gain_ref, o_ref, yn_ref):
    bsz, tt, _ = u_ref.shape
    for n in range(tt // S5_R):
        halves = [(v, k) for v in range(D_S5 // 128) for k in range(2)]
        groups = _atom_transpose(
            [[yc_ref[n * bsz:(n + 1) * bsz,
                     (8 * v + g) * S5_K + k * 128:(8 * v + g) * S5_K + (k + 1) * 128]
              for g in range(8)] for v, k in halves])
        for (v, k), xs in zip(halves, groups):
            for j, x in enumerate(xs):
                yn_ref[v, pl.ds(n * S5_R + 8 * k + j, bsz, stride=tt), :] = x
    y = jnp.concatenate([yn_ref[v] for v in range(D_S5 // 128)], axis=1)
    y = _gelu_tanh(y + d_ref[...] * u_ref[...].reshape(bsz * tt, D_S5))
    g = jnp.dot(y.astype(BF16), w_ref[...], preferred_element_type=F32) + b_ref[...]
    o_ref[...] = _rms(y * _sigmoid(g), gain_ref[...]).reshape(bsz, tt, D_S5).astype(BF16)


def _s5glu(yc, p3, d_skip, wglu, bglu, gain, tt=48):
    bsz, lp, _ = p3.shape
    u_blk = 4 * D_HG // D_S5
    return pl.pallas_call(
        _s5glu_kernel,
        grid=(lp // tt,),
        in_specs=[
            pl.BlockSpec((tt // S5_R * bsz, S5_GROUPS * S5_K), lambda i: (i, 0)),
            pl.BlockSpec((bsz, tt, D_S5), lambda i: (0, i, u_blk)),
            pl.BlockSpec((1, D_S5), lambda i: (0, 0)),
            pl.BlockSpec((D_S5, D_S5), lambda i: (0, 0)),
            pl.BlockSpec((1, D_S5), lambda i: (0, 0)),
            pl.BlockSpec((1, D_S5), lambda i: (0, 0)),
        ],
        out_specs=pl.BlockSpec((bsz, tt, D_S5), lambda i: (0, i, 0)),
        out_shape=jax.ShapeDtypeStruct((bsz, lp, D_S5), BF16),
        scratch_shapes=[pltpu.VMEM((D_S5 // 128, bsz * tt, 128), F32)],
        compiler_params=pltpu.CompilerParams(
            dimension_semantics=("arbitrary",), vmem_limit_bytes=VMEM_LIMIT),
        name="s5glu",
    )(yc, p3, d_skip.reshape(1, D_S5), wglu, bglu.reshape(1, D_S5), gain.reshape(1, D_S5))


def _s5_params(lam_re, lam_im, log_step, b_re, b_im, c_re, c_im):
    hp = lax.Precision.HIGHEST
    a_re = jnp.minimum(lam_re.astype(F32), -1e-4)
    a_im = lam_im.astype(F32)
    dt = jnp.exp(log_step.astype(F32))[:, None]
    mag = jnp.exp(a_re * dt)
    ab_re = mag * jnp.cos(a_im * dt)
    ab_im = mag * jnp.sin(a_im * dt)
    den = a_re * a_re + a_im * a_im
    x_re, x_im = ab_re - 1.0, ab_im
    z_re = (x_re * a_re + x_im * a_im) / den
    z_im = (x_im * a_re - x_re * a_im) / den
    br, bi = b_re.astype(F32), b_im.astype(F32)
    bb_re = z_re[..., None] * br - z_im[..., None] * bi
    bb_im = z_re[..., None] * bi + z_im[..., None] * br
    cr, ci = c_re.astype(F32), c_im.astype(F32)

    pw_re, pw_im = [jnp.ones_like(ab_re)], [jnp.zeros_like(ab_re)]
    for _ in range(S5_R):
        r, i = pw_re[-1], pw_im[-1]
        pw_re.append(r * ab_re - i * ab_im)
        pw_im.append(r * ab_im + i * ab_re)
    p_re, p_im = jnp.stack(pw_re), jnp.stack(pw_im)

    q_re, q_im = p_re[:S5_R, :, :, None], p_im[:S5_R, :, :, None]
    m_re = q_re * bb_re - q_im * bb_im
    m_im = q_re * bb_im + q_im * bb_re
    kern = (jnp.einsum('gop,lgph->ghlo', cr, m_re, precision=hp)
            - jnp.einsum('gop,lgph->ghlo', ci, m_im, precision=hp))
    kz = jnp.concatenate([jnp.zeros((S5_GROUPS, S5_GROUP, S5_K), F32),
                          kern.reshape(S5_GROUPS, S5_GROUP, S5_K)], axis=-1)
    tmat = jnp.stack([kz[:, :, (S5_R - s) * S5_GROUP:(S5_R - s) * S5_GROUP + S5_K]
                      for s in range(S5_R)], axis=1).reshape(S5_PAIRS, 2, S5_K, S5_K)

    bt_re, bt_im = bb_re.transpose(0, 2, 1)[:, None], bb_im.transpose(0, 2, 1)[:, None]
    rp_re = jnp.moveaxis(p_re[S5_R - 1::-1], 0, 1)[:, :, None, :]
    rp_im = jnp.moveaxis(p_im[S5_R - 1::-1], 0, 1)[:, :, None, :]
    h_re = (rp_re * bt_re - rp_im * bt_im).reshape(S5_PAIRS, 2, S5_K, S5_STATE)
    h_im = (rp_re * bt_im + rp_im * bt_re).reshape(S5_PAIRS, 2, S5_K, S5_STATE)
    zh = jnp.zeros_like(h_re[:, 0])
    hmat = jnp.concatenate([
        jnp.concatenate([h_re[:, 0], zh, h_im[:, 0], zh], axis=-1),
        jnp.concatenate([zh, h_re[:, 1], zh, h_im[:, 1]], axis=-1)], axis=1)

    ct_re, ct_im = cr.transpose(0, 2, 1)[:, :, None, :], ci.transpose(0, 2, 1)[:, :, None, :]
    e_re = jnp.moveaxis(p_re[1:], 0, 2)[..., None]
    e_im = jnp.moveaxis(p_im[1:], 0, 2)[..., None]
    g_re = (ct_re * e_re - ct_im * e_im).reshape(S5_PAIRS, 2, S5_STATE, S5_K)
    g_im = -(ct_re * e_im + ct_im * e_re).reshape(S5_PAIRS, 2, S5_STATE, S5_K)
    zg = jnp.zeros_like(g_re[:, 0])
    gmat = jnp.concatenate([
        jnp.concatenate([g_re[:, 0], zg], axis=-1), jnp.concatenate([zg, g_re[:, 1]], axis=-1),
        jnp.concatenate([g_im[:, 0], zg], axis=-1), jnp.concatenate([zg, g_im[:, 1]], axis=-1)],
        axis=1)

    a2 = jnp.stack([p_re[S5_R].reshape(S5_PAIRS, 2 * S5_STATE),
                    p_im[S5_R].reshape(S5_PAIRS, 2 * S5_STATE)], axis=1)
    return tmat.astype(BF16), hmat.astype(BF16), gmat.astype(BF16), a2


def _s5(p3, params, d_skip, wglu, bglu, gain):
    tmat, hmat, gmat, a2 = params
    yc = _s5core(_to_chunks(p3), tmat, hmat, gmat, a2, p3.shape[0])
    return _s5glu(yc, p3, d_skip, wglu, bglu, gain)


def _outproj_kernel(h_ref, a_ref, b_ref, w_ref, o_ref):
    o_ref[...] = (h_ref[...]
                  + jnp.dot(a_ref[...], w_ref[0], preferred_element_type=F32)
                  + jnp.dot(b_ref[...], w_ref[1], preferred_element_type=F32))


def _outproj(h, a, b, w2, tm=512):
    t, d = h.shape
    k = a.shape[1]
    return pl.pallas_call(
        _outproj_kernel,
        grid=(t // tm,),
        in_specs=[
            pl.BlockSpec((tm, d), lambda i: (i, 0)),
            pl.BlockSpec((tm, k), lambda i: (i, 0)),
            pl.BlockSpec((tm, k), lambda i: (i, 0)),
            pl.BlockSpec((2, k, d), lambda i: (0, 0, 0)),
        ],
        out_specs=pl.BlockSpec((tm, d), lambda i: (i, 0)),
        out_shape=jax.ShapeDtypeStruct((t, d), F32),
        compiler_params=pltpu.CompilerParams(
            dimension_semantics=("arbitrary",), vmem_limit_bytes=VMEM_LIMIT),
        name="outproj",
    )(h, a, b, w2)


FFN_HALO = 16


def _ffn_kernel(h_ref, halo_ref, g_ref, wg_ref, wu_ref, cw_ref, cb_ref, wd_ref, fg_ref, o_ref,
                hn_ref, *, final):
    f_axis = 2 if final else 1
    f = pl.program_id(f_axis)

    @pl.when(f == 0)
    def _():
        x = h_ref[...]
        hn_ref[FFN_HALO:, :] = _rms(x, g_ref[...]).astype(BF16)
        hn_ref[0:FFN_HALO, :] = _rms(halo_ref[...], g_ref[...]).astype(BF16)
        o_ref[...] = x

    a = jnp.dot(hn_ref[...], wg_ref[...], preferred_element_type=F32)
    up = jnp.dot(hn_ref[FFN_HALO:, :], wu_ref[...], preferred_element_type=F32)
    cw = cw_ref[...]
    conv = (cb_ref[...]
            + cw[0:1, :] * pltpu.roll(a, 2, 0)[FFN_HALO:, :]
            + cw[1:2, :] * pltpu.roll(a, 1, 0)[FFN_HALO:, :]
            + cw[2:3, :] * a[FFN_HALO:, :])
    hid = (conv * _sigmoid(conv) * up).astype(BF16)
    o_ref[...] += jnp.dot(hid, wd_ref[...], preferred_element_type=F32)

    if final:
        @pl.when(f == pl.num_programs(f_axis) - 1)
        def _():
            o_ref[...] = _rms(o_ref[...], fg_ref[...])


def _ffn(h, g, wg, wu, cw, cb, wd, fg, final, bsz, tm=512, tf=512):
    t, d = h.shape
    ff = wg.shape[1]
    hb = tm // FFN_HALO
    lp = t // bsz
    if final:
        grid = (bsz, (lp - CHUNK) // tm, ff // tf)
        row0 = lambda b, j, back: pl.multiple_of(b * lp + CHUNK + j * tm - back, FFN_HALO)
        row_specs = [
            pl.BlockSpec((pl.Element(tm), pl.Element(d)), lambda b, j, f: (row0(b, j, 0), 0)),
            pl.BlockSpec((pl.Element(FFN_HALO), pl.Element(d)),
                         lambda b, j, f: (row0(b, j, FFN_HALO), 0)),
        ]
        out_spec = pl.BlockSpec((tm, d), lambda b, j, f: (b * ((lp - CHUNK) // tm) + j, 0))
        out_rows = bsz * (lp - CHUNK)
        wmap = lambda fn: (lambda b, j, f: fn(f))
    else:
        grid = (t // tm, ff // tf)
        row_specs = [
            pl.BlockSpec((tm, d), lambda i, f: (i, 0)),
            pl.BlockSpec((FFN_HALO, d), lambda i, f: (jnp.maximum(i * hb - 1, 0), 0)),
        ]
        out_spec = pl.BlockSpec((tm, d), lambda i, f: (i, 0))
        out_rows = t
        wmap = lambda fn: (lambda i, f: fn(f))
    return pl.pallas_call(
        functools.partial(_ffn_kernel, final=final),
        grid=grid,
        in_specs=row_specs + [
            pl.BlockSpec((1, d), wmap(lambda f: (0, 0))),
            pl.BlockSpec((d, tf), wmap(lambda f: (0, f))),
            pl.BlockSpec((d, tf), wmap(lambda f: (0, f))),
            pl.BlockSpec((3, tf), wmap(lambda f: (0, f))),
            pl.BlockSpec((1, tf), wmap(lambda f: (0, f))),
            pl.BlockSpec((tf, d), wmap(lambda f: (f, 0))),
            pl.BlockSpec((1, d), wmap(lambda f: (0, 0))),
        ],
        out_specs=out_spec,
        out_shape=jax.ShapeDtypeStruct((out_rows, d), F32),
        scratch_shapes=[pltpu.VMEM((tm + FFN_HALO, d), BF16)],
        compiler_params=pltpu.CompilerParams(
            dimension_semantics=("arbitrary",) * len(grid), vmem_limit_bytes=VMEM_LIMIT),
        name="ffn_final" if final else "ffn",
    )(h, h, g.reshape(1, d), wg, wu, cw, cb.reshape(1, ff), wd, fg.reshape(1, d))


def kernel(x, meta_tokens, lb_logits, norm_mix, w_in, hg_norm, s5_lambda_re, s5_lambda_im,
           s5_log_step, s5_b_re, s5_b_im, s5_c_re, s5_c_im, s5_d, w_glu, b_glu, s5_norm, w_out,
           norm_ffn, w_ffn_gate, w_ffn_up, ffn_conv_w, ffn_conv_b, w_ffn_down, final_norm):
    bsz, seq, d = x.shape
    lp = seq + CHUNK
    meta = jnp.broadcast_to(meta_tokens.astype(x.dtype)[None], (bsz, N_META, d))
    pad = jnp.zeros((bsz, SEQ_PAD, d), x.dtype)
    h = jnp.concatenate([pad, meta, x], axis=1).reshape(bsz * lp, d)

    sm = jax.nn.softmax(lb_logits.astype(F32), axis=0)
    lb_all = jnp.cumsum(sm, axis=0) - sm[0:1]

    for l in range(DEPTH):
        proj = _inproj(h, norm_mix[l], w_in[l].astype(BF16))
        p3 = proj.reshape(bsz, lp, D_IN)
        o_hg = _hgrn2(p3, lb_all[l], hg_norm[l])
        s5_params = _s5_params(s5_lambda_re[l], s5_lambda_im[l], s5_log_step[l],
                               s5_b_re[l], s5_b_im[l], s5_c_re[l], s5_c_im[l])
        o_s5 = _s5(p3, s5_params, s5_d[l], w_glu[l].astype(BF16), b_glu[l], s5_norm[l])
        h = _outproj(h, o_hg.reshape(bsz * lp, D_HG), o_s5.reshape(bsz * lp, D_S5),
                     w_out[l].astype(BF16).reshape(2, D_HG, d))
        h = _ffn(h, norm_ffn[l], w_ffn_gate[l].astype(BF16), w_ffn_up[l].astype(BF16),
                 ffn_conv_w[l], ffn_conv_b[l], w_ffn_down[l].astype(BF16), final_norm,
                 final=(l == DEPTH - 1), bsz=bsz)
    return h.reshape(bsz, seq, d)
```

```python
import functools

import jax
import jax.numpy as jnp
from jax import lax
from jax.experimental import pallas as pl
from jax.experimental.pallas import tpu as pltpu

F32 = jnp.float32
BF16 = jnp.bfloat16

D_MODEL = 2048
DEPTH = 2
CHUNK = 64
N_META = 16
D_HG = 1024
HG_HEADS = 8
HG_DK = 128
D_S5 = 1024
S5_GROUP = 16
S5_GROUPS = 64
S5_STATE = 64
D_IN = 4 * D_HG + D_S5
D_FF = 5632
EPS = 1e-6
F_FLOOR = 1e-6
LOG2E = 1.4426950408889634

SEQ_PAD = CHUNK - N_META
VMEM_LIMIT = 56 * 1024 * 1024

S5_R = 16
S5_K = S5_R * S5_GROUP
S5_PAIRS = S5_GROUPS // 2


def _rms(x, g):
    ms = jnp.mean(x * x, axis=-1, keepdims=True)
    return x * lax.rsqrt(ms + EPS) * g


def _sigmoid(x):
    return 1.0 / (1.0 + jnp.exp(-x))


def _inproj_kernel(h_ref, g_ref, w_ref, o_ref, xn_ref):
    @pl.when(pl.program_id(1) == 0)
    def _():
        xn_ref[...] = _rms(h_ref[...], g_ref[...]).astype(BF16)

    o_ref[...] = jnp.dot(xn_ref[...], w_ref[...], preferred_element_type=F32)


def _inproj(h, g, w_all, layer, tm=1024, tn=1024):
    t, d = h.shape
    n = w_all.shape[2]
    return pl.pallas_call(
        _inproj_kernel,
        grid=(t // tm, n // tn),
        in_specs=[
            pl.BlockSpec((tm, d), lambda i, j: (i, 0)),
            pl.BlockSpec((1, d), lambda i, j: (0, 0)),
            pl.BlockSpec((None, d, tn), lambda i, j: (layer, 0, j)),
        ],
        out_specs=pl.BlockSpec((tm, tn), lambda i, j: (i, j)),
        out_shape=jax.ShapeDtypeStruct((t, n), F32),
        scratch_shapes=[pltpu.VMEM((tm, d), BF16)],
        compiler_params=pltpu.CompilerParams(
            dimension_semantics=("arbitrary", "arbitrary"), vmem_limit_bytes=VMEM_LIMIT),
        name="inproj",
    )(h, g.reshape(1, d), w_all)


def _pick_rows(x, m, first):
    j = lax.broadcasted_iota(jnp.int32, x.shape, 0)
    out = None
    for blk in range(8 // m - 1, -1, -1):
        r = blk * m + (0 if first else m - 1)
        row = jnp.broadcast_to(x[r:r + 1, :], x.shape)
        out = row if out is None else jnp.where(j < (blk + 1) * m, row, out)
    return out


def _expand_rows(ref, i, x):
    nv = x.shape[1] // 128
    for v in range(nv):
        ref[i, v] = x[:, v * 128:(v + 1) * 128]
    return jnp.concatenate(
        [jnp.concatenate([ref[i, v, pl.ds(j, 8, stride=0), :] for v in range(nv)], axis=1)
         for j in range(8)], axis=0)


def _dot_nt(a, b):
    return lax.dot_general(a, b, (((1,), (1,)), ((), ())), preferred_element_type=F32)


def _dot_tn(a, b):
    return lax.dot_general(a, b, (((0,), (0,)), ((), ())), preferred_element_type=F32)


def _hgrn2_kernel(p_ref, lb_ref, gain_ref, o_ref, st_ref, bc_ref):
    @pl.when(pl.program_id(1) == 0)
    def _():
        st_ref[...] = jnp.zeros_like(st_ref)

    def chunk(c, carry):
        _hgrn2_chunk(p_ref, lb_ref, gain_ref, o_ref, st_ref, bc_ref,
                     pl.ds(pl.multiple_of(c * CHUNK, CHUNK), CHUNK))
        return carry

    lax.fori_loop(0, p_ref.shape[0] // CHUNK, chunk, 0)


def _hgrn2_chunk(p_ref, lb_ref, gain_ref, o_ref, st_ref, bc_ref, rows):
    q = p_ref[rows, 0:D_HG]
    z = p_ref[rows, D_HG:2 * D_HG]
    v = p_ref[rows, 2 * D_HG:3 * D_HG].astype(BF16)
    lb = lb_ref[...]

    ez = jnp.exp2(jnp.abs(z) * (-LOG2E))
    rz = 1.0 / (1.0 + ez)
    sig_pos = jnp.where(z >= 0, rz, ez * rz)
    sig_neg = jnp.where(z >= 0, ez * rz, rz)
    gl = jnp.log2(jnp.maximum(lb + (1.0 - lb) * sig_pos, F_FLOOR))
    kk = (1.0 - lb) * sig_neg
    qf = q * _sigmoid(q)

    hi = gl.astype(BF16)
    r1 = gl - hi.astype(F32)
    mid = r1.astype(BF16)
    lo = (r1 - mid.astype(F32)).astype(BF16)
    nsel = CHUNK + 16
    row = lax.broadcasted_iota(jnp.int32, (nsel, 3 * CHUNK), 0)
    col = lax.broadcasted_iota(jnp.int32, (nsel, 3 * CHUNK), 1) & (CHUNK - 1)
    tgt = jnp.where(row < CHUNK, row,
                    jnp.where(row < CHUNK + 8, (row - CHUNK) * 8 + 7, (row - CHUNK - 8) * 8 + 3))
    tri3 = (tgt >= col).astype(BF16)
    cum = jnp.dot(tri3, jnp.concatenate([hi, mid, lo], axis=0), preferred_element_type=F32)
    b = cum[0:CHUNK]
    l8 = cum[CHUNK:CHUNK + 8]
    m8 = cum[CHUNK + 8:CHUNK + 16]
    j8 = lax.broadcasted_iota(jnp.int32, l8.shape, 0)
    s8 = jnp.where(j8 == 0, 0.0, pltpu.roll(l8, 1, 0))

    qe = qf * jnp.exp2(b - _expand_rows(bc_ref, 0, s8))
    kf = kk * jnp.exp2(_expand_rows(bc_ref, 1, l8) - b)
    qh = {8: qe.astype(BF16)}
    kh = {8: kf.astype(BF16)}
    for lvl, (w, m) in enumerate(((16, 2), (32, 4), (64, 8))):
        cq = jnp.exp2(s8 - _pick_rows(s8, m, True))
        ck = jnp.exp2(_pick_rows(l8, m, False) - l8)
        qh[w] = (qe * _expand_rows(bc_ref, 3 + 2 * lvl, cq)).astype(BF16)
        kh[w] = (kf * _expand_rows(bc_ref, 4 + 2 * lvl, ck)).astype(BF16)
    b_mid = _expand_rows(bc_ref, 2, m8)
    qh[0] = (qf * jnp.exp2(b - b_mid)).astype(BF16)
    kh[0] = (kk * jnp.exp2(b_mid - b)).astype(BF16)
    dec_all = jnp.exp2(l8[7:8, :])

    ti = lax.broadcasted_iota(jnp.int32, (CHUNK, CHUNK), 0)
    si = lax.broadcasted_iota(jnp.int32, (CHUNK, CHUNK), 1)
    masks = {}
    for w, sh in ((32, 5), (16, 4), (8, 3)):
        masks[w] = ((ti >> sh) == (si >> sh) + 1) & ((ti >> (sh + 1)) == (si >> (sh + 1)))
    masks[0] = ((ti >> 3) == (si >> 3)) & (si <= ti)

    gate = p_ref[rows, 3 * D_HG:4 * D_HG]
    gate = gate * _sigmoid(gate)
    gain = gain_ref[...]

    heads = [slice(h * HG_DK, (h + 1) * HG_DK) for h in range(HG_HEADS)]
    sts = [st_ref[h] for h in range(HG_HEADS)]
    o_inter = [_dot_nt(qh[64][:, sl], st.astype(BF16)) for sl, st in zip(heads, sts)]
    scores = []
    for sl in heads:
        sc = jnp.zeros((CHUNK, CHUNK), F32)
        for w in (32, 16, 8, 0):
            sc = jnp.where(masks[w], _dot_nt(qh[w][:, sl], kh[w][:, sl]), sc)
        scores.append(sc.astype(BF16))
    outs = [oi + jnp.dot(sc, v[:, sl], preferred_element_type=F32)
            for oi, sc, sl in zip(o_inter, scores, heads)]
    for h, (sl, st) in enumerate(zip(heads, sts)):
        st_ref[h] = dec_all[:, sl] * st + _dot_tn(v[:, sl], kh[64][:, sl])
    for o, sl in zip(outs, heads):
        ms = jnp.mean(o * o, axis=-1, keepdims=True)
        o = o * lax.rsqrt(ms + EPS) * gain[:, sl]
        o_ref[rows, sl] = (o * gate[:, sl]).astype(BF16)


def _hgrn2(p3, lb, gain, chunks_per_step=11):
    bsz, lp, _ = p3.shape
    rows = chunks_per_step * CHUNK
    return pl.pallas_call(
        _hgrn2_kernel,
        grid=(bsz, lp // rows),
        in_specs=[
            pl.BlockSpec((None, rows, 4 * D_HG), lambda b, c: (b, c, 0)),
            pl.BlockSpec((1, D_HG), lambda b, c: (0, 0)),
            pl.BlockSpec((1, D_HG), lambda b, c: (0, 0)),
        ],
        out_specs=pl.BlockSpec((None, rows, D_HG), lambda b, c: (b, c, 0)),
        out_shape=jax.ShapeDtypeStruct((bsz, lp, D_HG), BF16),
        scratch_shapes=[
            pltpu.VMEM((HG_HEADS, HG_DK, HG_DK), F32),
            pltpu.VMEM((9, D_HG // 128, 8, 128), F32),
        ],
        compiler_params=pltpu.CompilerParams(
            dimension_semantics=("arbitrary", "arbitrary"), vmem_limit_bytes=VMEM_LIMIT),
        name="hgrn2",
    )(p3, lb.reshape(1, D_HG), gain.reshape(1, D_HG))


def _gelu_tanh(x):
    c = 0.7978845608028654
    half = 0.5 * x
    return half + half * jnp.tanh(x * (c + (c * 0.044715) * (x * x)))


def _s5core_kernel(u_ref, t_ref, h_ref, g_ref, a_ref, y_ref, v_ref, xp_ref, *, bsz):
    nchunks = u_ref.shape[0] // bsz
    u = u_ref[...]
    v_ref[...] = jnp.dot(u, h_ref[0], preferred_element_type=F32)
    a_re = jnp.broadcast_to(a_ref[0, 0:1, :], (bsz, 128))
    a_im = jnp.broadcast_to(a_ref[0, 1:2, :], (bsz, 128))

    def step(n, carry):
        x_re, x_im = carry
        rows = pl.ds(pl.multiple_of(n * bsz, bsz), bsz)
        xp_ref[rows, 0:128] = x_re.astype(BF16)
        xp_ref[rows, 128:256] = x_im.astype(BF16)
        return (a_re * x_re - a_im * x_im + v_ref[rows, 0:128],
                a_re * x_im + a_im * x_re + v_ref[rows, 128:256])

    zero = jnp.zeros((bsz, 128), F32)
    lax.fori_loop(0, nchunks, step, (zero, zero), unroll=4)

    ys = jnp.dot(xp_ref[...], g_ref[0], preferred_element_type=F32)
    for k in range(2):
        cols = slice(k * S5_K, (k + 1) * S5_K)
        y_ref[:, cols] = ys[:, cols] + jnp.dot(u[:, cols], t_ref[0, k], preferred_element_type=F32)


def _s5core(uc, params, layer, bsz):
    tmat, hmat, gmat, a2 = params
    rows = uc.shape[0]
    return pl.pallas_call(
        functools.partial(_s5core_kernel, bsz=bsz),
        grid=(S5_PAIRS,),
        in_specs=[
            pl.BlockSpec((rows, 2 * S5_K), lambda p: (0, p)),
            pl.BlockSpec((None, 1, 2, S5_K, S5_K), lambda p: (layer, p, 0, 0, 0)),
            pl.BlockSpec((None, 1, 2 * S5_K, 4 * S5_STATE), lambda p: (layer, p, 0, 0)),
            pl.BlockSpec((None, 1, 4 * S5_STATE, 2 * S5_K), lambda p: (layer, p, 0, 0)),
            pl.BlockSpec((None, 1, 2, 2 * S5_STATE), lambda p: (layer, p, 0, 0)),
        ],
        out_specs=pl.BlockSpec((rows, 2 * S5_K), lambda p: (0, p)),
        out_shape=jax.ShapeDtypeStruct((rows, S5_GROUPS * S5_K), F32),
        scratch_shapes=[
            pltpu.VMEM((rows, 4 * S5_STATE), F32),
            pltpu.VMEM((rows, 4 * S5_STATE), BF16),
        ],
        compiler_params=pltpu.CompilerParams(
            dimension_semantics=("arbitrary",), vmem_limit_bytes=VMEM_LIMIT),
        name="s5core",
    )(uc, tmat, hmat, gmat, a2)


def _atom_transpose(groups):
    lane = lax.broadcasted_iota(jnp.int32, groups[0][0].shape, 1)
    groups = [list(xs) for xs in groups]
    for s in range(3):
        d = 1 << s
        keep = ((lane >> (4 + s)) & 1) == 0
        for xs in groups:
            for i in range(8):
                if i & d:
                    continue
                lo, hi = xs[i], xs[i + d]
                xs[i] = jnp.where(keep, lo, pltpu.roll(hi, S5_GROUP * d, 1))
                xs[i + d] = jnp.where(keep, pltpu.roll(lo, 128 - S5_GROUP * d, 1), hi)
    return groups


def _to_chunks_kernel(u_ref, o_ref, rows_ref):
    bsz, tt, width = u_ref.shape
    pitch = rows_ref.shape[1] // bsz
    for v in range(width // 128):
        for b in range(bsz):
            rows_ref[v, b * pitch:b * pitch + tt, :] = u_ref[b, :, v * 128:(v + 1) * 128]
    halves = [(v, n, k) for v in range(width // 128) for n in range(tt // S5_R) for k in range(2)]
    groups = _atom_transpose(
        [[rows_ref[v, pl.ds(n * S5_R + 8 * k + j, bsz, stride=pitch), :] for j in range(8)]
         for v, n, k in halves])
    for (v, n, k), ys in zip(halves, groups):
        for g, y in enumerate(ys):
            lane0 = (8 * v + g) * S5_K + k * 128
            o_ref[n * bsz:(n + 1) * bsz, lane0:lane0 + 128] = y.astype(BF16)


def _to_chunks(p3, tt=176, width=512):
    bsz, lp, _ = p3.shape
    u_blk0 = 4 * D_HG // width
    return pl.pallas_call(
        _to_chunks_kernel,
        grid=(lp // tt, D_S5 // width),
        in_specs=[pl.BlockSpec((bsz, tt, width), lambda i, v: (0, i, u_blk0 + v))],
        out_specs=pl.BlockSpec((tt, width // S5_GROUP * S5_K), lambda i, v: (i, v)),
        out_shape=jax.ShapeDtypeStruct((lp // S5_R * bsz, S5_GROUPS * S5_K), BF16),
        scratch_shapes=[pltpu.VMEM((width // 128, bsz * (tt + 8), 128), F32)],
        compiler_params=pltpu.CompilerParams(
            dimension_semantics=("arbitrary", "arbitrary"), vmem_limit_bytes=VMEM_LIMIT),
        name="s5_to_chunks",
    )(p3)


def _s5glu_kernel(yc_ref, u_ref, d_ref, w_ref, b_ref, gain_ref, o_ref, yn_ref):
    bsz, tt, _ = u_ref.shape
    pitch = yn_ref.shape[1] // bsz
    for n in range(tt // S5_R):
        halves = [(v, k) for v in range(D_S5 // 128) for k in range(2)]
        groups = _atom_transpose(
            [[yc_ref[n * bsz:(n + 1) * bsz,
                     (8 * v + g) * S5_K + k * 128:(8 * v + g) * S5_K + (k + 1) * 128]
              for g in range(8)] for v, k in halves])
        for (v, k), xs in zip(halves, groups):
            for j, x in enumerate(xs):
                yn_ref[v, pl.ds(n * S5_R + 8 * k + j, bsz, stride=pitch), :] = x
    y = jnp.concatenate(
        [jnp.concatenate([yn_ref[v, b * pitch:b * pitch + tt, :] for v in range(D_S5 // 128)],
                         axis=1) for b in range(bsz)], axis=0)
    y = _gelu_tanh(y + d_ref[...] * u_ref[...].reshape(bsz * tt, D_S5))
    g = jnp.dot(y.astype(BF16), w_ref[...], preferred_element_type=F32) + b_ref[...]
    o_ref[...] = _rms(y * _sigmoid(g), gain_ref[...]).reshape(bsz, tt, D_S5).astype(BF16)


def _s5glu(yc, p3, d_skip, wglu_all, layer, bglu, gain, tt=48):
    bsz, lp, _ = p3.shape
    u_blk = 4 * D_HG // D_S5
    return pl.pallas_call(
        _s5glu_kernel,
        grid=(lp // tt,),
        in_specs=[
            pl.BlockSpec((tt // S5_R * bsz, S5_GROUPS * S5_K), lambda i: (i, 0)),
            pl.BlockSpec((bsz, tt, D_S5), lambda i: (0, i, u_blk)),
            pl.BlockSpec((1, D_S5), lambda i: (0, 0)),
            pl.BlockSpec((None, D_S5, D_S5), lambda i: (layer, 0, 0)),
            pl.BlockSpec((1, D_S5), lambda i: (0, 0)),
            pl.BlockSpec((1, D_S5), lambda i: (0, 0)),
        ],
        out_specs=pl.BlockSpec((bsz, tt, D_S5), lambda i: (0, i, 0)),
        out_shape=jax.ShapeDtypeStruct((bsz, lp, D_S5), BF16),
        scratch_shapes=[pltpu.VMEM((D_S5 // 128, bsz * (tt + 8), 128), F32)],
        compiler_params=pltpu.CompilerParams(
            dimension_semantics=("arbitrary",), vmem_limit_bytes=VMEM_LIMIT),
        name="s5glu",
    )(yc, p3, d_skip.reshape(1, D_S5), wglu_all, bglu.reshape(1, D_S5), gain.reshape(1, D_S5))


def _s5_params(lam_re, lam_im, log_step, b_re, b_im, c_re, c_im):
    hp = lax.Precision.HIGHEST
    a_re = jnp.minimum(lam_re.astype(F32), -1e-4)
    a_im = lam_im.astype(F32)
    dt = jnp.exp(log_step.astype(F32))[:, None]
    mag = jnp.exp(a_re * dt)
    ab_re = mag * jnp.cos(a_im * dt)
    ab_im = mag * jnp.sin(a_im * dt)
    den = a_re * a_re + a_im * a_im
    x_re, x_im = ab_re - 1.0, ab_im
    z_re = (x_re * a_re + x_im * a_im) / den
    z_im = (x_im * a_re - x_re * a_im) / den
    br, bi = b_re.astype(F32), b_im.astype(F32)
    bb_re = z_re[..., None] * br - z_im[..., None] * bi
    bb_im = z_re[..., None] * bi + z_im[..., None] * br
    cr, ci = c_re.astype(F32), c_im.astype(F32)

    pw_re, pw_im = [jnp.ones_like(ab_re)], [jnp.zeros_like(ab_re)]
    for _ in range(S5_R):
        r, i = pw_re[-1], pw_im[-1]
        pw_re.append(r * ab_re - i * ab_im)
        pw_im.append(r * ab_im + i * ab_re)
    p_re, p_im = jnp.stack(pw_re), jnp.stack(pw_im)

    q_re, q_im = p_re[:S5_R, :, :, None], p_im[:S5_R, :, :, None]
    m_re = q_re * bb_re - q_im * bb_im
    m_im = q_re * bb_im + q_im * bb_re
    kern = (jnp.einsum('gop,lgph->ghlo', cr, m_re, precision=hp)
            - jnp.einsum('gop,lgph->ghlo', ci, m_im, precision=hp))
    kz = jnp.concatenate([jnp.zeros((S5_GROUPS, S5_GROUP, S5_K), F32),
                          kern.reshape(S5_GROUPS, S5_GROUP, S5_K)], axis=-1)
    tmat = jnp.stack([kz[:, :, (S5_R - s) * S5_GROUP:(S5_R - s) * S5_GROUP + S5_K]
                      for s in range(S5_R)], axis=1).reshape(S5_PAIRS, 2, S5_K, S5_K)

    bt_re, bt_im = bb_re.transpose(0, 2, 1)[:, None], bb_im.transpose(0, 2, 1)[:, None]
    rp_re = jnp.moveaxis(p_re[S5_R - 1::-1], 0, 1)[:, :, None, :]
    rp_im = jnp.moveaxis(p_im[S5_R - 1::-1], 0, 1)[:, :, None, :]
    h_re = (rp_re * bt_re - rp_im * bt_im).reshape(S5_PAIRS, 2, S5_K, S5_STATE)
    h_im = (rp_re * bt_im + rp_im * bt_re).reshape(S5_PAIRS, 2, S5_K, S5_STATE)
    zh = jnp.zeros_like(h_re[:, 0])
    hmat = jnp.concatenate([
        jnp.concatenate([h_re[:, 0], zh, h_im[:, 0], zh], axis=-1),
        jnp.concatenate([zh, h_re[:, 1], zh, h_im[:, 1]], axis=-1)], axis=1)

    ct_re, ct_im = cr.transpose(0, 2, 1)[:, :, None, :], ci.transpose(0, 2, 1)[:, :, None, :]
    e_re = jnp.moveaxis(p_re[1:], 0, 2)[..., None]
    e_im = jnp.moveaxis(p_im[1:], 0, 2)[..., None]
    g_re = (ct_re * e_re - ct_im * e_im).reshape(S5_PAIRS, 2, S5_STATE, S5_K)
    g_im = -(ct_re * e_im + ct_im * e_re).reshape(S5_PAIRS, 2, S5_STATE, S5_K)
    zg = jnp.zeros_like(g_re[:, 0])
    gmat = jnp.concatenate([
        jnp.concatenate([g_re[:, 0], zg], axis=-1), jnp.concatenate([zg, g_re[:, 1]], axis=-1),
        jnp.concatenate([g_im[:, 0], zg], axis=-1), jnp.concatenate([zg, g_im[:, 1]], axis=-1)],
        axis=1)

    a2 = jnp.stack([p_re[S5_R].reshape(S5_PAIRS, 2 * S5_STATE),
                    p_im[S5_R].reshape(S5_PAIRS, 2 * S5_STATE)], axis=1)
    return tmat.astype(BF16), hmat.astype(BF16), gmat.astype(BF16), a2


def _s5(p3, params, layer, d_skip, wglu_all, bglu, gain):
    yc = _s5core(_to_chunks(p3), params, layer, p3.shape[0])
    return _s5glu(yc, p3, d_skip, wglu_all, layer, bglu, gain)


def _outproj_kernel(h_ref, a_ref, b_ref, w_ref, g_ref, o_ref, hn_ref):
    y = (h_ref[...]
         + jnp.dot(a_ref[...], w_ref[0], preferred_element_type=F32)
         + jnp.dot(b_ref[...], w_ref[1], preferred_element_type=F32))
    o_ref[...] = y
    hn_ref[...] = _rms(y, g_ref[...]).astype(BF16)


def _outproj(h, a, b, w_all, layer, g_next, tm=512):
    t, d = h.shape
    k = a.shape[1]
    return pl.pallas_call(
        _outproj_kernel,
        grid=(t // tm,),
        in_specs=[
            pl.BlockSpec((tm, d), lambda i: (i, 0)),
            pl.BlockSpec((tm, k), lambda i: (i, 0)),
            pl.BlockSpec((tm, k), lambda i: (i, 0)),
            pl.BlockSpec((None, 2, k, d), lambda i: (layer, 0, 0, 0)),
            pl.BlockSpec((1, d), lambda i: (0, 0)),
        ],
        out_specs=[pl.BlockSpec((tm, d), lambda i: (i, 0)),
                   pl.BlockSpec((tm, d), lambda i: (i, 0))],
        out_shape=[jax.ShapeDtypeStruct((t, d), F32), jax.ShapeDtypeStruct((t, d), BF16)],
        compiler_params=pltpu.CompilerParams(
            dimension_semantics=("arbitrary",), vmem_limit_bytes=VMEM_LIMIT),
        name="outproj",
    )(h, a, b, w_all, g_next.reshape(1, d))


FFN_HALO = 16


def _ffn_kernel(h_ref, hn_ref, halo_ref, wg_ref, wu_ref, cw_ref, cb_ref, wd_ref, fg_ref, o_ref,
                hcat_ref, *, final):
    f_axis = 2 if final else 1
    f = pl.program_id(f_axis)

    @pl.when(f == 0)
    def _():
        hcat_ref[FFN_HALO:, :] = hn_ref[...]
        hcat_ref[0:FFN_HALO, :] = halo_ref[...]
        o_ref[...] = h_ref[...]

    a = jnp.dot(hcat_ref[...], wg_ref[...], preferred_element_type=F32)
    up = jnp.dot(hcat_ref[FFN_HALO:, :], wu_ref[...], preferred_element_type=F32)
    cw = cw_ref[...]
    conv = (cb_ref[...]
            + cw[0:1, :] * pltpu.roll(a, 2, 0)[FFN_HALO:, :]
            + cw[1:2, :] * pltpu.roll(a, 1, 0)[FFN_HALO:, :]
            + cw[2:3, :] * a[FFN_HALO:, :])
    hid = (conv * _sigmoid(conv) * up).astype(BF16)
    o_ref[...] += jnp.dot(hid, wd_ref[...], preferred_element_type=F32)

    if final:
        @pl.when(f == pl.num_programs(f_axis) - 1)
        def _():
            o_ref[...] = _rms(o_ref[...], fg_ref[...])


def _ffn(h, hn, layer, wg_all, wu_all, cw_all, cb_all, wd_all, fg, final, bsz, tm=512, tf=512):
    t, d = h.shape
    ff = wg_all.shape[2]
    hb = tm // FFN_HALO
    lp = t // bsz
    if final:
        grid = (bsz, (lp - CHUNK) // tm, ff // tf)
        row0 = lambda b, j, back: pl.multiple_of(b * lp + CHUNK + j * tm - back, FFN_HALO)
        main = lambda b, j, f: (row0(b, j, 0), 0)
        row_specs = [
            pl.BlockSpec((pl.Element(tm), pl.Element(d)), main),
            pl.BlockSpec((pl.Element(tm), pl.Element(d)), main),
            pl.BlockSpec((pl.Element(FFN_HALO), pl.Element(d)),
                         lambda b, j, f: (row0(b, j, FFN_HALO), 0)),
        ]
        out_spec = pl.BlockSpec((tm, d), lambda b, j, f: (b * ((lp - CHUNK) // tm) + j, 0))
        out_rows = bsz * (lp - CHUNK)
        wmap = lambda fn: (lambda b, j, f: fn(f))
    else:
        grid = (t // tm, ff // tf)
        row_specs = [
            pl.BlockSpec((tm, d), lambda i, f: (i, 0)),
            pl.BlockSpec((tm, d), lambda i, f: (i, 0)),
            pl.BlockSpec((FFN_HALO, d), lambda i, f: (jnp.maximum(i * hb - 1, 0), 0)),
        ]
        out_spec = pl.BlockSpec((tm, d), lambda i, f: (i, 0))
        out_rows = t
        wmap = lambda fn: (lambda i, f: fn(f))
    return pl.pallas_call(
        functools.partial(_ffn_kernel, final=final),
        grid=grid,
        in_specs=row_specs + [
            pl.BlockSpec((None, d, tf), wmap(lambda f: (layer, 0, f))),
            pl.BlockSpec((None, d, tf), wmap(lambda f: (layer, 0, f))),
            pl.BlockSpec((None, 3, tf), wmap(lambda f: (layer, 0, f))),
            pl.BlockSpec((None, 1, tf), wmap(lambda f: (layer, 0, f))),
            pl.BlockSpec((None, tf, d), wmap(lambda f: (layer, f, 0))),
            pl.BlockSpec((1, d), wmap(lambda f: (0, 0))),
        ],
        out_specs=out_spec,
        out_shape=jax.ShapeDtypeStruct((out_rows, d), F32),
        scratch_shapes=[pltpu.VMEM((tm + FFN_HALO, d), BF16)],
        compiler_params=pltpu.CompilerParams(
            dimension_semantics=("arbitrary",) * len(grid), vmem_limit_bytes=VMEM_LIMIT),
        name="ffn_final" if final else "ffn",
    )(h, hn, hn, wg_all, wu_all, cw_all, cb_all.reshape(cb_all.shape[0], 1, ff), wd_all,
      fg.reshape(1, d))


def kernel(x, meta_tokens, lb_logits, norm_mix, w_in, hg_norm, s5_lambda_re, s5_lambda_im,
           s5_log_step, s5_b_re, s5_b_im, s5_c_re, s5_c_im, s5_d, w_glu, b_glu, s5_norm, w_out,
           norm_ffn, w_ffn_gate, w_ffn_up, ffn_conv_w, ffn_conv_b, w_ffn_down, final_norm):
    bsz, seq, d = x.shape
    lp = seq + CHUNK
    meta = jnp.broadcast_to(meta_tokens.astype(x.dtype)[None], (bsz, N_META, d))
    pad = jnp.zeros((bsz, SEQ_PAD, d), x.dtype)
    h = jnp.concatenate([pad, meta, x], axis=1).reshape(bsz * lp, d)

    sm = jax.nn.softmax(lb_logits.astype(F32), axis=0)
    lb_all = jnp.cumsum(sm, axis=0) - sm[0:1]

    w_in_b = w_in.astype(BF16)
    w_out_b = w_out.astype(BF16).reshape(DEPTH, 2, D_HG, d)
    w_glu_b = w_glu.astype(BF16)
    wg_b, wu_b, wd_b = (w.astype(BF16) for w in (w_ffn_gate, w_ffn_up, w_ffn_down))
    s5_params = jax.vmap(_s5_params)(s5_lambda_re, s5_lambda_im, s5_log_step,
                                     s5_b_re, s5_b_im, s5_c_re, s5_c_im)

    for l in range(DEPTH):
        proj = _inproj(h, norm_mix[l], w_in_b, l)
        p3 = proj.reshape(bsz, lp, D_IN)
        o_hg = _hgrn2(p3, lb_all[l], hg_norm[l])
        o_s5 = _s5(p3, s5_params, l, s5_d[l], w_glu_b, b_glu[l], s5_norm[l])
        h, hn = _outproj(h, o_hg.reshape(bsz * lp, D_HG), o_s5.reshape(bsz * lp, D_S5),
                         w_out_b, l, norm_ffn[l])
        h = _ffn(h, hn, l, wg_b, wu_b, ffn_conv_w, ffn_conv_b, wd_b, final_norm,
                 final=(l == DEPTH - 1), bsz=bsz)
    return h.reshape(bsz, seq, d)
```

```python
import functools

import jax
import jax.numpy as jnp
from jax import lax
from jax.experimental import pallas as pl
from jax.experimental.pallas import tpu as pltpu

F32 = jnp.float32
BF16 = jnp.bfloat16

D_MODEL = 2048
DEPTH = 2
CHUNK = 64
N_META = 16
D_HG = 1024
HG_HEADS = 8
HG_DK = 128
D_S5 = 1024
S5_GROUP = 16
S5_GROUPS = 64
S5_STATE = 64
D_IN = 4 * D_HG + D_S5
D_FF = 5632
EPS = 1e-6
F_FLOOR = 1e-6
LOG2E = 1.4426950408889634

SEQ_PAD = CHUNK - N_META
VMEM_LIMIT = 56 * 1024 * 1024

S5_R = 16
S5_K = S5_R * S5_GROUP
S5_PAIRS = S5_GROUPS // 2


def _rms(x, g):
    ms = jnp.mean(x * x, axis=-1, keepdims=True)
    return x * lax.rsqrt(ms + EPS) * g


def _sigmoid(x):
    return 1.0 / (1.0 + jnp.exp(-x))


def _inproj_kernel(h_ref, pre_ref, g_ref, w_ref, o_ref, xn_ref, *, first):
    @pl.when(pl.program_id(2) == 0)
    def _():
        if first:
            npre = pre_ref.shape[0]

            @pl.when(pl.program_id(1) == 0)
            def _():
                xn_ref[0:npre, :] = _rms(pre_ref[...], g_ref[...]).astype(BF16)
                xn_ref[npre:, :] = _rms(h_ref[0:h_ref.shape[0] - npre, :], g_ref[...]).astype(BF16)

            @pl.when(pl.program_id(1) != 0)
            def _():
                xn_ref[...] = _rms(h_ref[...], g_ref[...]).astype(BF16)
        else:
            xn_ref[...] = _rms(h_ref[...], g_ref[...]).astype(BF16)

    o_ref[...] = jnp.dot(xn_ref[...], w_ref[...], preferred_element_type=F32)


def _inproj(h, prefix, g, w_all, layer, first, bsz, tm=1056, tn=1024):
    d = h.shape[1]
    n = w_all.shape[2]
    lp = h.shape[0] // bsz + (CHUNK if first else 0)
    tiles = lp // tm
    return pl.pallas_call(
        functools.partial(_inproj_kernel, first=first),
        grid=(bsz, tiles, n // tn),
        in_specs=[
            _row_tile_specs(first, bsz, lp, tm, d),
            pl.BlockSpec((CHUNK, d), lambda b, i, j: (0, 0)),
            pl.BlockSpec((1, d), lambda b, i, j: (0, 0)),
            pl.BlockSpec((None, d, tn), lambda b, i, j: (layer, 0, j)),
        ],
        out_specs=pl.BlockSpec((tm, tn), lambda b, i, j: (b * tiles + i, j)),
        out_shape=jax.ShapeDtypeStruct((bsz * lp, n), F32),
        scratch_shapes=[pltpu.VMEM((tm, d), BF16)],
        compiler_params=pltpu.CompilerParams(
            dimension_semantics=("arbitrary", "arbitrary", "arbitrary"),
            vmem_limit_bytes=VMEM_LIMIT),
        name="inproj",
    )(h, prefix, g.reshape(1, d), w_all)


def _row_tile_specs(first, bsz, lp, tm, d):
    tiles = lp // tm
    if first:
        def rows(b, j, *_):
            return (pl.multiple_of(b * (lp - CHUNK) + jnp.maximum(j * tm - CHUNK, 0), 8), 0)
        return pl.BlockSpec((pl.Element(tm), pl.Element(d)), rows)
    return pl.BlockSpec((tm, d), lambda b, j, *_: (b * tiles + j, 0))


def _pick_rows(x, m, first):
    j = lax.broadcasted_iota(jnp.int32, x.shape, 0)
    out = None
    for blk in range(8 // m - 1, -1, -1):
        r = blk * m + (0 if first else m - 1)
        row = jnp.broadcast_to(x[r:r + 1, :], x.shape)
        out = row if out is None else jnp.where(j < (blk + 1) * m, row, out)
    return out


def _expand_rows(ref, i, x):
    nv = x.shape[1] // 128
    for v in range(nv):
        ref[i, v] = x[:, v * 128:(v + 1) * 128]
    return jnp.concatenate(
        [jnp.concatenate([ref[i, v, pl.ds(j, 8, stride=0), :] for v in range(nv)], axis=1)
         for j in range(8)], axis=0)


def _dot_nt(a, b):
    return lax.dot_general(a, b, (((1,), (1,)), ((), ())), preferred_element_type=F32)


def _dot_tn(a, b):
    return lax.dot_general(a, b, (((0,), (0,)), ((), ())), preferred_element_type=F32)


def _hgrn2_kernel(p_ref, lb_ref, gain_ref, o_ref, st_ref, bc_ref):
    @pl.when(pl.program_id(1) == 0)
    def _():
        st_ref[...] = jnp.zeros_like(st_ref)

    def chunk(c, carry):
        _hgrn2_chunk(p_ref, lb_ref, gain_ref, o_ref, st_ref, bc_ref,
                     pl.ds(pl.multiple_of(c * CHUNK, CHUNK), CHUNK))
        return carry

    lax.fori_loop(0, p_ref.shape[0] // CHUNK, chunk, 0)


def _hgrn2_chunk(p_ref, lb_ref, gain_ref, o_ref, st_ref, bc_ref, rows):
    q = p_ref[rows, 0:D_HG]
    z = p_ref[rows, D_HG:2 * D_HG]
    v = p_ref[rows, 2 * D_HG:3 * D_HG].astype(BF16)
    lb = lb_ref[...]

    ez = jnp.exp2(jnp.abs(z) * (-LOG2E))
    rz = 1.0 / (1.0 + ez)
    sig_pos = jnp.where(z >= 0, rz, ez * rz)
    sig_neg = jnp.where(z >= 0, ez * rz, rz)
    gl = jnp.log2(jnp.maximum(lb + (1.0 - lb) * sig_pos, F_FLOOR))
    kk = (1.0 - lb) * sig_neg
    qf = q * _sigmoid(q)

    hi = gl.astype(BF16)
    r1 = gl - hi.astype(F32)
    mid = r1.astype(BF16)
    lo = (r1 - mid.astype(F32)).astype(BF16)
    nsel = CHUNK + 16
    row = lax.broadcasted_iota(jnp.int32, (nsel, 3 * CHUNK), 0)
    col = lax.broadcasted_iota(jnp.int32, (nsel, 3 * CHUNK), 1) & (CHUNK - 1)
    tgt = jnp.where(row < CHUNK, row,
                    jnp.where(row < CHUNK + 8, (row - CHUNK) * 8 + 7, (row - CHUNK - 8) * 8 + 3))
    tri3 = (tgt >= col).astype(BF16)
    cum = jnp.dot(tri3, jnp.concatenate([hi, mid, lo], axis=0), preferred_element_type=F32)
    b = cum[0:CHUNK]
    l8 = cum[CHUNK:CHUNK + 8]
    m8 = cum[CHUNK + 8:CHUNK + 16]
    j8 = lax.broadcasted_iota(jnp.int32, l8.shape, 0)
    s8 = jnp.where(j8 == 0, 0.0, pltpu.roll(l8, 1, 0))

    qe = qf * jnp.exp2(b - _expand_rows(bc_ref, 0, s8))
    kf = kk * jnp.exp2(_expand_rows(bc_ref, 1, l8) - b)
    qh = {8: qe.astype(BF16)}
    kh = {8: kf.astype(BF16)}
    for lvl, (w, m) in enumerate(((16, 2), (32, 4), (64, 8))):
        cq = jnp.exp2(s8 - _pick_rows(s8, m, True))
        ck = jnp.exp2(_pick_rows(l8, m, False) - l8)
        qh[w] = (qe * _expand_rows(bc_ref, 3 + 2 * lvl, cq)).astype(BF16)
        kh[w] = (kf * _expand_rows(bc_ref, 4 + 2 * lvl, ck)).astype(BF16)
    b_mid = _expand_rows(bc_ref, 2, m8)
    qh[0] = (qf * jnp.exp2(b - b_mid)).astype(BF16)
    kh[0] = (kk * jnp.exp2(b_mid - b)).astype(BF16)
    dec_all = jnp.exp2(l8[7:8, :])

    ti = lax.broadcasted_iota(jnp.int32, (CHUNK, CHUNK), 0)
    si = lax.broadcasted_iota(jnp.int32, (CHUNK, CHUNK), 1)
    masks = {}
    for w, sh in ((32, 5), (16, 4), (8, 3)):
        masks[w] = ((ti >> sh) == (si >> sh) + 1) & ((ti >> (sh + 1)) == (si >> (sh + 1)))
    masks[0] = ((ti >> 3) == (si >> 3)) & (si <= ti)

    gate = p_ref[rows, 3 * D_HG:4 * D_HG]
    gate = gate * _sigmoid(gate)
    gain = gain_ref[...]

    heads = [slice(h * HG_DK, (h + 1) * HG_DK) for h in range(HG_HEADS)]
    sts = [st_ref[h] for h in range(HG_HEADS)]
    o_inter = [_dot_nt(qh[64][:, sl], st.astype(BF16)) for sl, st in zip(heads, sts)]
    scores = []
    for sl in heads:
        sc = jnp.zeros((CHUNK, CHUNK), F32)
        for w in (32, 16, 8, 0):
            sc = jnp.where(masks[w], _dot_nt(qh[w][:, sl], kh[w][:, sl]), sc)
        scores.append(sc.astype(BF16))
    outs = [oi + jnp.dot(sc, v[:, sl], preferred_element_type=F32)
            for oi, sc, sl in zip(o_inter, scores, heads)]
    for h, (sl, st) in enumerate(zip(heads, sts)):
        st_ref[h] = dec_all[:, sl] * st + _dot_tn(v[:, sl], kh[64][:, sl])
    for o, sl in zip(outs, heads):
        ms = jnp.mean(o * o, axis=-1, keepdims=True)
        o = o * lax.rsqrt(ms + EPS) * gain[:, sl]
        o_ref[rows, sl] = (o * gate[:, sl]).astype(BF16)


def _hgrn2(p3, lb, gain, chunks_per_step=11):
    bsz, lp, _ = p3.shape
    rows = chunks_per_step * CHUNK
    return pl.pallas_call(
        _hgrn2_kernel,
        grid=(bsz, lp // rows),
        in_specs=[
            pl.BlockSpec((None, rows, 4 * D_HG), lambda b, c: (b, c, 0)),
            pl.BlockSpec((1, D_HG), lambda b, c: (0, 0)),
            pl.BlockSpec((1, D_HG), lambda b, c: (0, 0)),
        ],
        out_specs=pl.BlockSpec((None, rows, D_HG), lambda b, c: (b, c, 0)),
        out_shape=jax.ShapeDtypeStruct((bsz, lp, D_HG), BF16),
        scratch_shapes=[
            pltpu.VMEM((HG_HEADS, HG_DK, HG_DK), F32),
            pltpu.VMEM((9, D_HG // 128, 8, 128), F32),
        ],
        compiler_params=pltpu.CompilerParams(
            dimension_semantics=("arbitrary", "arbitrary"), vmem_limit_bytes=VMEM_LIMIT),
        name="hgrn2",
    )(p3, lb.reshape(1, D_HG), gain.reshape(1, D_HG))


def _gelu_tanh(x):
    c = 0.7978845608028654
    half = 0.5 * x
    return half + half * jnp.tanh(x * (c + (c * 0.044715) * (x * x)))


def _s5core_kernel(u_ref, t_ref, h_ref, g_ref, a_ref, y_ref, v_ref, xp_ref, *, bsz):
    nchunks = u_ref.shape[0] // bsz
    u = u_ref[...]
    v_ref[...] = jnp.dot(u, h_ref[0], preferred_element_type=F32)
    a_re = jnp.broadcast_to(a_ref[0, 0:1, :], (bsz, 128))
    a_im = jnp.broadcast_to(a_ref[0, 1:2, :], (bsz, 128))

    def step(n, carry):
        x_re, x_im = carry
        rows = pl.ds(pl.multiple_of(n * bsz, bsz), bsz)
        xp_ref[rows, 0:128] = x_re.astype(BF16)
        xp_ref[rows, 128:256] = x_im.astype(BF16)
        return (a_re * x_re - a_im * x_im + v_ref[rows, 0:128],
                a_re * x_im + a_im * x_re + v_ref[rows, 128:256])

    zero = jnp.zeros((bsz, 128), F32)
    lax.fori_loop(0, nchunks, step, (zero, zero), unroll=4)

    ys = jnp.dot(xp_ref[...], g_ref[0], preferred_element_type=F32)
    for k in range(2):
        cols = slice(k * S5_K, (k + 1) * S5_K)
        y_ref[:, cols] = ys[:, cols] + jnp.dot(u[:, cols], t_ref[0, k], preferred_element_type=F32)


def _s5core(uc, params, layer, bsz):
    tmat, hmat, gmat, a2 = params
    rows = uc.shape[0]
    return pl.pallas_call(
        functools.partial(_s5core_kernel, bsz=bsz),
        grid=(S5_PAIRS,),
        in_specs=[
            pl.BlockSpec((rows, 2 * S5_K), lambda p: (0, p)),
            pl.BlockSpec((None, 1, 2, S5_K, S5_K), lambda p: (layer, p, 0, 0, 0)),
            pl.BlockSpec((None, 1, 2 * S5_K, 4 * S5_STATE), lambda p: (layer, p, 0, 0)),
            pl.BlockSpec((None, 1, 4 * S5_STATE, 2 * S5_K), lambda p: (layer, p, 0, 0)),
            pl.BlockSpec((None, 1, 2, 2 * S5_STATE), lambda p: (layer, p, 0, 0)),
        ],
        out_specs=pl.BlockSpec((rows, 2 * S5_K), lambda p: (0, p)),
        out_shape=jax.ShapeDtypeStruct((rows, S5_GROUPS * S5_K), F32),
        scratch_shapes=[
            pltpu.VMEM((rows, 4 * S5_STATE), F32),
            pltpu.VMEM((rows, 4 * S5_STATE), BF16),
        ],
        compiler_params=pltpu.CompilerParams(
            dimension_semantics=("arbitrary",), vmem_limit_bytes=VMEM_LIMIT),
        name="s5core",
    )(uc, tmat, hmat, gmat, a2)


def _atom_transpose(groups):
    lane = lax.broadcasted_iota(jnp.int32, groups[0][0].shape, 1)
    groups = [list(xs) for xs in groups]
    for s in range(3):
        d = 1 << s
        keep = ((lane >> (4 + s)) & 1) == 0
        for xs in groups:
            for i in range(8):
                if i & d:
                    continue
                lo, hi = xs[i], xs[i + d]
                xs[i] = jnp.where(keep, lo, pltpu.roll(hi, S5_GROUP * d, 1))
                xs[i + d] = jnp.where(keep, pltpu.roll(lo, 128 - S5_GROUP * d, 1), hi)
    return groups


def _to_chunks_kernel(u_ref, o_ref, rows_ref):
    bsz, tt, width = u_ref.shape
    pitch = rows_ref.shape[1] // bsz
    for v in range(width // 128):
        for b in range(bsz):
            rows_ref[v, b * pitch:b * pitch + tt, :] = u_ref[b, :, v * 128:(v + 1) * 128]
    halves = [(v, n, k) for v in range(width // 128) for n in range(tt // S5_R) for k in range(2)]
    groups = _atom_transpose(
        [[rows_ref[v, pl.ds(n * S5_R + 8 * k + j, bsz, stride=pitch), :] for j in range(8)]
         for v, n, k in halves])
    for (v, n, k), ys in zip(halves, groups):
        for g, y in enumerate(ys):
            lane0 = (8 * v + g) * S5_K + k * 128
            o_ref[n * bsz:(n + 1) * bsz, lane0:lane0 + 128] = y.astype(BF16)


def _to_chunks(p3, tt=176, width=512):
    bsz, lp, _ = p3.shape
    u_blk0 = 4 * D_HG // width
    return pl.pallas_call(
        _to_chunks_kernel,
        grid=(lp // tt, D_S5 // width),
        in_specs=[pl.BlockSpec((bsz, tt, width), lambda i, v: (0, i, u_blk0 + v))],
        out_specs=pl.BlockSpec((tt, width // S5_GROUP * S5_K), lambda i, v: (i, v)),
        out_shape=jax.ShapeDtypeStruct((lp // S5_R * bsz, S5_GROUPS * S5_K), BF16),
        scratch_shapes=[pltpu.VMEM((width // 128, bsz * (tt + 8), 128), F32)],
        compiler_params=pltpu.CompilerParams(
            dimension_semantics=("arbitrary", "arbitrary"), vmem_limit_bytes=VMEM_LIMIT),
        name="s5_to_chunks",
    )(p3)


def _s5glu_kernel(yc_ref, u_ref, d_ref, w_ref, b_ref, gain_ref, o_ref, yn_ref):
    bsz, tt, _ = u_ref.shape
    pitch = yn_ref.shape[1] // bsz
    for n in range(tt // S5_R):
        halves = [(v, k) for v in range(D_S5 // 128) for k in range(2)]
        groups = _atom_transpose(
            [[yc_ref[n * bsz:(n + 1) * bsz,
                     (8 * v + g) * S5_K + k * 128:(8 * v + g) * S5_K + (k + 1) * 128]
              for g in range(8)] for v, k in halves])
        for (v, k), xs in zip(halves, groups):
            for j, x in enumerate(xs):
                yn_ref[v, pl.ds(n * S5_R + 8 * k + j, bsz, stride=pitch), :] = x
    y = jnp.concatenate(
        [jnp.concatenate([yn_ref[v, b * pitch:b * pitch + tt, :] for v in range(D_S5 // 128)],
                         axis=1) for b in range(bsz)], axis=0)
    y = _gelu_tanh(y + d_ref[...] * u_ref[...].reshape(bsz * tt, D_S5))
    g = jnp.dot(y.astype(BF16), w_ref[...], preferred_element_type=F32) + b_ref[...]
    o_ref[...] = _rms(y * _sigmoid(g), gain_ref[...]).reshape(bsz, tt, D_S5).astype(BF16)


def _s5glu(yc, p3, d_skip, wglu_all, layer, bglu, gain, tt=48):
    bsz, lp, _ = p3.shape
    u_blk = 4 * D_HG // D_S5
    return pl.pallas_call(
        _s5glu_kernel,
        grid=(lp // tt,),
        in_specs=[
            pl.BlockSpec((tt // S5_R * bsz, S5_GROUPS * S5_K), lambda i: (i, 0)),
            pl.BlockSpec((bsz, tt, D_S5), lambda i: (0, i, u_blk)),
            pl.BlockSpec((1, D_S5), lambda i: (0, 0)),
            pl.BlockSpec((None, D_S5, D_S5), lambda i: (layer, 0, 0)),
            pl.BlockSpec((1, D_S5), lambda i: (0, 0)),
            pl.BlockSpec((1, D_S5), lambda i: (0, 0)),
        ],
        out_specs=pl.BlockSpec((bsz, tt, D_S5), lambda i: (0, i, 0)),
        out_shape=jax.ShapeDtypeStruct((bsz, lp, D_S5), BF16),
        scratch_shapes=[pltpu.VMEM((D_S5 // 128, bsz * (tt + 8), 128), F32)],
        compiler_params=pltpu.CompilerParams(
            dimension_semantics=("arbitrary",), vmem_limit_bytes=VMEM_LIMIT),
        name="s5glu",
    )(yc, p3, d_skip.reshape(1, D_S5), wglu_all, bglu.reshape(1, D_S5), gain.reshape(1, D_S5))


def _s5_params(lam_re, lam_im, log_step, b_re, b_im, c_re, c_im):
    hp = lax.Precision.HIGHEST
    a_re = jnp.minimum(lam_re.astype(F32), -1e-4)
    a_im = lam_im.astype(F32)
    dt = jnp.exp(log_step.astype(F32))[:, None]
    mag = jnp.exp(a_re * dt)
    ab_re = mag * jnp.cos(a_im * dt)
    ab_im = mag * jnp.sin(a_im * dt)
    den = a_re * a_re + a_im * a_im
    x_re, x_im = ab_re - 1.0, ab_im
    z_re = (x_re * a_re + x_im * a_im) / den
    z_im = (x_im * a_re - x_re * a_im) / den
    br, bi = b_re.astype(F32), b_im.astype(F32)
    bb_re = z_re[..., None] * br - z_im[..., None] * bi
    bb_im = z_re[..., None] * bi + z_im[..., None] * br
    cr, ci = c_re.astype(F32), c_im.astype(F32)

    pw_re, pw_im = [jnp.ones_like(ab_re)], [jnp.zeros_like(ab_re)]
    for _ in range(S5_R):
        r, i = pw_re[-1], pw_im[-1]
        pw_re.append(r * ab_re - i * ab_im)
        pw_im.append(r * ab_im + i * ab_re)
    p_re, p_im = jnp.stack(pw_re), jnp.stack(pw_im)

    q_re, q_im = p_re[:S5_R, :, :, None], p_im[:S5_R, :, :, None]
    m_re = q_re * bb_re - q_im * bb_im
    m_im = q_re * bb_im + q_im * bb_re
    kern = (jnp.einsum('gop,lgph->ghlo', cr, m_re, precision=hp)
            - jnp.einsum('gop,lgph->ghlo', ci, m_im, precision=hp))
    kz = jnp.concatenate([jnp.zeros((S5_GROUPS, S5_GROUP, S5_K), F32),
                          kern.reshape(S5_GROUPS, S5_GROUP, S5_K)], axis=-1)
    tmat = jnp.stack([kz[:, :, (S5_R - s) * S5_GROUP:(S5_R - s) * S5_GROUP + S5_K]
                      for s in range(S5_R)], axis=1).reshape(S5_PAIRS, 2, S5_K, S5_K)

    bt_re, bt_im = bb_re.transpose(0, 2, 1)[:, None], bb_im.transpose(0, 2, 1)[:, None]
    rp_re = jnp.moveaxis(p_re[S5_R - 1::-1], 0, 1)[:, :, None, :]
    rp_im = jnp.moveaxis(p_im[S5_R - 1::-1], 0, 1)[:, :, None, :]
    h_re = (rp_re * bt_re - rp_im * bt_im).reshape(S5_PAIRS, 2, S5_K, S5_STATE)
    h_im = (rp_re * bt_im + rp_im * bt_re).reshape(S5_PAIRS, 2, S5_K, S5_STATE)
    zh = jnp.zeros_like(h_re[:, 0])
    hmat = jnp.concatenate([
        jnp.concatenate([h_re[:, 0], zh, h_im[:, 0], zh], axis=-1),
        jnp.concatenate([zh, h_re[:, 1], zh, h_im[:, 1]], axis=-1)], axis=1)

    ct_re, ct_im = cr.transpose(0, 2, 1)[:, :, None, :], ci.transpose(0, 2, 1)[:, :, None, :]
    e_re = jnp.moveaxis(p_re[1:], 0, 2)[..., None]
    e_im = jnp.moveaxis(p_im[1:], 0, 2)[..., None]
    g_re = (ct_re * e_re - ct_im * e_im).reshape(S5_PAIRS, 2, S5_STATE, S5_K)
    g_im = -(ct_re * e_im + ct_im * e_re).reshape(S5_PAIRS, 2, S5_STATE, S5_K)
    zg = jnp.zeros_like(g_re[:, 0])
    gmat = jnp.concatenate([
        jnp.concatenate([g_re[:, 0], zg], axis=-1), jnp.concatenate([zg, g_re[:, 1]], axis=-1),
        jnp.concatenate([g_im[:, 0], zg], axis=-1), jnp.concatenate([zg, g_im[:, 1]], axis=-1)],
        axis=1)

    a2 = jnp.stack([p_re[S5_R].reshape(S5_PAIRS, 2 * S5_STATE),
                    p_im[S5_R].reshape(S5_PAIRS, 2 * S5_STATE)], axis=1)
    return tmat.astype(BF16), hmat.astype(BF16), gmat.astype(BF16), a2


def _s5(p3, params, layer, d_skip, wglu_all, bglu, gain):
    yc = _s5core(_to_chunks(p3), params, layer, p3.shape[0])
    return _s5glu(yc, p3, d_skip, wglu_all, layer, bglu, gain)


def _outproj_kernel(h_ref, pre_ref, a_ref, b_ref, w_ref, o_ref, *, first):
    y = (jnp.dot(a_ref[...], w_ref[0], preferred_element_type=F32)
         + jnp.dot(b_ref[...], w_ref[1], preferred_element_type=F32))
    if first:
        npre = pre_ref.shape[0]

        @pl.when(pl.program_id(1) == 0)
        def _():
            o_ref[0:npre, :] = y[0:npre, :] + pre_ref[...]
            o_ref[npre:, :] = y[npre:, :] + h_ref[0:h_ref.shape[0] - npre, :]

        @pl.when(pl.program_id(1) != 0)
        def _():
            o_ref[...] = y + h_ref[...]
    else:
        o_ref[...] = y + h_ref[...]


def _outproj(h, prefix, a, b, w_all, layer, first, bsz, tm=704):
    k = a.shape[1]
    d = h.shape[1]
    lp = a.shape[0] // bsz
    tiles = lp // tm
    flat = lambda b, j: (b * tiles + j, 0)
    return pl.pallas_call(
        functools.partial(_outproj_kernel, first=first),
        grid=(bsz, tiles),
        in_specs=[
            _row_tile_specs(first, bsz, lp, tm, d),
            pl.BlockSpec((CHUNK, d), lambda b, j: (0, 0)),
            pl.BlockSpec((tm, k), flat),
            pl.BlockSpec((tm, k), flat),
            pl.BlockSpec((None, 2, k, d), lambda b, j: (layer, 0, 0, 0)),
        ],
        out_specs=pl.BlockSpec((tm, d), flat),
        out_shape=jax.ShapeDtypeStruct((bsz * lp, d), F32),
        compiler_params=pltpu.CompilerParams(
            dimension_semantics=("arbitrary", "arbitrary"), vmem_limit_bytes=VMEM_LIMIT),
        name="outproj",
    )(h, prefix, a, b, w_all)


FFN_HALO = 16


def _ffn_kernel(h_ref, halo_ref, g_ref, wg_ref, wu_ref, cw_ref, cb_ref, wd_ref, fg_ref, o_ref,
                hn_ref, *, final):
    f_axis = 2 if final else 1
    f = pl.program_id(f_axis)

    @pl.when(f == 0)
    def _():
        x = h_ref[...]
        hn_ref[FFN_HALO:, :] = _rms(x, g_ref[...]).astype(BF16)
        hn_ref[0:FFN_HALO, :] = _rms(halo_ref[...], g_ref[...]).astype(BF16)
        o_ref[...] = x

    a = jnp.dot(hn_ref[...], wg_ref[...], preferred_element_type=F32)
    up = jnp.dot(hn_ref[FFN_HALO:, :], wu_ref[...], preferred_element_type=F32)
    cw = cw_ref[...]
    conv = (cb_ref[...]
            + cw[0:1, :] * pltpu.roll(a, 2, 0)[FFN_HALO:, :]
            + cw[1:2, :] * pltpu.roll(a, 1, 0)[FFN_HALO:, :]
            + cw[2:3, :] * a[FFN_HALO:, :])
    hid = (conv * _sigmoid(conv) * up).astype(BF16)
    o_ref[...] += jnp.dot(hid, wd_ref[...], preferred_element_type=F32)

    if final:
        @pl.when(f == pl.num_programs(f_axis) - 1)
        def _():
            o_ref[...] = _rms(o_ref[...], fg_ref[...])


def _ffn(h, g, layer, wg_all, wu_all, cw_all, cb_all, wd_all, fg, final, bsz, tm=512, tf=512):
    t, d = h.shape
    ff = wg_all.shape[2]
    hb = tm // FFN_HALO
    lp = t // bsz
    if final:
        grid = (bsz, (lp - CHUNK) // tm, ff // tf)
        row0 = lambda b, j, back: pl.multiple_of(b * lp + CHUNK + j * tm - back, FFN_HALO)
        row_specs = [
            pl.BlockSpec((pl.Element(tm), pl.Element(d)), lambda b, j, f: (row0(b, j, 0), 0)),
            pl.BlockSpec((pl.Element(FFN_HALO), pl.Element(d)),
                         lambda b, j, f: (row0(b, j, FFN_HALO), 0)),
        ]
        out_spec = pl.BlockSpec((tm, d), lambda b, j, f: (b * ((lp - CHUNK) // tm) + j, 0))
        out_rows = bsz * (lp - CHUNK)
        wmap = lambda fn: (lambda b, j, f: fn(f))
    else:
        grid = (t // tm, ff // tf)
        row_specs = [
            pl.BlockSpec((tm, d), lambda i, f: (i, 0)),
            pl.BlockSpec((FFN_HALO, d), lambda i, f: (jnp.maximum(i * hb - 1, 0), 0)),
        ]
        out_spec = pl.BlockSpec((tm, d), lambda i, f: (i, 0))
        out_rows = t
        wmap = lambda fn: (lambda i, f: fn(f))
    return pl.pallas_call(
        functools.partial(_ffn_kernel, final=final),
        grid=grid,
        in_specs=row_specs + [
            pl.BlockSpec((1, d), wmap(lambda f: (0, 0))),
            pl.BlockSpec((None, d, tf), wmap(lambda f: (layer, 0, f))),
            pl.BlockSpec((None, d, tf), wmap(lambda f: (layer, 0, f))),
            pl.BlockSpec((None, 3, tf), wmap(lambda f: (layer, 0, f))),
            pl.BlockSpec((None, 1, tf), wmap(lambda f: (layer, 0, f))),
            pl.BlockSpec((None, tf, d), wmap(lambda f: (layer, f, 0))),
            pl.BlockSpec((1, d), wmap(lambda f: (0, 0))),
        ],
        out_specs=out_spec,
        out_shape=jax.ShapeDtypeStruct((out_rows, d), F32),
        scratch_shapes=[pltpu.VMEM((tm + FFN_HALO, d), BF16)],
        compiler_params=pltpu.CompilerParams(
            dimension_semantics=("arbitrary",) * len(grid), vmem_limit_bytes=VMEM_LIMIT),
        name="ffn_final" if final else "ffn",
    )(h, h, g.reshape(1, d), wg_all, wu_all, cw_all, cb_all.reshape(cb_all.shape[0], 1, ff),
      wd_all, fg.reshape(1, d))


def kernel(x, meta_tokens, lb_logits, norm_mix, w_in, hg_norm, s5_lambda_re, s5_lambda_im,
           s5_log_step, s5_b_re, s5_b_im, s5_c_re, s5_c_im, s5_d, w_glu, b_glu, s5_norm, w_out,
           norm_ffn, w_ffn_gate, w_ffn_up, ffn_conv_w, ffn_conv_b, w_ffn_down, final_norm):
    bsz, seq, d = x.shape
    lp = seq + CHUNK
    prefix = jnp.concatenate([jnp.zeros((SEQ_PAD, d), x.dtype), meta_tokens.astype(x.dtype)], axis=0)
    h = x.reshape(bsz * seq, d)

    sm = jax.nn.softmax(lb_logits.astype(F32), axis=0)
    lb_all = jnp.cumsum(sm, axis=0) - sm[0:1]

    w_in_b = w_in.astype(BF16)
    w_out_b = w_out.astype(BF16).reshape(DEPTH, 2, D_HG, d)
    w_glu_b = w_glu.astype(BF16)
    wg_b, wu_b, wd_b = (w.astype(BF16) for w in (w_ffn_gate, w_ffn_up, w_ffn_down))
    s5_params = jax.vmap(_s5_params)(s5_lambda_re, s5_lambda_im, s5_log_step,
                                     s5_b_re, s5_b_im, s5_c_re, s5_c_im)

    for l in range(DEPTH):
        proj = _inproj(h, prefix, norm_mix[l], w_in_b, l, first=(l == 0), bsz=bsz)
        p3 = proj.reshape(bsz, lp, D_IN)
        o_hg = _hgrn2(p3, lb_all[l], hg_norm[l])
        o_s5 = _s5(p3, s5_params, l, s5_d[l], w_glu_b, b_glu[l], s5_norm[l])
        h = _outproj(h, prefix, o_hg.reshape(bsz * lp, D_HG), o_s5.reshape(bsz * lp, D_S5),
                     w_out_b, l, first=(l == 0), bsz=bsz)
        h = _ffn(h, norm_ffn[l], l, wg_b, wu_b, ffn_conv_w, ffn_conv_b, wd_b, final_norm,
                 final=(l == DEPTH - 1), bsz=bsz)
    return h.reshape(bsz, seq, d)
```

```python
import functools

import jax
import jax.numpy as jnp
from jax import lax
from jax.experimental import pallas as pl
from jax.experimental.pallas import tpu as pltpu

F32 = jnp.float32
BF16 = jnp.bfloat16

D_MODEL = 2048
DEPTH = 2
CHUNK = 64
N_META = 16
D_HG = 1024
HG_HEADS = 8
HG_DK = 128
D_S5 = 1024
S5_GROUP = 16
S5_GROUPS = 64
S5_STATE = 64
D_IN = 4 * D_HG + D_S5
D_FF = 5632
EPS = 1e-6
F_FLOOR = 1e-6
LOG2E = 1.4426950408889634

SEQ_PAD = CHUNK - N_META
VMEM_LIMIT = 56 * 1024 * 1024

S5_R = 16
S5_K = S5_R * S5_GROUP
S5_PAIRS = S5_GROUPS // 2


def _rms(x, g):
    ms = jnp.mean(x * x, axis=-1, keepdims=True)
    return x * lax.rsqrt(ms + EPS) * g


def _sigmoid(x):
    return 0.5 + 0.5 * jnp.tanh(0.5 * x)


def _silu(x):
    half = 0.5 * x
    return half + half * jnp.tanh(half)


def _inproj_kernel(h_ref, pre_ref, g_ref, w_ref, o_ref, xn_ref, *, first):
    @pl.when(pl.program_id(2) == 0)
    def _():
        if first:
            npre = pre_ref.shape[0]

            @pl.when(pl.program_id(1) == 0)
            def _():
                xn_ref[0:npre, :] = _rms(pre_ref[...], g_ref[...]).astype(BF16)
                xn_ref[npre:, :] = _rms(h_ref[0:h_ref.shape[0] - npre, :], g_ref[...]).astype(BF16)

            @pl.when(pl.program_id(1) != 0)
            def _():
                xn_ref[...] = _rms(h_ref[...], g_ref[...]).astype(BF16)
        else:
            xn_ref[...] = _rms(h_ref[...], g_ref[...]).astype(BF16)

    o_ref[...] = jnp.dot(xn_ref[...], w_ref[...], preferred_element_type=F32)


def _inproj(h, prefix, g, w_all, layer, first, bsz, tm=1056, tn=1024):
    d = h.shape[1]
    n = w_all.shape[2]
    lp = h.shape[0] // bsz + (CHUNK if first else 0)
    tiles = lp // tm
    return pl.pallas_call(
        functools.partial(_inproj_kernel, first=first),
        grid=(bsz, tiles, n // tn),
        in_specs=[
            _row_tile_specs(first, bsz, lp, tm, d),
            pl.BlockSpec((CHUNK, d), lambda b, i, j: (0, 0)),
            pl.BlockSpec((1, d), lambda b, i, j: (0, 0)),
            pl.BlockSpec((None, d, tn), lambda b, i, j: (layer, 0, j)),
        ],
        out_specs=pl.BlockSpec((tm, tn), lambda b, i, j: (b * tiles + i, j)),
        out_shape=jax.ShapeDtypeStruct((bsz * lp, n), F32),
        scratch_shapes=[pltpu.VMEM((tm, d), BF16)],
        compiler_params=pltpu.CompilerParams(
            dimension_semantics=("arbitrary", "arbitrary", "arbitrary"),
            vmem_limit_bytes=VMEM_LIMIT),
        name="inproj",
    )(h, prefix, g.reshape(1, d), w_all)


def _row_tile_specs(first, bsz, lp, tm, d):
    tiles = lp // tm
    if first:
        def rows(b, j, *_):
            return (pl.multiple_of(b * (lp - CHUNK) + jnp.maximum(j * tm - CHUNK, 0), 8), 0)
        return pl.BlockSpec((pl.Element(tm), pl.Element(d)), rows)
    return pl.BlockSpec((tm, d), lambda b, j, *_: (b * tiles + j, 0))


def _pick_rows(x, m, first):
    j = lax.broadcasted_iota(jnp.int32, x.shape, 0)
    out = None
    for blk in range(8 // m - 1, -1, -1):
        r = blk * m + (0 if first else m - 1)
        row = jnp.broadcast_to(x[r:r + 1, :], x.shape)
        out = row if out is None else jnp.where(j < (blk + 1) * m, row, out)
    return out


def _expand_rows(ref, i, x):
    nv = x.shape[1] // 128
    for v in range(nv):
        ref[i, v] = x[:, v * 128:(v + 1) * 128]
    return jnp.concatenate(
        [jnp.concatenate([ref[i, v, pl.ds(j, 8, stride=0), :] for v in range(nv)], axis=1)
         for j in range(8)], axis=0)


def _dot_nt(a, b):
    return lax.dot_general(a, b, (((1,), (1,)), ((), ())), preferred_element_type=F32)


def _dot_tn(a, b):
    return lax.dot_general(a, b, (((0,), (0,)), ((), ())), preferred_element_type=F32)


def _hgrn2_kernel(p_ref, lb_ref, gain_ref, o_ref, st_ref, bc_ref):
    @pl.when(pl.program_id(1) == 0)
    def _():
        st_ref[...] = jnp.zeros_like(st_ref)

    nsel = CHUNK + 16
    row = lax.broadcasted_iota(jnp.int32, (nsel, 3 * CHUNK), 0)
    col = lax.broadcasted_iota(jnp.int32, (nsel, 3 * CHUNK), 1) & (CHUNK - 1)
    tgt = jnp.where(row < CHUNK, row,
                    jnp.where(row < CHUNK + 8, (row - CHUNK) * 8 + 7, (row - CHUNK - 8) * 8 + 3))
    tri3 = (tgt >= col).astype(BF16)
    ti = lax.broadcasted_iota(jnp.int32, (CHUNK, CHUNK), 0)
    si = lax.broadcasted_iota(jnp.int32, (CHUNK, CHUNK), 1)
    masks = {}
    for w, sh in ((32, 5), (16, 4), (8, 3)):
        masks[w] = ((ti >> sh) == (si >> sh) + 1) & ((ti >> (sh + 1)) == (si >> (sh + 1)))
    masks[0] = ((ti >> 3) == (si >> 3)) & (si <= ti)

    def chunk(c, carry):
        _hgrn2_chunk(p_ref, lb_ref, gain_ref, o_ref, st_ref, bc_ref, tri3, masks,
                     pl.ds(pl.multiple_of(c * CHUNK, CHUNK), CHUNK))
        return carry

    lax.fori_loop(0, p_ref.shape[0] // CHUNK, chunk, 0)


def _hgrn2_chunk(p_ref, lb_ref, gain_ref, o_ref, st_ref, bc_ref, tri3, masks, rows):
    q = p_ref[rows, 0:D_HG]
    z = p_ref[rows, D_HG:2 * D_HG]
    v = p_ref[rows, 2 * D_HG:3 * D_HG].astype(BF16)
    lb = lb_ref[...]

    th = 0.5 * jnp.tanh(0.5 * z)
    gl = jnp.log2(jnp.maximum(lb + (1.0 - lb) * (0.5 + th), F_FLOOR))
    kk = (1.0 - lb) * (0.5 - th)
    qf = _silu(q)

    hi = gl.astype(BF16)
    r1 = gl - hi.astype(F32)
    mid = r1.astype(BF16)
    lo = (r1 - mid.astype(F32)).astype(BF16)
    cum = jnp.dot(tri3, jnp.concatenate([hi, mid, lo], axis=0), preferred_element_type=F32)
    b = cum[0:CHUNK]
    l8 = cum[CHUNK:CHUNK + 8]
    m8 = cum[CHUNK + 8:CHUNK + 16]
    j8 = lax.broadcasted_iota(jnp.int32, l8.shape, 0)
    s8 = jnp.where(j8 == 0, 0.0, pltpu.roll(l8, 1, 0))

    qe = qf * jnp.exp2(b - _expand_rows(bc_ref, 0, s8))
    kf = kk * jnp.exp2(_expand_rows(bc_ref, 1, l8) - b)
    qh = {8: qe.astype(BF16)}
    kh = {8: kf.astype(BF16)}
    for lvl, (w, m) in enumerate(((16, 2), (32, 4), (64, 8))):
        cq = jnp.exp2(s8 - _pick_rows(s8, m, True))
        ck = jnp.exp2(_pick_rows(l8, m, False) - l8)
        qh[w] = (qe * _expand_rows(bc_ref, 3 + 2 * lvl, cq)).astype(BF16)
        kh[w] = (kf * _expand_rows(bc_ref, 4 + 2 * lvl, ck)).astype(BF16)
    b_mid = _expand_rows(bc_ref, 2, m8)
    qh[0] = (qf * jnp.exp2(b - b_mid)).astype(BF16)
    kh[0] = (kk * jnp.exp2(b_mid - b)).astype(BF16)
    dec_all = jnp.exp2(l8[7:8, :])

    gate = p_ref[rows, 3 * D_HG:4 * D_HG]
    gate = _silu(gate)
    gain = gain_ref[...]

    heads = [slice(h * HG_DK, (h + 1) * HG_DK) for h in range(HG_HEADS)]
    sts = [st_ref[h] for h in range(HG_HEADS)]
    o_inter = [_dot_nt(qh[64][:, sl], st.astype(BF16)) for sl, st in zip(heads, sts)]
    scores = []
    for sl in heads:
        sc = jnp.zeros((CHUNK, CHUNK), F32)
        for w in (32, 16, 8, 0):
            sc = jnp.where(masks[w], _dot_nt(qh[w][:, sl], kh[w][:, sl]), sc)
        scores.append(sc.astype(BF16))
    outs = [oi + jnp.dot(sc, v[:, sl], preferred_element_type=F32)
            for oi, sc, sl in zip(o_inter, scores, heads)]
    for h, (sl, st) in enumerate(zip(heads, sts)):
        st_ref[h] = dec_all[:, sl] * st + _dot_tn(v[:, sl], kh[64][:, sl])
    for o, sl in zip(outs, heads):
        ms = jnp.mean(o * o, axis=-1, keepdims=True)
        o = o * lax.rsqrt(ms + EPS) * gain[:, sl]
        o_ref[rows, sl] = (o * gate[:, sl]).astype(BF16)


def _hgrn2(p3, lb, gain, chunks_per_step=11):
    bsz, lp, _ = p3.shape
    rows = chunks_per_step * CHUNK
    return pl.pallas_call(
        _hgrn2_kernel,
        grid=(bsz, lp // rows),
        in_specs=[
            pl.BlockSpec((None, rows, 4 * D_HG), lambda b, c: (b, c, 0)),
            pl.BlockSpec((1, D_HG), lambda b, c: (0, 0)),
            pl.BlockSpec((1, D_HG), lambda b, c: (0, 0)),
        ],
        out_specs=pl.BlockSpec((None, rows, D_HG), lambda b, c: (b, c, 0)),
        out_shape=jax.ShapeDtypeStruct((bsz, lp, D_HG), BF16),
        scratch_shapes=[
            pltpu.VMEM((HG_HEADS, HG_DK, HG_DK), F32),
            pltpu.VMEM((9, D_HG // 128, 8, 128), F32),
        ],
        compiler_params=pltpu.CompilerParams(
            dimension_semantics=("arbitrary", "arbitrary"), vmem_limit_bytes=VMEM_LIMIT),
        name="hgrn2",
    )(p3, lb.reshape(1, D_HG), gain.reshape(1, D_HG))


def _gelu_tanh(x):
    c = 0.7978845608028654
    half = 0.5 * x
    return half + half * jnp.tanh(x * (c + (c * 0.044715) * (x * x)))


def _s5core_kernel(u_ref, t_ref, h_ref, g_ref, a_ref, y_ref, v_ref, xp_ref, *, bsz):
    nchunks = u_ref.shape[0] // bsz
    u = u_ref[...]
    v_ref[...] = jnp.dot(u, h_ref[0], preferred_element_type=F32)
    a_re = jnp.broadcast_to(a_ref[0, 0:1, :], (bsz, 128))
    a_im = jnp.broadcast_to(a_ref[0, 1:2, :], (bsz, 128))

    def step(n, carry):
        x_re, x_im = carry
        rows = pl.ds(pl.multiple_of(n * bsz, bsz), bsz)
        xp_ref[rows, 0:128] = x_re.astype(BF16)
        xp_ref[rows, 128:256] = x_im.astype(BF16)
        return (a_re * x_re - a_im * x_im + v_ref[rows, 0:128],
                a_re * x_im + a_im * x_re + v_ref[rows, 128:256])

    zero = jnp.zeros((bsz, 128), F32)
    lax.fori_loop(0, nchunks, step, (zero, zero), unroll=4)

    ys = jnp.dot(xp_ref[...], g_ref[0], preferred_element_type=F32)
    for k in range(2):
        cols = slice(k * S5_K, (k + 1) * S5_K)
        y_ref[:, cols] = ys[:, cols] + jnp.dot(u[:, cols], t_ref[0, k], preferred_element_type=F32)


def _s5core(uc, params, layer, bsz):
    tmat, hmat, gmat, a2 = params
    rows = uc.shape[0]
    return pl.pallas_call(
        functools.partial(_s5core_kernel, bsz=bsz),
        grid=(S5_PAIRS,),
        in_specs=[
            pl.BlockSpec((rows, 2 * S5_K), lambda p: (0, p)),
            pl.BlockSpec((None, 1, 2, S5_K, S5_K), lambda p: (layer, p, 0, 0, 0)),
            pl.BlockSpec((None, 1, 2 * S5_K, 4 * S5_STATE), lambda p: (layer, p, 0, 0)),
            pl.BlockSpec((None, 1, 4 * S5_STATE, 2 * S5_K), lambda p: (layer, p, 0, 0)),
            pl.BlockSpec((None, 1, 2, 2 * S5_STATE), lambda p: (layer, p, 0, 0)),
        ],
        out_specs=pl.BlockSpec((rows, 2 * S5_K), lambda p: (0, p)),
        out_shape=jax.ShapeDtypeStruct((rows, S5_GROUPS * S5_K), F32),
        scratch_shapes=[
            pltpu.VMEM((rows, 4 * S5_STATE), F32),
            pltpu.VMEM((rows, 4 * S5_STATE), BF16),
        ],
        compiler_params=pltpu.CompilerParams(
            dimension_semantics=("arbitrary",), vmem_limit_bytes=VMEM_LIMIT),
        name="s5core",
    )(uc, tmat, hmat, gmat, a2)


def _atom_transpose(groups):
    lane = lax.broadcasted_iota(jnp.int32, groups[0][0].shape, 1)
    groups = [list(xs) for xs in groups]
    for s in range(3):
        d = 1 << s
        keep = ((lane >> (4 + s)) & 1) == 0
        for xs in groups:
            for i in range(8):
                if i & d:
                    continue
                lo, hi = xs[i], xs[i + d]
                xs[i] = jnp.where(keep, lo, pltpu.roll(hi, S5_GROUP * d, 1))
                xs[i + d] = jnp.where(keep, pltpu.roll(lo, 128 - S5_GROUP * d, 1), hi)
    return groups


def _to_chunks_kernel(u_ref, o_ref, rows_ref):
    bsz, tt, width = u_ref.shape
    pitch = rows_ref.shape[1] // bsz
    for v in range(width // 128):
        for b in range(bsz):
            rows_ref[v, b * pitch:b * pitch + tt, :] = u_ref[b, :, v * 128:(v + 1) * 128]
    halves = [(v, n, k) for v in range(width // 128) for n in range(tt // S5_R) for k in range(2)]
    groups = _atom_transpose(
        [[rows_ref[v, pl.ds(n * S5_R + 8 * k + j, bsz, stride=pitch), :] for j in range(8)]
         for v, n, k in halves])
    for (v, n, k), ys in zip(halves, groups):
        for g, y in enumerate(ys):
            lane0 = (8 * v + g) * S5_K + k * 128
            o_ref[n * bsz:(n + 1) * bsz, lane0:lane0 + 128] = y.astype(BF16)


def _to_chunks(p3, tt=176, width=512):
    bsz, lp, _ = p3.shape
    u_blk0 = 4 * D_HG // width
    return pl.pallas_call(
        _to_chunks_kernel,
        grid=(lp // tt, D_S5 // width),
        in_specs=[pl.BlockSpec((bsz, tt, width), lambda i, v: (0, i, u_blk0 + v))],
        out_specs=pl.BlockSpec((tt, width // S5_GROUP * S5_K), lambda i, v: (i, v)),
        out_shape=jax.ShapeDtypeStruct((lp // S5_R * bsz, S5_GROUPS * S5_K), BF16),
        scratch_shapes=[pltpu.VMEM((width // 128, bsz * (tt + 8), 128), F32)],
        compiler_params=pltpu.CompilerParams(
            dimension_semantics=("arbitrary", "arbitrary"), vmem_limit_bytes=VMEM_LIMIT),
        name="s5_to_chunks",
    )(p3)


def _s5glu_kernel(yc_ref, u_ref, d_ref, w_ref, b_ref, gain_ref, o_ref, yn_ref):
    bsz, tt, _ = u_ref.shape
    pitch = yn_ref.shape[1] // bsz
    for n in range(tt // S5_R):
        halves = [(v, k) for v in range(D_S5 // 128) for k in range(2)]
        groups = _atom_transpose(
            [[yc_ref[n * bsz:(n + 1) * bsz,
                     (8 * v + g) * S5_K + k * 128:(8 * v + g) * S5_K + (k + 1) * 128]
              for g in range(8)] for v, k in halves])
        for (v, k), xs in zip(halves, groups):
            for j, x in enumerate(xs):
                yn_ref[v, pl.ds(n * S5_R + 8 * k + j, bsz, stride=pitch), :] = x
    y = jnp.concatenate(
        [jnp.concatenate([yn_ref[v, b * pitch:b * pitch + tt, :] for v in range(D_S5 // 128)],
                         axis=1) for b in range(bsz)], axis=0)
    y = _gelu_tanh(y + d_ref[...] * u_ref[...].reshape(bsz * tt, D_S5))
    g = jnp.dot(y.astype(BF16), w_ref[...], preferred_element_type=F32) + b_ref[...]
    o_ref[...] = _rms(y * _sigmoid(g), gain_ref[...]).reshape(bsz, tt, D_S5).astype(BF16)


def _s5glu(yc, p3, d_skip, wglu_all, layer, bglu, gain, tt=48):
    bsz, lp, _ = p3.shape
    u_blk = 4 * D_HG // D_S5
    return pl.pallas_call(
        _s5glu_kernel,
        grid=(lp // tt,),
        in_specs=[
            pl.BlockSpec((tt // S5_R * bsz, S5_GROUPS * S5_K), lambda i: (i, 0)),
            pl.BlockSpec((bsz, tt, D_S5), lambda i: (0, i, u_blk)),
            pl.BlockSpec((1, D_S5), lambda i: (0, 0)),
            pl.BlockSpec((None, D_S5, D_S5), lambda i: (layer, 0, 0)),
            pl.BlockSpec((1, D_S5), lambda i: (0, 0)),
            pl.BlockSpec((1, D_S5), lambda i: (0, 0)),
        ],
        out_specs=pl.BlockSpec((bsz, tt, D_S5), lambda i: (0, i, 0)),
        out_shape=jax.ShapeDtypeStruct((bsz, lp, D_S5), BF16),
        scratch_shapes=[pltpu.VMEM((D_S5 // 128, bsz * (tt + 8), 128), F32)],
        compiler_params=pltpu.CompilerParams(
            dimension_semantics=("arbitrary",), vmem_limit_bytes=VMEM_LIMIT),
        name="s5glu",
    )(yc, p3, d_skip.reshape(1, D_S5), wglu_all, bglu.reshape(1, D_S5), gain.reshape(1, D_S5))


def _s5_params(lam_re, lam_im, log_step, b_re, b_im, c_re, c_im):
    hp = lax.Precision.HIGHEST
    a_re = jnp.minimum(lam_re.astype(F32), -1e-4)
    a_im = lam_im.astype(F32)
    dt = jnp.exp(log_step.astype(F32))[:, None]
    mag = jnp.exp(a_re * dt)
    ab_re = mag * jnp.cos(a_im * dt)
    ab_im = mag * jnp.sin(a_im * dt)
    den = a_re * a_re + a_im * a_im
    x_re, x_im = ab_re - 1.0, ab_im
    z_re = (x_re * a_re + x_im * a_im) / den
    z_im = (x_im * a_re - x_re * a_im) / den
    br, bi = b_re.astype(F32), b_im.astype(F32)
    bb_re = z_re[..., None] * br - z_im[..., None] * bi
    bb_im = z_re[..., None] * bi + z_im[..., None] * br
    cr, ci = c_re.astype(F32), c_im.astype(F32)

    pw_re, pw_im = [jnp.ones_like(ab_re)], [jnp.zeros_like(ab_re)]
    for _ in range(S5_R):
        r, i = pw_re[-1], pw_im[-1]
        pw_re.append(r * ab_re - i * ab_im)
        pw_im.append(r * ab_im + i * ab_re)
    p_re, p_im = jnp.stack(pw_re), jnp.stack(pw_im)

    q_re, q_im = p_re[:S5_R, :, :, None], p_im[:S5_R, :, :, None]
    m_re = q_re * bb_re - q_im * bb_im
    m_im = q_re * bb_im + q_im * bb_re
    kern = (jnp.einsum('gop,lgph->ghlo', cr, m_re, precision=hp)
            - jnp.einsum('gop,lgph->ghlo', ci, m_im, precision=hp))
    kz = jnp.concatenate([jnp.zeros((S5_GROUPS, S5_GROUP, S5_K), F32),
                          kern.reshape(S5_GROUPS, S5_GROUP, S5_K)], axis=-1)
    tmat = jnp.stack([kz[:, :, (S5_R - s) * S5_GROUP:(S5_R - s) * S5_GROUP + S5_K]
                      for s in range(S5_R)], axis=1).reshape(S5_PAIRS, 2, S5_K, S5_K)

    bt_re, bt_im = bb_re.transpose(0, 2, 1)[:, None], bb_im.transpose(0, 2, 1)[:, None]
    rp_re = jnp.moveaxis(p_re[S5_R - 1::-1], 0, 1)[:, :, None, :]
    rp_im = jnp.moveaxis(p_im[S5_R - 1::-1], 0, 1)[:, :, None, :]
    h_re = (rp_re * bt_re - rp_im * bt_im).reshape(S5_PAIRS, 2, S5_K, S5_STATE)
    h_im = (rp_re * bt_im + rp_im * bt_re).reshape(S5_PAIRS, 2, S5_K, S5_STATE)
    zh = jnp.zeros_like(h_re[:, 0])
    hmat = jnp.concatenate([
        jnp.concatenate([h_re[:, 0], zh, h_im[:, 0], zh], axis=-1),
        jnp.concatenate([zh, h_re[:, 1], zh, h_im[:, 1]], axis=-1)], axis=1)

    ct_re, ct_im = cr.transpose(0, 2, 1)[:, :, None, :], ci.transpose(0, 2, 1)[:, :, None, :]
    e_re = jnp.moveaxis(p_re[1:], 0, 2)[..., None]
    e_im = jnp.moveaxis(p_im[1:], 0, 2)[..., None]
    g_re = (ct_re * e_re - ct_im * e_im).reshape(S5_PAIRS, 2, S5_STATE, S5_K)
    g_im = -(ct_re * e_im + ct_im * e_re).reshape(S5_PAIRS, 2, S5_STATE, S5_K)
    zg = jnp.zeros_like(g_re[:, 0])
    gmat = jnp.concatenate([
        jnp.concatenate([g_re[:, 0], zg], axis=-1), jnp.concatenate([zg, g_re[:, 1]], axis=-1),
        jnp.concatenate([g_im[:, 0], zg], axis=-1), jnp.concatenate([zg, g_im[:, 1]], axis=-1)],
        axis=1)

    a2 = jnp.stack([p_re[S5_R].reshape(S5_PAIRS, 2 * S5_STATE),
                    p_im[S5_R].reshape(S5_PAIRS, 2 * S5_STATE)], axis=1)
    return tmat.astype(BF16), hmat.astype(BF16), gmat.astype(BF16), a2


def _s5(p3, params, layer, d_skip, wglu_all, bglu, gain):
    yc = _s5core(_to_chunks(p3), params, layer, p3.shape[0])
    return _s5glu(yc, p3, d_skip, wglu_all, layer, bglu, gain)


def _outproj_kernel(h_ref, pre_ref, a_ref, b_ref, w_ref, o_ref, *, first):
    y = (jnp.dot(a_ref[...], w_ref[0], preferred_element_type=F32)
         + jnp.dot(b_ref[...], w_ref[1], preferred_element_type=F32))
    if first:
        npre = pre_ref.shape[0]

        @pl.when(pl.program_id(1) == 0)
        def _():
            o_ref[0:npre, :] = y[0:npre, :] + pre_ref[...]
            o_ref[npre:, :] = y[npre:, :] + h_ref[0:h_ref.shape[0] - npre, :]

        @pl.when(pl.program_id(1) != 0)
        def _():
            o_ref[...] = y + h_ref[...]
    else:
        o_ref[...] = y + h_ref[...]


def _outproj(h, prefix, a, b, w_all, layer, first, bsz, tm=704):
    k = a.shape[1]
    d = h.shape[1]
    lp = a.shape[0] // bsz
    tiles = lp // tm
    flat = lambda b, j: (b * tiles + j, 0)
    return pl.pallas_call(
        functools.partial(_outproj_kernel, first=first),
        grid=(bsz, tiles),
        in_specs=[
            _row_tile_specs(first, bsz, lp, tm, d),
            pl.BlockSpec((CHUNK, d), lambda b, j: (0, 0)),
            pl.BlockSpec((tm, k), flat),
            pl.BlockSpec((tm, k), flat),
            pl.BlockSpec((None, 2, k, d), lambda b, j: (layer, 0, 0, 0)),
        ],
        out_specs=pl.BlockSpec((tm, d), flat),
        out_shape=jax.ShapeDtypeStruct((bsz * lp, d), F32),
        compiler_params=pltpu.CompilerParams(
            dimension_semantics=("arbitrary", "arbitrary"), vmem_limit_bytes=VMEM_LIMIT),
        name="outproj",
    )(h, prefix, a, b, w_all)


FFN_HALO = 16


def _ffn_kernel(h_ref, halo_ref, g_ref, wg_ref, wu_ref, cw_ref, cb_ref, wd_ref, fg_ref, o_ref,
                hn_ref, *, final):
    f_axis = 2 if final else 1
    f = pl.program_id(f_axis)

    @pl.when(f == 0)
    def _():
        x = h_ref[...]
        hn_ref[FFN_HALO:, :] = _rms(x, g_ref[...]).astype(BF16)
        hn_ref[0:FFN_HALO, :] = _rms(halo_ref[...], g_ref[...]).astype(BF16)
        o_ref[...] = x

    a = jnp.dot(hn_ref[...], wg_ref[...], preferred_element_type=F32)
    up = jnp.dot(hn_ref[FFN_HALO:, :], wu_ref[...], preferred_element_type=F32)
    cw = cw_ref[...]
    conv = (cb_ref[...]
            + cw[0:1, :] * pltpu.roll(a, 2, 0)[FFN_HALO:, :]
            + cw[1:2, :] * pltpu.roll(a, 1, 0)[FFN_HALO:, :]
            + cw[2:3, :] * a[FFN_HALO:, :])
    hid = (_silu(conv) * up).astype(BF16)
    o_ref[...] += jnp.dot(hid, wd_ref[...], preferred_element_type=F32)

    if final:
        @pl.when(f == pl.num_programs(f_axis) - 1)
        def _():
            o_ref[...] = _rms(o_ref[...], fg_ref[...])


def _ffn(h, g, layer, wg_all, wu_all, cw_all, cb_all, wd_all, fg, final, bsz, tm=512, tf=512):
    t, d = h.shape
    ff = wg_all.shape[2]
    hb = tm // FFN_HALO
    lp = t // bsz
    if final:
        grid = (bsz, (lp - CHUNK) // tm, ff // tf)
        row0 = lambda b, j, back: pl.multiple_of(b * lp + CHUNK + j * tm - back, FFN_HALO)
        row_specs = [
            pl.BlockSpec((pl.Element(tm), pl.Element(d)), lambda b, j, f: (row0(b, j, 0), 0)),
            pl.BlockSpec((pl.Element(FFN_HALO), pl.Element(d)),
                         lambda b, j, f: (row0(b, j, FFN_HALO), 0)),
        ]
        out_spec = pl.BlockSpec((tm, d), lambda b, j, f: (b * ((lp - CHUNK) // tm) + j, 0))
        out_rows = bsz * (lp - CHUNK)
        wmap = lambda fn: (lambda b, j, f: fn(f))
    else:
        grid = (t // tm, ff // tf)
        row_specs = [
            pl.BlockSpec((tm, d), lambda i, f: (i, 0)),
            pl.BlockSpec((FFN_HALO, d), lambda i, f: (jnp.maximum(i * hb - 1, 0), 0)),
        ]
        out_spec = pl.BlockSpec((tm, d), lambda i, f: (i, 0))
        out_rows = t
        wmap = lambda fn: (lambda i, f: fn(f))
    return pl.pallas_call(
        functools.partial(_ffn_kernel, final=final),
        grid=grid,
        in_specs=row_specs + [
            pl.BlockSpec((1, d), wmap(lambda f: (0, 0))),
            pl.BlockSpec((None, d, tf), wmap(lambda f: (layer, 0, f))),
            pl.BlockSpec((None, d, tf), wmap(lambda f: (layer, 0, f))),
            pl.BlockSpec((None, 3, tf), wmap(lambda f: (layer, 0, f))),
            pl.BlockSpec((None, 1, tf), wmap(lambda f: (layer, 0, f))),
            pl.BlockSpec((None, tf, d), wmap(lambda f: (layer, f, 0))),
            pl.BlockSpec((1, d), wmap(lambda f: (0, 0))),
        ],
        out_specs=out_spec,
        out_shape=jax.ShapeDtypeStruct((out_rows, d), F32),
        scratch_shapes=[pltpu.VMEM((tm + FFN_HALO, d), BF16)],
        compiler_params=pltpu.CompilerParams(
            dimension_semantics=("arbitrary",) * len(grid), vmem_limit_bytes=VMEM_LIMIT),
        name="ffn_final" if final else "ffn",
    )(h, h, g.reshape(1, d), wg_all, wu_all, cw_all, cb_all.reshape(cb_all.shape[0], 1, ff),
      wd_all, fg.reshape(1, d))


def kernel(x, meta_tokens, lb_logits, norm_mix, w_in, hg_norm, s5_lambda_re, s5_lambda_im,
           s5_log_step, s5_b_re, s5_b_im, s5_c_re, s5_c_im, s5_d, w_glu, b_glu, s5_norm, w_out,
           norm_ffn, w_ffn_gate, w_ffn_up, ffn_conv_w, ffn_conv_b, w_ffn_down, final_norm):
    bsz, seq, d = x.shape
    lp = seq + CHUNK
    prefix = jnp.concatenate([jnp.zeros((SEQ_PAD, d), x.dtype), meta_tokens.astype(x.dtype)], axis=0)
    h = x.reshape(bsz * seq, d)

    sm = jax.nn.softmax(lb_logits.astype(F32), axis=0)
    lb_all = jnp.cumsum(sm, axis=0) - sm[0:1]

    w_in_b = w_in.astype(BF16)
    w_out_b = w_out.astype(BF16).reshape(DEPTH, 2, D_HG, d)
    w_glu_b = w_glu.astype(BF16)
    wg_b, wu_b, wd_b = (w.astype(BF16) for w in (w_ffn_gate, w_ffn_up, w_ffn_down))
    s5_params = jax.vmap(_s5_params)(s5_lambda_re, s5_lambda_im, s5_log_step,
                                     s5_b_re, s5_b_im, s5_c_re, s5_c_im)

    for l in range(DEPTH):
        proj = _inproj(h, prefix, norm_mix[l], w_in_b, l, first=(l == 0), bsz=bsz)
        p3 = proj.reshape(bsz, lp, D_IN)
        o_hg = _hgrn2(p3, lb_all[l], hg_norm[l])
        o_s5 = _s5(p3, s5_params, l, s5_d[l], w_glu_b, b_glu[l], s5_norm[l])
        h = _outproj(h, prefix, o_hg.reshape(bsz * lp, D_HG), o_s5.reshape(bsz * lp, D_S5),
                     w_out_b, l, first=(l == 0), bsz=bsz)
        h = _ffn(h, norm_ffn[l], l, wg_b, wu_b, ffn_conv_w, ffn_conv_b, wd_b, final_norm,
                 final=(l == DEPTH - 1), bsz=bsz, tm=512 if l == DEPTH - 1 else 768)
    return h.reshape(bsz, seq, d)
```

```python
import functools

import jax
import jax.numpy as jnp
from jax import lax
from jax.experimental import pallas as pl
from jax.experimental.pallas import tpu as pltpu

F32 = jnp.float32
BF16 = jnp.bfloat16

D_MODEL = 2048
DEPTH = 2
CHUNK = 64
N_META = 16
D_HG = 1024
HG_HEADS = 8
HG_DK = 128
D_S5 = 1024
S5_GROUP = 16
S5_GROUPS = 64
S5_STATE = 64
D_IN = 4 * D_HG + D_S5
D_FF = 5632
EPS = 1e-6
F_FLOOR = 1e-6
LOG2E = 1.4426950408889634

SEQ_PAD = CHUNK - N_META
VMEM_LIMIT = 56 * 1024 * 1024
VMEM_LIMIT_BIG = 62 * 1024 * 1024

S5_R = 16
S5_K = S5_R * S5_GROUP
S5_PAIRS = S5_GROUPS // 2


def _rms(x, g):
    ms = jnp.mean(x * x, axis=-1, keepdims=True)
    return x * lax.rsqrt(ms + EPS) * g


def _sigmoid(x):
    return 0.5 + 0.5 * jnp.tanh(0.5 * x)


def _silu(x):
    half = 0.5 * x
    return half + half * jnp.tanh(half)


def _normed_tile(h_ref, pre_ref, g_ref, xn_ref, first):
    if first:
        npre = pre_ref.shape[0]

        @pl.when(pl.program_id(1) == 0)
        def _():
            xn_ref[0:npre, :] = _rms(pre_ref[...], g_ref[...]).astype(BF16)
            xn_ref[npre:, :] = _rms(h_ref[0:h_ref.shape[0] - npre, :], g_ref[...]).astype(BF16)

        @pl.when(pl.program_id(1) != 0)
        def _():
            xn_ref[...] = _rms(h_ref[...], g_ref[...]).astype(BF16)
    else:
        xn_ref[...] = _rms(h_ref[...], g_ref[...]).astype(BF16)


def _row_tile_specs(first, bsz, lp, tm, d):
    tiles = lp // tm
    if first:
        def rows(b, j, *_):
            return (pl.multiple_of(b * (lp - CHUNK) + jnp.maximum(j * tm - CHUNK, 0), 8), 0)
        return pl.BlockSpec((pl.Element(tm), pl.Element(d)), rows)
    return pl.BlockSpec((tm, d), lambda b, j, *_: (b * tiles + j, 0))


def _pick_rows(x, m, first):
    j = lax.broadcasted_iota(jnp.int32, x.shape, 0)
    out = None
    for blk in range(8 // m - 1, -1, -1):
        r = blk * m + (0 if first else m - 1)
        row = jnp.broadcast_to(x[r:r + 1, :], x.shape)
        out = row if out is None else jnp.where(j < (blk + 1) * m, row, out)
    return out


def _expand_rows(ref, i, x):
    nv = x.shape[1] // 128
    for v in range(nv):
        ref[i, v] = x[:, v * 128:(v + 1) * 128]
    return jnp.concatenate(
        [jnp.concatenate([ref[i, v, pl.ds(j, 8, stride=0), :] for v in range(nv)], axis=1)
         for j in range(8)], axis=0)


def _dot_nt(a, b):
    return lax.dot_general(a, b, (((1,), (1,)), ((), ())), preferred_element_type=F32)


def _dot_tn(a, b):
    return lax.dot_general(a, b, (((0,), (0,)), ((), ())), preferred_element_type=F32)


def _proj_hgrn2_kernel(h_ref, pre_ref, g_ref, w_ref, lb_ref, gain_ref, o_ref, u_ref,
                       xn_ref, p_ref, st_ref, bc_ref, *, first):
    @pl.when(pl.program_id(1) == 0)
    def _():
        st_ref[...] = jnp.zeros_like(st_ref)

    _normed_tile(h_ref, pre_ref, g_ref, xn_ref, first)
    xn = xn_ref[...]
    for k in range(4 * D_HG // D_S5):
        cols = slice(k * D_S5, (k + 1) * D_S5)
        p_ref[:, cols] = jnp.dot(xn, w_ref[:, cols], preferred_element_type=F32)
    u_ref[...] = jnp.dot(xn, w_ref[:, 4 * D_HG:], preferred_element_type=F32)

    nsel = CHUNK + 16
    row = lax.broadcasted_iota(jnp.int32, (nsel, 3 * CHUNK), 0)
    col = lax.broadcasted_iota(jnp.int32, (nsel, 3 * CHUNK), 1) & (CHUNK - 1)
    tgt = jnp.where(row < CHUNK, row,
                    jnp.where(row < CHUNK + 8, (row - CHUNK) * 8 + 7, (row - CHUNK - 8) * 8 + 3))
    tri3 = (tgt >= col).astype(BF16)
    ti = lax.broadcasted_iota(jnp.int32, (CHUNK, CHUNK), 0)
    si = lax.broadcasted_iota(jnp.int32, (CHUNK, CHUNK), 1)
    masks = {}
    for w, sh in ((32, 5), (16, 4), (8, 3)):
        masks[w] = ((ti >> sh) == (si >> sh) + 1) & ((ti >> (sh + 1)) == (si >> (sh + 1)))
    masks[0] = ((ti >> 3) == (si >> 3)) & (si <= ti)

    def chunk(c, carry):
        _hgrn2_chunk(p_ref, lb_ref, gain_ref, o_ref, st_ref, bc_ref, tri3, masks,
                     pl.ds(pl.multiple_of(c * CHUNK, CHUNK), CHUNK))
        return carry

    lax.fori_loop(0, p_ref.shape[0] // CHUNK, chunk, 0)


def _hgrn2_chunk(p_ref, lb_ref, gain_ref, o_ref, st_ref, bc_ref, tri3, masks, rows):
    q = p_ref[rows, 0:D_HG]
    z = p_ref[rows, D_HG:2 * D_HG]
    v = p_ref[rows, 2 * D_HG:3 * D_HG].astype(BF16)
    lb = lb_ref[...]

    th = 0.5 * jnp.tanh(0.5 * z)
    gl = jnp.log2(jnp.maximum(lb + (1.0 - lb) * (0.5 + th), F_FLOOR))
    kk = (1.0 - lb) * (0.5 - th)
    qf = _silu(q)

    hi = gl.astype(BF16)
    r1 = gl - hi.astype(F32)
    mid = r1.astype(BF16)
    lo = (r1 - mid.astype(F32)).astype(BF16)
    cum = jnp.dot(tri3, jnp.concatenate([hi, mid, lo], axis=0), preferred_element_type=F32)
    b = cum[0:CHUNK]
    l8 = cum[CHUNK:CHUNK + 8]
    m8 = cum[CHUNK + 8:CHUNK + 16]
    j8 = lax.broadcasted_iota(jnp.int32, l8.shape, 0)
    s8 = jnp.where(j8 == 0, 0.0, pltpu.roll(l8, 1, 0))

    qe = qf * jnp.exp2(b - _expand_rows(bc_ref, 0, s8))
    kf = kk * jnp.exp2(_expand_rows(bc_ref, 1, l8) - b)
    qh = {8: qe.astype(BF16)}
    kh = {8: kf.astype(BF16)}
    for lvl, (w, m) in enumerate(((16, 2), (32, 4), (64, 8))):
        cq = jnp.exp2(s8 - _pick_rows(s8, m, True))
        ck = jnp.exp2(_pick_rows(l8, m, False) - l8)
        qh[w] = (qe * _expand_rows(bc_ref, 3 + 2 * lvl, cq)).astype(BF16)
        kh[w] = (kf * _expand_rows(bc_ref, 4 + 2 * lvl, ck)).astype(BF16)
    b_mid = _expand_rows(bc_ref, 2, m8)
    qh[0] = (qf * jnp.exp2(b - b_mid)).astype(BF16)
    kh[0] = (kk * jnp.exp2(b_mid - b)).astype(BF16)
    dec_all = jnp.exp2(l8[7:8, :])

    gate = p_ref[rows, 3 * D_HG:4 * D_HG]
    gate = _silu(gate)
    gain = gain_ref[...]

    heads = [slice(h * HG_DK, (h + 1) * HG_DK) for h in range(HG_HEADS)]
    sts = [st_ref[h] for h in range(HG_HEADS)]
    o_inter = [_dot_nt(qh[64][:, sl], st.astype(BF16)) for sl, st in zip(heads, sts)]
    scores = []
    for sl in heads:
        sc = jnp.zeros((CHUNK, CHUNK), F32)
        for w in (32, 16, 8, 0):
            sc = jnp.where(masks[w], _dot_nt(qh[w][:, sl], kh[w][:, sl]), sc)
        scores.append(sc.astype(BF16))
    outs = [oi + jnp.dot(sc, v[:, sl], preferred_element_type=F32)
            for oi, sc, sl in zip(o_inter, scores, heads)]
    for h, (sl, st) in enumerate(zip(heads, sts)):
        st_ref[h] = dec_all[:, sl] * st + _dot_tn(v[:, sl], kh[64][:, sl])
    for o, sl in zip(outs, heads):
        ms = jnp.mean(o * o, axis=-1, keepdims=True)
        o = o * lax.rsqrt(ms + EPS) * gain[:, sl]
        o_ref[rows, sl] = (o * gate[:, sl]).astype(BF16)


def _proj_hgrn2(h, prefix, g, w_all, layer, lb, gain, first, bsz, chunks_per_step=11):
    d = h.shape[1]
    lp = h.shape[0] // bsz + (CHUNK if first else 0)
    rows = chunks_per_step * CHUNK
    const = lambda b, c: (0, 0)
    return pl.pallas_call(
        functools.partial(_proj_hgrn2_kernel, first=first),
        grid=(bsz, lp // rows),
        in_specs=[
            _row_tile_specs(first, bsz, lp, rows, d),
            pl.BlockSpec((CHUNK, d), const),
            pl.BlockSpec((1, d), const),
            pl.BlockSpec((None, d, D_IN), lambda b, c: (layer, 0, 0),
                         pipeline_mode=pl.Buffered(1)),
            pl.BlockSpec((1, D_HG), const),
            pl.BlockSpec((1, D_HG), const),
        ],
        out_specs=[pl.BlockSpec((None, rows, D_HG), lambda b, c: (b, c, 0)),
                   pl.BlockSpec((None, rows, D_S5), lambda b, c: (b, c, 0))],
        out_shape=[jax.ShapeDtypeStruct((bsz, lp, D_HG), BF16),
                   jax.ShapeDtypeStruct((bsz, lp, D_S5), F32)],
        scratch_shapes=[
            pltpu.VMEM((rows, d), BF16),
            pltpu.VMEM((rows, 4 * D_HG), F32),
            pltpu.VMEM((HG_HEADS, HG_DK, HG_DK), F32),
            pltpu.VMEM((9, D_HG // 128, 8, 128), F32),
        ],
        compiler_params=pltpu.CompilerParams(
            dimension_semantics=("arbitrary", "arbitrary"), vmem_limit_bytes=VMEM_LIMIT_BIG),
        name="proj_hgrn2",
    )(h, prefix, g.reshape(1, d), w_all, lb.reshape(1, D_HG), gain.reshape(1, D_HG))


def _gelu_tanh(x):
    c = 0.7978845608028654
    half = 0.5 * x
    return half + half * jnp.tanh(x * (c + (c * 0.044715) * (x * x)))


def _s5core_kernel(u_ref, t_ref, h_ref, g_ref, a_ref, y_ref, v_ref, xp_ref, *, bsz):
    nchunks = u_ref.shape[0] // bsz
    u = u_ref[...]
    v_ref[...] = jnp.dot(u, h_ref[0], preferred_element_type=F32)
    a_re = jnp.broadcast_to(a_ref[0, 0:1, :], (bsz, 128))
    a_im = jnp.broadcast_to(a_ref[0, 1:2, :], (bsz, 128))

    def step(n, carry):
        x_re, x_im = carry
        rows = pl.ds(pl.multiple_of(n * bsz, bsz), bsz)
        xp_ref[rows, 0:128] = x_re.astype(BF16)
        xp_ref[rows, 128:256] = x_im.astype(BF16)
        return (a_re * x_re - a_im * x_im + v_ref[rows, 0:128],
                a_re * x_im + a_im * x_re + v_ref[rows, 128:256])

    zero = jnp.zeros((bsz, 128), F32)
    lax.fori_loop(0, nchunks, step, (zero, zero), unroll=4)

    ys = jnp.dot(xp_ref[...], g_ref[0], preferred_element_type=F32)
    for k in range(2):
        cols = slice(k * S5_K, (k + 1) * S5_K)
        y_ref[:, cols] = ys[:, cols] + jnp.dot(u[:, cols], t_ref[0, k], preferred_element_type=F32)


def _s5core(uc, params, layer, bsz):
    tmat, hmat, gmat, a2 = params
    rows = uc.shape[0]
    return pl.pallas_call(
        functools.partial(_s5core_kernel, bsz=bsz),
        grid=(S5_PAIRS,),
        in_specs=[
            pl.BlockSpec((rows, 2 * S5_K), lambda p: (0, p)),
            pl.BlockSpec((None, 1, 2, S5_K, S5_K), lambda p: (layer, p, 0, 0, 0)),
            pl.BlockSpec((None, 1, 2 * S5_K, 4 * S5_STATE), lambda p: (layer, p, 0, 0)),
            pl.BlockSpec((None, 1, 4 * S5_STATE, 2 * S5_K), lambda p: (layer, p, 0, 0)),
            pl.BlockSpec((None, 1, 2, 2 * S5_STATE), lambda p: (layer, p, 0, 0)),
        ],
        out_specs=pl.BlockSpec((rows, 2 * S5_K), lambda p: (0, p)),
        out_shape=jax.ShapeDtypeStruct((rows, S5_GROUPS * S5_K), F32),
        scratch_shapes=[
            pltpu.VMEM((rows, 4 * S5_STATE), F32),
            pltpu.VMEM((rows, 4 * S5_STATE), BF16),
        ],
        compiler_params=pltpu.CompilerParams(
            dimension_semantics=("arbitrary",), vmem_limit_bytes=VMEM_LIMIT),
        name="s5core",
    )(uc, tmat, hmat, gmat, a2)


def _atom_transpose(groups):
    lane = lax.broadcasted_iota(jnp.int32, groups[0][0].shape, 1)
    groups = [list(xs) for xs in groups]
    for s in range(3):
        d = 1 << s
        keep = ((lane >> (4 + s)) & 1) == 0
        for xs in groups:
            for i in range(8):
                if i & d:
                    continue
                lo, hi = xs[i], xs[i + d]
                xs[i] = jnp.where(keep, lo, pltpu.roll(hi, S5_GROUP * d, 1))
                xs[i + d] = jnp.where(keep, pltpu.roll(lo, 128 - S5_GROUP * d, 1), hi)
    return groups


def _to_chunks_kernel(u_ref, o_ref, rows_ref):
    bsz, tt, width = u_ref.shape
    pitch = rows_ref.shape[1] // bsz
    for v in range(width // 128):
        for b in range(bsz):
            rows_ref[v, b * pitch:b * pitch + tt, :] = u_ref[b, :, v * 128:(v + 1) * 128]
    halves = [(v, n, k) for v in range(width // 128) for n in range(tt // S5_R) for k in range(2)]
    groups = _atom_transpose(
        [[rows_ref[v, pl.ds(n * S5_R + 8 * k + j, bsz, stride=pitch), :] for j in range(8)]
         for v, n, k in halves])
    for (v, n, k), ys in zip(halves, groups):
        for g, y in enumerate(ys):
            lane0 = (8 * v + g) * S5_K + k * 128
            o_ref[n * bsz:(n + 1) * bsz, lane0:lane0 + 128] = y.astype(BF16)


def _to_chunks(u3, tt=176, width=512):
    bsz, lp, _ = u3.shape
    return pl.pallas_call(
        _to_chunks_kernel,
        grid=(lp // tt, D_S5 // width),
        in_specs=[pl.BlockSpec((bsz, tt, width), lambda i, v: (0, i, v))],
        out_specs=pl.BlockSpec((tt, width // S5_GROUP * S5_K), lambda i, v: (i, v)),
        out_shape=jax.ShapeDtypeStruct((lp // S5_R * bsz, S5_GROUPS * S5_K), BF16),
        scratch_shapes=[pltpu.VMEM((width // 128, bsz * (tt + 8), 128), F32)],
        compiler_params=pltpu.CompilerParams(
            dimension_semantics=("arbitrary", "arbitrary"), vmem_limit_bytes=VMEM_LIMIT),
        name="s5_to_chunks",
    )(u3)


def _s5glu_kernel(yc_ref, u_ref, d_ref, w_ref, b_ref, gain_ref, o_ref, yn_ref):
    bsz, tt, _ = u_ref.shape
    pitch = yn_ref.shape[1] // bsz
    for n in range(tt // S5_R):
        halves = [(v, k) for v in range(D_S5 // 128) for k in range(2)]
        groups = _atom_transpose(
            [[yc_ref[n * bsz:(n + 1) * bsz,
                     (8 * v + g) * S5_K + k * 128:(8 * v + g) * S5_K + (k + 1) * 128]
              for g in range(8)] for v, k in halves])
        for (v, k), xs in zip(halves, groups):
            for j, x in enumerate(xs):
                yn_ref[v, pl.ds(n * S5_R + 8 * k + j, bsz, stride=pitch), :] = x
    y = jnp.concatenate(
        [jnp.concatenate([yn_ref[v, b * pitch:b * pitch + tt, :] for v in range(D_S5 // 128)],
                         axis=1) for b in range(bsz)], axis=0)
    y = _gelu_tanh(y + d_ref[...] * u_ref[...].reshape(bsz * tt, D_S5))
    g = jnp.dot(y.astype(BF16), w_ref[...], preferred_element_type=F32) + b_ref[...]
    o_ref[...] = _rms(y * _sigmoid(g), gain_ref[...]).reshape(bsz, tt, D_S5).astype(BF16)


def _s5glu(yc, u3, d_skip, wglu_all, layer, bglu, gain, tt=48):
    bsz, lp, _ = u3.shape
    return pl.pallas_call(
        _s5glu_kernel,
        grid=(lp // tt,),
        in_specs=[
            pl.BlockSpec((tt // S5_R * bsz, S5_GROUPS * S5_K), lambda i: (i, 0)),
            pl.BlockSpec((bsz, tt, D_S5), lambda i: (0, i, 0)),
            pl.BlockSpec((1, D_S5), lambda i: (0, 0)),
            pl.BlockSpec((None, D_S5, D_S5), lambda i: (layer, 0, 0)),
            pl.BlockSpec((1, D_S5), lambda i: (0, 0)),
            pl.BlockSpec((1, D_S5), lambda i: (0, 0)),
        ],
        out_specs=pl.BlockSpec((bsz, tt, D_S5), lambda i: (0, i, 0)),
        out_shape=jax.ShapeDtypeStruct((bsz, lp, D_S5), BF16),
        scratch_shapes=[pltpu.VMEM((D_S5 // 128, bsz * (tt + 8), 128), F32)],
        compiler_params=pltpu.CompilerParams(
            dimension_semantics=("arbitrary",), vmem_limit_bytes=VMEM_LIMIT),
        name="s5glu",
    )(yc, u3, d_skip.reshape(1, D_S5), wglu_all, bglu.reshape(1, D_S5), gain.reshape(1, D_S5))


def _s5_params(lam_re, lam_im, log_step, b_re, b_im, c_re, c_im):
    hp = lax.Precision.HIGHEST
    a_re = jnp.minimum(lam_re.astype(F32), -1e-4)
    a_im = lam_im.astype(F32)
    dt = jnp.exp(log_step.astype(F32))[:, None]
    mag = jnp.exp(a_re * dt)
    ab_re = mag * jnp.cos(a_im * dt)
    ab_im = mag * jnp.sin(a_im * dt)
    den = a_re * a_re + a_im * a_im
    x_re, x_im = ab_re - 1.0, ab_im
    z_re = (x_re * a_re + x_im * a_im) / den
    z_im = (x_im * a_re - x_re * a_im) / den
    br, bi = b_re.astype(F32), b_im.astype(F32)
    bb_re = z_re[..., None] * br - z_im[..., None] * bi
    bb_im = z_re[..., None] * bi + z_im[..., None] * br
    cr, ci = c_re.astype(F32), c_im.astype(F32)

    pw_re, pw_im = [jnp.ones_like(ab_re)], [jnp.zeros_like(ab_re)]
    for _ in range(S5_R):
        r, i = pw_re[-1], pw_im[-1]
        pw_re.append(r * ab_re - i * ab_im)
        pw_im.append(r * ab_im + i * ab_re)
    p_re, p_im = jnp.stack(pw_re), jnp.stack(pw_im)

    q_re, q_im = p_re[:S5_R, :, :, None], p_im[:S5_R, :, :, None]
    m_re = q_re * bb_re - q_im * bb_im
    m_im = q_re * bb_im + q_im * bb_re
    kern = (jnp.einsum('gop,lgph->ghlo', cr, m_re, precision=hp)
            - jnp.einsum('gop,lgph->ghlo', ci, m_im, precision=hp))
    kz = jnp.concatenate([jnp.zeros((S5_GROUPS, S5_GROUP, S5_K), F32),
                          kern.reshape(S5_GROUPS, S5_GROUP, S5_K)], axis=-1)
    tmat = jnp.stack([kz[:, :, (S5_R - s) * S5_GROUP:(S5_R - s) * S5_GROUP + S5_K]
                      for s in range(S5_R)], axis=1).reshape(S5_PAIRS, 2, S5_K, S5_K)

    bt_re, bt_im = bb_re.transpose(0, 2, 1)[:, None], bb_im.transpose(0, 2, 1)[:, None]
    rp_re = jnp.moveaxis(p_re[S5_R - 1::-1], 0, 1)[:, :, None, :]
    rp_im = jnp.moveaxis(p_im[S5_R - 1::-1], 0, 1)[:, :, None, :]
    h_re = (rp_re * bt_re - rp_im * bt_im).reshape(S5_PAIRS, 2, S5_K, S5_STATE)
    h_im = (rp_re * bt_im + rp_im * bt_re).reshape(S5_PAIRS, 2, S5_K, S5_STATE)
    zh = jnp.zeros_like(h_re[:, 0])
    hmat = jnp.concatenate([
        jnp.concatenate([h_re[:, 0], zh, h_im[:, 0], zh], axis=-1),
        jnp.concatenate([zh, h_re[:, 1], zh, h_im[:, 1]], axis=-1)], axis=1)

    ct_re, ct_im = cr.transpose(0, 2, 1)[:, :, None, :], ci.transpose(0, 2, 1)[:, :, None, :]
    e_re = jnp.moveaxis(p_re[1:], 0, 2)[..., None]
    e_im = jnp.moveaxis(p_im[1:], 0, 2)[..., None]
    g_re = (ct_re * e_re - ct_im * e_im).reshape(S5_PAIRS, 2, S5_STATE, S5_K)
    g_im = -(ct_re * e_im + ct_im * e_re).reshape(S5_PAIRS, 2, S5_STATE, S5_K)
    zg = jnp.zeros_like(g_re[:, 0])
    gmat = jnp.concatenate([
        jnp.concatenate([g_re[:, 0], zg], axis=-1), jnp.concatenate([zg, g_re[:, 1]], axis=-1),
        jnp.concatenate([g_im[:, 0], zg], axis=-1), jnp.concatenate([zg, g_im[:, 1]], axis=-1)],
        axis=1)

    a2 = jnp.stack([p_re[S5_R].reshape(S5_PAIRS, 2 * S5_STATE),
                    p_im[S5_R].reshape(S5_PAIRS, 2 * S5_STATE)], axis=1)
    return tmat.astype(BF16), hmat.astype(BF16), gmat.astype(BF16), a2


def _s5(u3, params, layer, d_skip, wglu_all, bglu, gain):
    yc = _s5core(_to_chunks(u3), params, layer, u3.shape[0])
    return _s5glu(yc, u3, d_skip, wglu_all, layer, bglu, gain)


def _outproj_kernel(h_ref, pre_ref, a_ref, b_ref, w_ref, o_ref, *, first):
    y = (jnp.dot(a_ref[...], w_ref[0], preferred_element_type=F32)
         + jnp.dot(b_ref[...], w_ref[1], preferred_element_type=F32))
    if first:
        npre = pre_ref.shape[0]

        @pl.when(pl.program_id(1) == 0)
        def _():
            o_ref[0:npre, :] = y[0:npre, :] + pre_ref[...]
            o_ref[npre:, :] = y[npre:, :] + h_ref[0:h_ref.shape[0] - npre, :]

        @pl.when(pl.program_id(1) != 0)
        def _():
            o_ref[...] = y + h_ref[...]
    else:
        o_ref[...] = y + h_ref[...]


def _outproj(h, prefix, a, b, w_all, layer, first, bsz, tm=704):
    k = a.shape[1]
    d = h.shape[1]
    lp = a.shape[0] // bsz
    tiles = lp // tm
    flat = lambda b, j: (b * tiles + j, 0)
    return pl.pallas_call(
        functools.partial(_outproj_kernel, first=first),
        grid=(bsz, tiles),
        in_specs=[
            _row_tile_specs(first, bsz, lp, tm, d),
            pl.BlockSpec((CHUNK, d), lambda b, j: (0, 0)),
            pl.BlockSpec((tm, k), flat),
            pl.BlockSpec((tm, k), flat),
            pl.BlockSpec((None, 2, k, d), lambda b, j: (layer, 0, 0, 0)),
        ],
        out_specs=pl.BlockSpec((tm, d), flat),
        out_shape=jax.ShapeDtypeStruct((bsz * lp, d), F32),
        compiler_params=pltpu.CompilerParams(
            dimension_semantics=("arbitrary", "arbitrary"), vmem_limit_bytes=VMEM_LIMIT),
        name="outproj",
    )(h, prefix, a, b, w_all)


FFN_HALO = 16


def _ffn_kernel(h_ref, halo_ref, g_ref, wg_ref, wu_ref, cw_ref, cb_ref, wd_ref, fg_ref, o_ref,
                hn_ref, *, final):
    f_axis = 2 if final else 1
    f = pl.program_id(f_axis)

    @pl.when(f == 0)
    def _():
        x = h_ref[...]
        hn_ref[FFN_HALO:, :] = _rms(x, g_ref[...]).astype(BF16)
        hn_ref[0:FFN_HALO, :] = _rms(halo_ref[...], g_ref[...]).astype(BF16)
        o_ref[...] = x

    a = jnp.dot(hn_ref[...], wg_ref[...], preferred_element_type=F32)
    up = jnp.dot(hn_ref[FFN_HALO:, :], wu_ref[...], preferred_element_type=F32)
    cw = cw_ref[...]
    conv = (cb_ref[...]
            + cw[0:1, :] * pltpu.roll(a, 2, 0)[FFN_HALO:, :]
            + cw[1:2, :] * pltpu.roll(a, 1, 0)[FFN_HALO:, :]
            + cw[2:3, :] * a[FFN_HALO:, :])
    hid = (_silu(conv) * up).astype(BF16)
    o_ref[...] += jnp.dot(hid, wd_ref[...], preferred_element_type=F32)

    if final:
        @pl.when(f == pl.num_programs(f_axis) - 1)
        def _():
            o_ref[...] = _rms(o_ref[...], fg_ref[...])


def _ffn(h, g, layer, wg_all, wu_all, cw_all, cb_all, wd_all, fg, final, bsz, tm=512, tf=512):
    t, d = h.shape
    ff = wg_all.shape[2]
    hb = tm // FFN_HALO
    lp = t // bsz
    if final:
        grid = (bsz, (lp - CHUNK) // tm, ff // tf)
        row0 = lambda b, j, back: pl.multiple_of(b * lp + CHUNK + j * tm - back, FFN_HALO)
        row_specs = [
            pl.BlockSpec((pl.Element(tm), pl.Element(d)), lambda b, j, f: (row0(b, j, 0), 0)),
            pl.BlockSpec((pl.Element(FFN_HALO), pl.Element(d)),
                         lambda b, j, f: (row0(b, j, FFN_HALO), 0)),
        ]
        out_spec = pl.BlockSpec((tm, d), lambda b, j, f: (b * ((lp - CHUNK) // tm) + j, 0))
        out_rows = bsz * (lp - CHUNK)
        wmap = lambda fn: (lambda b, j, f: fn(f))
    else:
        grid = (t // tm, ff // tf)
        row_specs = [
            pl.BlockSpec((tm, d), lambda i, f: (i, 0)),
            pl.BlockSpec((FFN_HALO, d), lambda i, f: (jnp.maximum(i * hb - 1, 0), 0)),
        ]
        out_spec = pl.BlockSpec((tm, d), lambda i, f: (i, 0))
        out_rows = t
        wmap = lambda fn: (lambda i, f: fn(f))
    return pl.pallas_call(
        functools.partial(_ffn_kernel, final=final),
        grid=grid,
        in_specs=row_specs + [
            pl.BlockSpec((1, d), wmap(lambda f: (0, 0))),
            pl.BlockSpec((None, d, tf), wmap(lambda f: (layer, 0, f))),
            pl.BlockSpec((None, d, tf), wmap(lambda f: (layer, 0, f))),
            pl.BlockSpec((None, 3, tf), wmap(lambda f: (layer, 0, f))),
            pl.BlockSpec((None, 1, tf), wmap(lambda f: (layer, 0, f))),
            pl.BlockSpec((None, tf, d), wmap(lambda f: (layer, f, 0))),
            pl.BlockSpec((1, d), wmap(lambda f: (0, 0))),
        ],
        out_specs=out_spec,
        out_shape=jax.ShapeDtypeStruct((out_rows, d), F32),
        scratch_shapes=[pltpu.VMEM((tm + FFN_HALO, d), BF16)],
        compiler_params=pltpu.CompilerParams(
            dimension_semantics=("arbitrary",) * len(grid), vmem_limit_bytes=VMEM_LIMIT),
        name="ffn_final" if final else "ffn",
    )(h, h, g.reshape(1, d), wg_all, wu_all, cw_all, cb_all.reshape(cb_all.shape[0], 1, ff),
      wd_all, fg.reshape(1, d))


def kernel(x, meta_tokens, lb_logits, norm_mix, w_in, hg_norm, s5_lambda_re, s5_lambda_im,
           s5_log_step, s5_b_re, s5_b_im, s5_c_re, s5_c_im, s5_d, w_glu, b_glu, s5_norm, w_out,
           norm_ffn, w_ffn_gate, w_ffn_up, ffn_conv_w, ffn_conv_b, w_ffn_down, final_norm):
    bsz, seq, d = x.shape
    lp = seq + CHUNK
    prefix = jnp.concatenate([jnp.zeros((SEQ_PAD, d), x.dtype), meta_tokens.astype(x.dtype)], axis=0)
    h = x.reshape(bsz * seq, d)

    sm = jax.nn.softmax(lb_logits.astype(F32), axis=0)
    lb_all = jnp.cumsum(sm, axis=0) - sm[0:1]

    w_in_b = w_in.astype(BF16)
    w_out_b = w_out.astype(BF16).reshape(DEPTH, 2, D_HG, d)
    w_glu_b = w_glu.astype(BF16)
    wg_b, wu_b, wd_b = (w.astype(BF16) for w in (w_ffn_gate, w_ffn_up, w_ffn_down))
    s5_params = jax.vmap(_s5_params)(s5_lambda_re, s5_lambda_im, s5_log_step,
                                     s5_b_re, s5_b_im, s5_c_re, s5_c_im)

    for l in range(DEPTH):
        o_hg, u3 = _proj_hgrn2(h, prefix, norm_mix[l], w_in_b, l, lb_all[l], hg_norm[l],
                               first=(l == 0), bsz=bsz)
        o_s5 = _s5(u3, s5_params, l, s5_d[l], w_glu_b, b_glu[l], s5_norm[l])
        h = _outproj(h, prefix, o_hg.reshape(bsz * lp, D_HG), o_s5.reshape(bsz * lp, D_S5),
                     w_out_b, l, first=(l == 0), bsz=bsz)
        h = _ffn(h, norm_ffn[l], l, wg_b, wu_b, ffn_conv_w, ffn_conv_b, wd_b, final_norm,
                 final=(l == DEPTH - 1), bsz=bsz, tm=512 if l == DEPTH - 1 else 768)
    return h.reshape(bsz, seq, d)
```

```python
import functools

import jax
import jax.numpy as jnp
from jax import lax
from jax.experimental import pallas as pl
from jax.experimental.pallas import tpu as pltpu

F32 = jnp.float32
BF16 = jnp.bfloat16
LANES = 128

DEPTH = 2
CHUNK = 64
N_META = 16
D_HG = 1024
HG_HEADS = 8
HG_DK = 128
D_S5 = 1024
S5_GROUP = 16
S5_GROUPS = 64
S5_STATE = 64
D_IN = 4 * D_HG + D_S5
EPS = 1e-6
F_FLOOR = 1e-6

SEQ_PAD = CHUNK - N_META
VMEM_LIMIT = 56 * 1024 * 1024
VMEM_LIMIT_BIG = 62 * 1024 * 1024

S5_R = 16
S5_K = S5_R * S5_GROUP
S5_PAIRS = S5_GROUPS // 2


def _rms(x, g):
    ms = jnp.mean(x * x, axis=-1, keepdims=True)
    return x * lax.rsqrt(ms + EPS) * g


def _sigmoid(x):
    return 0.5 + 0.5 * jnp.tanh(0.5 * x)


def _silu(x):
    half = 0.5 * x
    return half + half * jnp.tanh(half)


def _normed_tile(h_ref, pre_ref, g_ref, xn_ref, first):
    if first:
        npre = pre_ref.shape[0]

        @pl.when(pl.program_id(1) == 0)
        def _():
            xn_ref[0:npre, :] = _rms(pre_ref[...], g_ref[...]).astype(BF16)
            xn_ref[npre:, :] = _rms(h_ref[0:h_ref.shape[0] - npre, :], g_ref[...]).astype(BF16)

        @pl.when(pl.program_id(1) != 0)
        def _():
            xn_ref[...] = _rms(h_ref[...], g_ref[...]).astype(BF16)
    else:
        xn_ref[...] = _rms(h_ref[...], g_ref[...]).astype(BF16)


def _row_tile_specs(first, bsz, lp, tm, d):
    tiles = lp // tm
    if first:
        def rows(b, j, *_):
            return (pl.multiple_of(b * (lp - CHUNK) + jnp.maximum(j * tm - CHUNK, 0), 8), 0)
        return pl.BlockSpec((pl.Element(tm), pl.Element(d)), rows)
    return pl.BlockSpec((tm, d), lambda b, j, *_: (b * tiles + j, 0))


def _pick_rows(x, m, first):
    j = lax.broadcasted_iota(jnp.int32, x.shape, 0)
    out = None
    for blk in range(8 // m - 1, -1, -1):
        r = blk * m + (0 if first else m - 1)
        row = jnp.broadcast_to(x[r:r + 1, :], x.shape)
        out = row if out is None else jnp.where(j < (blk + 1) * m, row, out)
    return out


def _expand_rows(ref, i, x):
    nv = x.shape[1] // LANES
    for v in range(nv):
        ref[i, v] = x[:, v * LANES:(v + 1) * LANES]
    return jnp.concatenate(
        [jnp.concatenate([ref[i, v, pl.ds(j, 8, stride=0), :] for v in range(nv)], axis=1)
         for j in range(8)], axis=0)


def _dot_nt(a, b):
    return lax.dot_general(a, b, (((1,), (1,)), ((), ())), preferred_element_type=F32)


def _dot_tn(a, b):
    return lax.dot_general(a, b, (((0,), (0,)), ((), ())), preferred_element_type=F32)


def _proj_hgrn2_kernel(h_ref, pre_ref, g_ref, w_ref, lb_ref, gain_ref, o_ref, u_ref,
                       xn_ref, p_ref, st_ref, bc_ref, *, first):
    @pl.when(pl.program_id(1) == 0)
    def _():
        st_ref[...] = jnp.zeros_like(st_ref)

    _normed_tile(h_ref, pre_ref, g_ref, xn_ref, first)
    xn = xn_ref[...]
    for k in range(4 * D_HG // D_S5):
        cols = slice(k * D_S5, (k + 1) * D_S5)
        p_ref[:, cols] = jnp.dot(xn, w_ref[:, cols], preferred_element_type=F32)
    u_ref[...] = jnp.dot(xn, w_ref[:, 4 * D_HG:], preferred_element_type=F32)

    nsel = CHUNK + 16
    row = lax.broadcasted_iota(jnp.int32, (nsel, 3 * CHUNK), 0)
    col = lax.broadcasted_iota(jnp.int32, (nsel, 3 * CHUNK), 1) & (CHUNK - 1)
    tgt = jnp.where(row < CHUNK, row,
                    jnp.where(row < CHUNK + 8, (row - CHUNK) * 8 + 7, (row - CHUNK - 8) * 8 + 3))
    tri3 = (tgt >= col).astype(BF16)
    ti = lax.broadcasted_iota(jnp.int32, (CHUNK, CHUNK), 0)
    si = lax.broadcasted_iota(jnp.int32, (CHUNK, CHUNK), 1)
    masks = {}
    for w, sh in ((32, 5), (16, 4), (8, 3)):
        masks[w] = ((ti >> sh) == (si >> sh) + 1) & ((ti >> (sh + 1)) == (si >> (sh + 1)))
    masks[0] = ((ti >> 3) == (si >> 3)) & (si <= ti)

    def chunk(c, carry):
        _hgrn2_chunk(p_ref, lb_ref, gain_ref, o_ref, st_ref, bc_ref, tri3, masks,
                     pl.ds(pl.multiple_of(c * CHUNK, CHUNK), CHUNK))
        return carry

    lax.fori_loop(0, p_ref.shape[0] // CHUNK, chunk, 0)


def _hgrn2_chunk(p_ref, lb_ref, gain_ref, o_ref, st_ref, bc_ref, tri3, masks, rows):
    q = p_ref[rows, 0:D_HG]
    z = p_ref[rows, D_HG:2 * D_HG]
    v = p_ref[rows, 2 * D_HG:3 * D_HG].astype(BF16)
    lb = lb_ref[...]

    th = 0.5 * jnp.tanh(0.5 * z)
    gl = jnp.log2(jnp.maximum(lb + (1.0 - lb) * (0.5 + th), F_FLOOR))
    kk = (1.0 - lb) * (0.5 - th)
    qf = _silu(q)

    hi = gl.astype(BF16)
    r1 = gl - hi.astype(F32)
    mid = r1.astype(BF16)
    lo = (r1 - mid.astype(F32)).astype(BF16)
    cum = jnp.dot(tri3, jnp.concatenate([hi, mid, lo], axis=0), preferred_element_type=F32)
    b = cum[0:CHUNK]
    l8 = cum[CHUNK:CHUNK + 8]
    m8 = cum[CHUNK + 8:CHUNK + 16]
    j8 = lax.broadcasted_iota(jnp.int32, l8.shape, 0)
    s8 = jnp.where(j8 == 0, 0.0, pltpu.roll(l8, 1, 0))

    qe = qf * jnp.exp2(b - _expand_rows(bc_ref, 0, s8))
    kf = kk * jnp.exp2(_expand_rows(bc_ref, 1, l8) - b)
    qh = {8: qe.astype(BF16)}
    kh = {8: kf.astype(BF16)}
    for lvl, (w, m) in enumerate(((16, 2), (32, 4), (64, 8))):
        cq = jnp.exp2(s8 - _pick_rows(s8, m, True))
        ck = jnp.exp2(_pick_rows(l8, m, False) - l8)
        qh[w] = (qe * _expand_rows(bc_ref, 3 + 2 * lvl, cq)).astype(BF16)
        kh[w] = (kf * _expand_rows(bc_ref, 4 + 2 * lvl, ck)).astype(BF16)
    b_mid = _expand_rows(bc_ref, 2, m8)
    qh[0] = (qf * jnp.exp2(b - b_mid)).astype(BF16)
    kh[0] = (kk * jnp.exp2(b_mid - b)).astype(BF16)
    dec_all = jnp.exp2(l8[7:8, :])

    gate = p_ref[rows, 3 * D_HG:4 * D_HG]
    gate = _silu(gate)
    gain = gain_ref[...]

    heads = [slice(h * HG_DK, (h + 1) * HG_DK) for h in range(HG_HEADS)]
    sts = [st_ref[h] for h in range(HG_HEADS)]
    o_inter = [_dot_nt(qh[64][:, sl], st.astype(BF16)) for sl, st in zip(heads, sts)]
    scores = []
    for sl in heads:
        sc = jnp.zeros((CHUNK, CHUNK), F32)
        for w in (32, 16, 8, 0):
            sc = jnp.where(masks[w], _dot_nt(qh[w][:, sl], kh[w][:, sl]), sc)
        scores.append(sc.astype(BF16))
    outs = [oi + jnp.dot(sc, v[:, sl], preferred_element_type=F32)
            for oi, sc, sl in zip(o_inter, scores, heads)]
    for h, (sl, st) in enumerate(zip(heads, sts)):
        st_ref[h] = dec_all[:, sl] * st + _dot_tn(v[:, sl], kh[64][:, sl])
    for o, sl in zip(outs, heads):
        ms = jnp.mean(o * o, axis=-1, keepdims=True)
        o = o * lax.rsqrt(ms + EPS) * gain[:, sl]
        o_ref[rows, sl] = (o * gate[:, sl]).astype(BF16)


def _proj_hgrn2(h, prefix, g, w_all, layer, lb, gain, first, bsz, chunks_per_step=11):
    d = h.shape[1]
    lp = h.shape[0] // bsz + (CHUNK if first else 0)
    rows = chunks_per_step * CHUNK
    const = lambda b, c: (0, 0)
    return pl.pallas_call(
        functools.partial(_proj_hgrn2_kernel, first=first),
        grid=(bsz, lp // rows),
        in_specs=[
            _row_tile_specs(first, bsz, lp, rows, d),
            pl.BlockSpec((CHUNK, d), const),
            pl.BlockSpec((1, d), const),
            pl.BlockSpec((None, d, D_IN), lambda b, c: (layer, 0, 0),
                         pipeline_mode=pl.Buffered(1)),
            pl.BlockSpec((1, D_HG), const),
            pl.BlockSpec((1, D_HG), const),
        ],
        out_specs=[pl.BlockSpec((None, rows, D_HG), lambda b, c: (b, c, 0)),
                   pl.BlockSpec((None, rows, D_S5), lambda b, c: (b, c, 0))],
        out_shape=[jax.ShapeDtypeStruct((bsz, lp, D_HG), BF16),
                   jax.ShapeDtypeStruct((bsz, lp, D_S5), F32)],
        scratch_shapes=[
            pltpu.VMEM((rows, d), BF16),
            pltpu.VMEM((rows, 4 * D_HG), F32),
            pltpu.VMEM((HG_HEADS, HG_DK, HG_DK), F32),
            pltpu.VMEM((9, D_HG // LANES, 8, LANES), F32),
        ],
        compiler_params=pltpu.CompilerParams(
            dimension_semantics=("arbitrary", "arbitrary"), vmem_limit_bytes=VMEM_LIMIT_BIG),
        name="proj_hgrn2",
    )(h, prefix, g.reshape(1, d), w_all, lb.reshape(1, D_HG), gain.reshape(1, D_HG))


def _gelu_tanh(x):
    c = 0.7978845608028654
    half = 0.5 * x
    return half + half * jnp.tanh(x * (c + (c * 0.044715) * (x * x)))


def _s5core_kernel(u_ref, t_ref, h_ref, g_ref, a_ref, y_ref, v_ref, xp_ref, *, bsz):
    nchunks = u_ref.shape[0] // bsz
    u = u_ref[...]
    v_ref[...] = jnp.dot(u, h_ref[0], preferred_element_type=F32)
    a_re = jnp.broadcast_to(a_ref[0, 0:1, :], (bsz, LANES))
    a_im = jnp.broadcast_to(a_ref[0, 1:2, :], (bsz, LANES))

    def step(n, carry):
        x_re, x_im = carry
        rows = pl.ds(pl.multiple_of(n * bsz, bsz), bsz)
        xp_ref[rows, 0:LANES] = x_re.astype(BF16)
        xp_ref[rows, LANES:2 * LANES] = x_im.astype(BF16)
        return (a_re * x_re - a_im * x_im + v_ref[rows, 0:LANES],
                a_re * x_im + a_im * x_re + v_ref[rows, LANES:2 * LANES])

    zero = jnp.zeros((bsz, LANES), F32)
    lax.fori_loop(0, nchunks, step, (zero, zero), unroll=4)

    ys = jnp.dot(xp_ref[...], g_ref[0], preferred_element_type=F32)
    for k in range(2):
        cols = slice(k * S5_K, (k + 1) * S5_K)
        y_ref[:, cols] = ys[:, cols] + jnp.dot(u[:, cols], t_ref[0, k], preferred_element_type=F32)


def _s5core(uc, params, layer, bsz):
    tmat, hmat, gmat, a2 = params
    rows = uc.shape[0]
    return pl.pallas_call(
        functools.partial(_s5core_kernel, bsz=bsz),
        grid=(S5_PAIRS,),
        in_specs=[
            pl.BlockSpec((rows, 2 * S5_K), lambda p: (0, p)),
            pl.BlockSpec((None, 1, 2, S5_K, S5_K), lambda p: (layer, p, 0, 0, 0)),
            pl.BlockSpec((None, 1, 2 * S5_K, 4 * S5_STATE), lambda p: (layer, p, 0, 0)),
            pl.BlockSpec((None, 1, 4 * S5_STATE, 2 * S5_K), lambda p: (layer, p, 0, 0)),
            pl.BlockSpec((None, 1, 2, 2 * S5_STATE), lambda p: (layer, p, 0, 0)),
        ],
        out_specs=pl.BlockSpec((rows, 2 * S5_K), lambda p: (0, p)),
        out_shape=jax.ShapeDtypeStruct((rows, S5_GROUPS * S5_K), F32),
        scratch_shapes=[
            pltpu.VMEM((rows, 4 * S5_STATE), F32),
            pltpu.VMEM((rows, 4 * S5_STATE), BF16),
        ],
        compiler_params=pltpu.CompilerParams(
            dimension_semantics=("arbitrary",), vmem_limit_bytes=VMEM_LIMIT),
        name="s5core",
    )(uc, tmat, hmat, gmat, a2)


def _atom_transpose(groups):
    lane = lax.broadcasted_iota(jnp.int32, groups[0][0].shape, 1)
    groups = [list(xs) for xs in groups]
    for s in range(3):
        d = 1 << s
        keep = ((lane >> (4 + s)) & 1) == 0
        for xs in groups:
            for i in range(8):
                if i & d:
                    continue
                lo, hi = xs[i], xs[i + d]
                xs[i] = jnp.where(keep, lo, pltpu.roll(hi, S5_GROUP * d, 1))
                xs[i + d] = jnp.where(keep, pltpu.roll(lo, LANES - S5_GROUP * d, 1), hi)
    return groups


def _to_chunks_kernel(u_ref, o_ref, rows_ref):
    bsz, tt, width = u_ref.shape
    pitch = rows_ref.shape[1] // bsz
    for v in range(width // LANES):
        for b in range(bsz):
            rows_ref[v, b * pitch:b * pitch + tt, :] = u_ref[b, :, v * LANES:(v + 1) * LANES]
    halves = [(v, n, k) for v in range(width // LANES) for n in range(tt // S5_R) for k in range(2)]
    groups = _atom_transpose(
        [[rows_ref[v, pl.ds(n * S5_R + 8 * k + j, bsz, stride=pitch), :] for j in range(8)]
         for v, n, k in halves])
    for (v, n, k), ys in zip(halves, groups):
        for g, y in enumerate(ys):
            lane0 = (8 * v + g) * S5_K + k * LANES
            o_ref[n * bsz:(n + 1) * bsz, lane0:lane0 + LANES] = y.astype(BF16)


def _to_chunks(u3, tt=176, width=512):
    bsz, lp, _ = u3.shape
    return pl.pallas_call(
        _to_chunks_kernel,
        grid=(lp // tt, D_S5 // width),
        in_specs=[pl.BlockSpec((bsz, tt, width), lambda i, v: (0, i, v))],
        out_specs=pl.BlockSpec((tt, width // S5_GROUP * S5_K), lambda i, v: (i, v)),
        out_shape=jax.ShapeDtypeStruct((lp // S5_R * bsz, S5_GROUPS * S5_K), BF16),
        scratch_shapes=[pltpu.VMEM((width // LANES, bsz * (tt + 8), LANES), F32)],
        compiler_params=pltpu.CompilerParams(
            dimension_semantics=("arbitrary", "arbitrary"), vmem_limit_bytes=VMEM_LIMIT),
        name="s5_to_chunks",
    )(u3)


def _s5glu_kernel(yc_ref, u_ref, d_ref, w_ref, b_ref, gain_ref, o_ref, yn_ref):
    bsz, tt, _ = u_ref.shape
    pitch = yn_ref.shape[1] // bsz
    for n in range(tt // S5_R):
        halves = [(v, k) for v in range(D_S5 // LANES) for k in range(2)]
        groups = _atom_transpose(
            [[yc_ref[n * bsz:(n + 1) * bsz,
                     (8 * v + g) * S5_K + k * LANES:(8 * v + g) * S5_K + (k + 1) * LANES]
              for g in range(8)] for v, k in halves])
        for (v, k), xs in zip(halves, groups):
            for j, x in enumerate(xs):
                yn_ref[v, pl.ds(n * S5_R + 8 * k + j, bsz, stride=pitch), :] = x
    y = jnp.concatenate(
        [jnp.concatenate([yn_ref[v, b * pitch:b * pitch + tt, :] for v in range(D_S5 // LANES)],
                         axis=1) for b in range(bsz)], axis=0)
    y = _gelu_tanh(y + d_ref[...] * u_ref[...].reshape(bsz * tt, D_S5))
    g = jnp.dot(y.astype(BF16), w_ref[...], preferred_element_type=F32) + b_ref[...]
    o_ref[...] = _rms(y * _sigmoid(g), gain_ref[...]).reshape(bsz, tt, D_S5).astype(BF16)


def _s5glu(yc, u3, d_skip, wglu_all, layer, bglu, gain, tt=48):
    bsz, lp, _ = u3.shape
    return pl.pallas_call(
        _s5glu_kernel,
        grid=(lp // tt,),
        in_specs=[
            pl.BlockSpec((tt // S5_R * bsz, S5_GROUPS * S5_K), lambda i: (i, 0)),
            pl.BlockSpec((bsz, tt, D_S5), lambda i: (0, i, 0)),
            pl.BlockSpec((1, D_S5), lambda i: (0, 0)),
            pl.BlockSpec((None, D_S5, D_S5), lambda i: (layer, 0, 0)),
            pl.BlockSpec((1, D_S5), lambda i: (0, 0)),
            pl.BlockSpec((1, D_S5), lambda i: (0, 0)),
        ],
        out_specs=pl.BlockSpec((bsz, tt, D_S5), lambda i: (0, i, 0)),
        out_shape=jax.ShapeDtypeStruct((bsz, lp, D_S5), BF16),
        scratch_shapes=[pltpu.VMEM((D_S5 // LANES, bsz * (tt + 8), LANES), F32)],
        compiler_params=pltpu.CompilerParams(
            dimension_semantics=("arbitrary",), vmem_limit_bytes=VMEM_LIMIT),
        name="s5glu",
    )(yc, u3, d_skip.reshape(1, D_S5), wglu_all, bglu.reshape(1, D_S5), gain.reshape(1, D_S5))


def _s5_params(lam_re, lam_im, log_step, b_re, b_im, c_re, c_im):
    hp = lax.Precision.HIGHEST
    a_re = jnp.minimum(lam_re.astype(F32), -1e-4)
    a_im = lam_im.astype(F32)
    dt = jnp.exp(log_step.astype(F32))[:, None]
    mag = jnp.exp(a_re * dt)
    ab_re = mag * jnp.cos(a_im * dt)
    ab_im = mag * jnp.sin(a_im * dt)
    den = a_re * a_re + a_im * a_im
    x_re, x_im = ab_re - 1.0, ab_im
    z_re = (x_re * a_re + x_im * a_im) / den
    z_im = (x_im * a_re - x_re * a_im) / den
    br, bi = b_re.astype(F32), b_im.astype(F32)
    bb_re = z_re[..., None] * br - z_im[..., None] * bi
    bb_im = z_re[..., None] * bi + z_im[..., None] * br
    cr, ci = c_re.astype(F32), c_im.astype(F32)

    pw_re, pw_im = [jnp.ones_like(ab_re)], [jnp.zeros_like(ab_re)]
    for _ in range(S5_R):
        r, i = pw_re[-1], pw_im[-1]
        pw_re.append(r * ab_re - i * ab_im)
        pw_im.append(r * ab_im + i * ab_re)
    p_re, p_im = jnp.stack(pw_re), jnp.stack(pw_im)

    q_re, q_im = p_re[:S5_R, :, :, None], p_im[:S5_R, :, :, None]
    m_re = q_re * bb_re - q_im * bb_im
    m_im = q_re * bb_im + q_im * bb_re
    kern = (jnp.einsum('gop,lgph->ghlo', cr, m_re, precision=hp)
            - jnp.einsum('gop,lgph->ghlo', ci, m_im, precision=hp))
    kz = jnp.concatenate([jnp.zeros((S5_GROUPS, S5_GROUP, S5_K), F32),
                          kern.reshape(S5_GROUPS, S5_GROUP, S5_K)], axis=-1)
    tmat = jnp.stack([kz[:, :, (S5_R - s) * S5_GROUP:(S5_R - s) * S5_GROUP + S5_K]
                      for s in range(S5_R)], axis=1).reshape(S5_PAIRS, 2, S5_K, S5_K)

    bt_re, bt_im = bb_re.transpose(0, 2, 1)[:, None], bb_im.transpose(0, 2, 1)[:, None]
    rp_re = jnp.moveaxis(p_re[S5_R - 1::-1], 0, 1)[:, :, None, :]
    rp_im = jnp.moveaxis(p_im[S5_R - 1::-1], 0, 1)[:, :, None, :]
    h_re = (rp_re * bt_re - rp_im * bt_im).reshape(S5_PAIRS, 2, S5_K, S5_STATE)
    h_im = (rp_re * bt_im + rp_im * bt_re).reshape(S5_PAIRS, 2, S5_K, S5_STATE)
    zh = jnp.zeros_like(h_re[:, 0])
    hmat = jnp.concatenate([
        jnp.concatenate([h_re[:, 0], zh, h_im[:, 0], zh], axis=-1),
        jnp.concatenate([zh, h_re[:, 1], zh, h_im[:, 1]], axis=-1)], axis=1)

    ct_re, ct_im = cr.transpose(0, 2, 1)[:, :, None, :], ci.transpose(0, 2, 1)[:, :, None, :]
    e_re = jnp.moveaxis(p_re[1:], 0, 2)[..., None]
    e_im = jnp.moveaxis(p_im[1:], 0, 2)[..., None]
    g_re = (ct_re * e_re - ct_im * e_im).reshape(S5_PAIRS, 2, S5_STATE, S5_K)
    g_im = -(ct_re * e_im + ct_im * e_re).reshape(S5_PAIRS, 2, S5_STATE, S5_K)
    zg = jnp.zeros_like(g_re[:, 0])
    gmat = jnp.concatenate([
        jnp.concatenate([g_re[:, 0], zg], axis=-1), jnp.concatenate([zg, g_re[:, 1]], axis=-1),
        jnp.concatenate([g_im[:, 0], zg], axis=-1), jnp.concatenate([zg, g_im[:, 1]], axis=-1)],
        axis=1)

    a2 = jnp.stack([p_re[S5_R].reshape(S5_PAIRS, 2 * S5_STATE),
                    p_im[S5_R].reshape(S5_PAIRS, 2 * S5_STATE)], axis=1)
    return tmat.astype(BF16), hmat.astype(BF16), gmat.astype(BF16), a2


def _s5(u3, params, layer, d_skip, wglu_all, bglu, gain):
    yc = _s5core(_to_chunks(u3), params, layer, u3.shape[0])
    return _s5glu(yc, u3, d_skip, wglu_all, layer, bglu, gain)


def _outproj_kernel(h_ref, pre_ref, a_ref, b_ref, w_ref, o_ref, *, first):
    y = (jnp.dot(a_ref[...], w_ref[0], preferred_element_type=F32)
         + jnp.dot(b_ref[...], w_ref[1], preferred_element_type=F32))
    if first:
        npre = pre_ref.shape[0]

        @pl.when(pl.program_id(1) == 0)
        def _():
            o_ref[0:npre, :] = y[0:npre, :] + pre_ref[...]
            o_ref[npre:, :] = y[npre:, :] + h_ref[0:h_ref.shape[0] - npre, :]

        @pl.when(pl.program_id(1) != 0)
        def _():
            o_ref[...] = y + h_ref[...]
    else:
        o_ref[...] = y + h_ref[...]


def _outproj(h, prefix, a, b, w_all, layer, first, bsz, tm=704):
    k = a.shape[1]
    d = h.shape[1]
    lp = a.shape[0] // bsz
    tiles = lp // tm
    flat = lambda b, j: (b * tiles + j, 0)
    return pl.pallas_call(
        functools.partial(_outproj_kernel, first=first),
        grid=(bsz, tiles),
        in_specs=[
            _row_tile_specs(first, bsz, lp, tm, d),
            pl.BlockSpec((CHUNK, d), lambda b, j: (0, 0)),
            pl.BlockSpec((tm, k), flat),
            pl.BlockSpec((tm, k), flat),
            pl.BlockSpec((None, 2, k, d), lambda b, j: (layer, 0, 0, 0)),
        ],
        out_specs=pl.BlockSpec((tm, d), flat),
        out_shape=jax.ShapeDtypeStruct((bsz * lp, d), F32),
        compiler_params=pltpu.CompilerParams(
            dimension_semantics=("arbitrary", "arbitrary"), vmem_limit_bytes=VMEM_LIMIT),
        name="outproj",
    )(h, prefix, a, b, w_all)


FFN_HALO = 16


def _ffn_kernel(h_ref, halo_ref, g_ref, wg_ref, wu_ref, cw_ref, cb_ref, wd_ref, fg_ref, o_ref,
                hn_ref, *, final):
    f_axis = 2 if final else 1
    f = pl.program_id(f_axis)

    @pl.when(f == 0)
    def _():
        x = h_ref[...]
        hn_ref[FFN_HALO:, :] = _rms(x, g_ref[...]).astype(BF16)
        hn_ref[0:FFN_HALO, :] = _rms(halo_ref[...], g_ref[...]).astype(BF16)
        o_ref[...] = x

    a = jnp.dot(hn_ref[...], wg_ref[...], preferred_element_type=F32)
    up = jnp.dot(hn_ref[FFN_HALO:, :], wu_ref[...], preferred_element_type=F32)
    cw = cw_ref[...]
    conv = (cb_ref[...]
            + cw[0:1, :] * pltpu.roll(a, 2, 0)[FFN_HALO:, :]
            + cw[1:2, :] * pltpu.roll(a, 1, 0)[FFN_HALO:, :]
            + cw[2:3, :] * a[FFN_HALO:, :])
    hid = (_silu(conv) * up).astype(BF16)
    o_ref[...] += jnp.dot(hid, wd_ref[...], preferred_element_type=F32)

    if final:
        @pl.when(f == pl.num_programs(f_axis) - 1)
        def _():
            o_ref[...] = _rms(o_ref[...], fg_ref[...])


def _ffn(h, g, layer, wg_all, wu_all, cw_all, cb_all, wd_all, fg, final, bsz, tm=512, tf=512):
    t, d = h.shape
    ff = wg_all.shape[2]
    hb = tm // FFN_HALO
    lp = t // bsz
    if final:
        grid = (bsz, (lp - CHUNK) // tm, ff // tf)
        row0 = lambda b, j, back: pl.multiple_of(b * lp + CHUNK + j * tm - back, FFN_HALO)
        row_specs = [
            pl.BlockSpec((pl.Element(tm), pl.Element(d)), lambda b, j, f: (row0(b, j, 0), 0)),
            pl.BlockSpec((pl.Element(FFN_HALO), pl.Element(d)),
                         lambda b, j, f: (row0(b, j, FFN_HALO), 0)),
        ]
        out_spec = pl.BlockSpec((tm, d), lambda b, j, f: (b * ((lp - CHUNK) // tm) + j, 0))
        out_rows = bsz * (lp - CHUNK)
        wmap = lambda fn: (lambda b, j, f: fn(f))
    else:
        grid = (t // tm, ff // tf)
        row_specs = [
            pl.BlockSpec((tm, d), lambda i, f: (i, 0)),
            pl.BlockSpec((FFN_HALO, d), lambda i, f: (jnp.maximum(i * hb - 1, 0), 0)),
        ]
        out_spec = pl.BlockSpec((tm, d), lambda i, f: (i, 0))
        out_rows = t
        wmap = lambda fn: (lambda i, f: fn(f))
    return pl.pallas_call(
        functools.partial(_ffn_kernel, final=final),
        grid=grid,
        in_specs=row_specs + [
            pl.BlockSpec((1, d), wmap(lambda f: (0, 0))),
            pl.BlockSpec((None, d, tf), wmap(lambda f: (layer, 0, f))),
            pl.BlockSpec((None, d, tf), wmap(lambda f: (layer, 0, f))),
            pl.BlockSpec((None, 3, tf), wmap(lambda f: (layer, 0, f))),
            pl.BlockSpec((None, 1, tf), wmap(lambda f: (layer, 0, f))),
            pl.BlockSpec((None, tf, d), wmap(lambda f: (layer, f, 0))),
            pl.BlockSpec((1, d), wmap(lambda f: (0, 0))),
        ],
        out_specs=out_spec,
        out_shape=jax.ShapeDtypeStruct((out_rows, d), F32),
        scratch_shapes=[pltpu.VMEM((tm + FFN_HALO, d), BF16)],
        compiler_params=pltpu.CompilerParams(
            dimension_semantics=("arbitrary",) * len(grid), vmem_limit_bytes=VMEM_LIMIT),
        name="ffn_final" if final else "ffn",
    )(h, h, g.reshape(1, d), wg_all, wu_all, cw_all, cb_all.reshape(cb_all.shape[0], 1, ff),
      wd_all, fg.reshape(1, d))


def kernel(x, meta_tokens, lb_logits, norm_mix, w_in, hg_norm, s5_lambda_re, s5_lambda_im,
           s5_log_step, s5_b_re, s5_b_im, s5_c_re, s5_c_im, s5_d, w_glu, b_glu, s5_norm, w_out,
           norm_ffn, w_ffn_gate, w_ffn_up, ffn_conv_w, ffn_conv_b, w_ffn_down, final_norm):
    bsz, seq, d = x.shape
    lp = seq + CHUNK
    prefix = jnp.concatenate([jnp.zeros((SEQ_PAD, d), x.dtype), meta_tokens.astype(x.dtype)], axis=0)
    h = x.reshape(bsz * seq, d)

    sm = jax.nn.softmax(lb_logits.astype(F32), axis=0)
    lb_all = jnp.cumsum(sm, axis=0) - sm[0:1]

    w_in_b = w_in.astype(BF16)
    w_out_b = w_out.astype(BF16).reshape(DEPTH, 2, D_HG, d)
    w_glu_b = w_glu.astype(BF16)
    wg_b, wu_b, wd_b = (w.astype(BF16) for w in (w_ffn_gate, w_ffn_up, w_ffn_down))
    s5_params = jax.vmap(_s5_params)(s5_lambda_re, s5_lambda_im, s5_log_step,
                                     s5_b_re, s5_b_im, s5_c_re, s5_c_im)

    for l in range(DEPTH):
        o_hg, u3 = _proj_hgrn2(h, prefix, norm_mix[l], w_in_b, l, lb_all[l], hg_norm[l],
                               first=(l == 0), bsz=bsz)
        o_s5 = _s5(u3, s5_params, l, s5_d[l], w_glu_b, b_glu[l], s5_norm[l])
        h = _outproj(h, prefix, o_hg.reshape(bsz * lp, D_HG), o_s5.reshape(bsz * lp, D_S5),
                     w_out_b, l, first=(l == 0), bsz=bsz)
        h = _ffn(h, norm_ffn[l], l, wg_b, wu_b, ffn_conv_w, ffn_conv_b, wd_b, final_norm,
                 final=(l == DEPTH - 1), bsz=bsz, tm=512 if l == DEPTH - 1 else 768)
    return h.reshape(bsz, seq, d)
```

```python
import functools

import jax
import jax.numpy as jnp
from jax import lax
from jax.experimental import pallas as pl
from jax.experimental.pallas import tpu as pltpu

F32 = jnp.float32
BF16 = jnp.bfloat16
LANES = 128

DEPTH = 2
CHUNK = 64
N_META = 16
D_HG = 1024
HG_HEADS = 8
HG_DK = 128
D_S5 = 1024
S5_GROUP = 16
S5_GROUPS = 64
S5_STATE = 64
D_IN = 4 * D_HG + D_S5
EPS = 1e-6
F_FLOOR = 1e-6

SEQ_PAD = CHUNK - N_META
VMEM_LIMIT = 56 * 1024 * 1024
VMEM_LIMIT_BIG = 62 * 1024 * 1024

S5_R = 16
S5_K = S5_R * S5_GROUP
S5_PAIRS = S5_GROUPS // 2


def _rms(x, g):
    ms = jnp.mean(x * x, axis=-1, keepdims=True)
    return x * lax.rsqrt(ms + EPS) * g


def _sigmoid(x):
    return 0.5 + 0.5 * jnp.tanh(0.5 * x)


def _silu(x):
    half = 0.5 * x
    return half + half * jnp.tanh(half)


def _normed_tile(h_ref, pre_ref, g_ref, xn_ref, first):
    if first:
        npre = pre_ref.shape[0]

        @pl.when(pl.program_id(1) == 0)
        def _():
            xn_ref[0:npre, :] = _rms(pre_ref[...], g_ref[...]).astype(BF16)
            xn_ref[npre:, :] = _rms(h_ref[0:h_ref.shape[0] - npre, :], g_ref[...]).astype(BF16)

        @pl.when(pl.program_id(1) != 0)
        def _():
            xn_ref[...] = _rms(h_ref[...], g_ref[...]).astype(BF16)
    else:
        xn_ref[...] = _rms(h_ref[...], g_ref[...]).astype(BF16)


def _row_tile_specs(first, bsz, lp, tm, d):
    tiles = lp // tm
    if first:
        def rows(b, j, *_):
            return (pl.multiple_of(b * (lp - CHUNK) + jnp.maximum(j * tm - CHUNK, 0), 8), 0)
        return pl.BlockSpec((pl.Element(tm), pl.Element(d)), rows)
    return pl.BlockSpec((tm, d), lambda b, j, *_: (b * tiles + j, 0))


def _pick_rows(x, m, first):
    j = lax.broadcasted_iota(jnp.int32, x.shape, 0)
    out = None
    for blk in range(8 // m - 1, -1, -1):
        r = blk * m + (0 if first else m - 1)
        row = jnp.broadcast_to(x[r:r + 1, :], x.shape)
        out = row if out is None else jnp.where(j < (blk + 1) * m, row, out)
    return out


def _expand_rows(ref, i, x):
    nv = x.shape[1] // LANES
    for v in range(nv):
        ref[i, v] = x[:, v * LANES:(v + 1) * LANES]
    return jnp.concatenate(
        [jnp.concatenate([ref[i, v, pl.ds(j, 8, stride=0), :] for v in range(nv)], axis=1)
         for j in range(8)], axis=0)


def _dot_nt(a, b):
    return lax.dot_general(a, b, (((1,), (1,)), ((), ())), preferred_element_type=F32)


def _dot_tn(a, b):
    return lax.dot_general(a, b, (((0,), (0,)), ((), ())), preferred_element_type=F32)


def _proj_hgrn2_kernel(h_ref, pre_ref, g_ref, w_ref, lb_ref, gain_ref, o_ref, u_ref,
                       xn_ref, p_ref, st_ref, bc_ref, *, first):
    @pl.when(pl.program_id(1) == 0)
    def _():
        st_ref[...] = jnp.zeros_like(st_ref)

    _normed_tile(h_ref, pre_ref, g_ref, xn_ref, first)
    xn = xn_ref[...]
    for k in range(4 * D_HG // D_S5):
        cols = slice(k * D_S5, (k + 1) * D_S5)
        p_ref[:, cols] = jnp.dot(xn, w_ref[:, cols], preferred_element_type=F32)
    u_ref[...] = jnp.dot(xn, w_ref[:, 4 * D_HG:], preferred_element_type=F32)

    nsel = CHUNK + 16
    row = lax.broadcasted_iota(jnp.int32, (nsel, 3 * CHUNK), 0)
    col = lax.broadcasted_iota(jnp.int32, (nsel, 3 * CHUNK), 1) & (CHUNK - 1)
    tgt = jnp.where(row < CHUNK, row,
                    jnp.where(row < CHUNK + 8, (row - CHUNK) * 8 + 7, (row - CHUNK - 8) * 8 + 3))
    tri3 = (tgt >= col).astype(BF16)
    ti = lax.broadcasted_iota(jnp.int32, (CHUNK, CHUNK), 0)
    si = lax.broadcasted_iota(jnp.int32, (CHUNK, CHUNK), 1)
    masks = {}
    for w, sh in ((32, 5), (16, 4), (8, 3)):
        masks[w] = ((ti >> sh) == (si >> sh) + 1) & ((ti >> (sh + 1)) == (si >> (sh + 1)))
    masks[0] = ((ti >> 3) == (si >> 3)) & (si <= ti)

    def chunk(c, carry):
        _hgrn2_chunk(p_ref, lb_ref, gain_ref, o_ref, st_ref, bc_ref, tri3, masks,
                     pl.ds(pl.multiple_of(c * CHUNK, CHUNK), CHUNK))
        return carry

    lax.fori_loop(0, p_ref.shape[0] // CHUNK, chunk, 0, unroll=2)


def _hgrn2_chunk(p_ref, lb_ref, gain_ref, o_ref, st_ref, bc_ref, tri3, masks, rows):
    q = p_ref[rows, 0:D_HG]
    z = p_ref[rows, D_HG:2 * D_HG]
    v = p_ref[rows, 2 * D_HG:3 * D_HG].astype(BF16)
    lb = lb_ref[...]

    th = 0.5 * jnp.tanh(0.5 * z)
    gl = jnp.log2(jnp.maximum(lb + (1.0 - lb) * (0.5 + th), F_FLOOR))
    kk = (1.0 - lb) * (0.5 - th)
    qf = _silu(q)

    hi = gl.astype(BF16)
    r1 = gl - hi.astype(F32)
    mid = r1.astype(BF16)
    lo = (r1 - mid.astype(F32)).astype(BF16)
    cum = jnp.dot(tri3, jnp.concatenate([hi, mid, lo], axis=0), preferred_element_type=F32)
    b = cum[0:CHUNK]
    l8 = cum[CHUNK:CHUNK + 8]
    m8 = cum[CHUNK + 8:CHUNK + 16]
    j8 = lax.broadcasted_iota(jnp.int32, l8.shape, 0)
    s8 = jnp.where(j8 == 0, 0.0, pltpu.roll(l8, 1, 0))

    qe = qf * jnp.exp2(b - _expand_rows(bc_ref, 0, s8))
    kf = kk * jnp.exp2(_expand_rows(bc_ref, 1, l8) - b)
    qh = {8: qe.astype(BF16)}
    kh = {8: kf.astype(BF16)}
    for lvl, (w, m) in enumerate(((16, 2), (32, 4), (64, 8))):
        cq = jnp.exp2(s8 - _pick_rows(s8, m, True))
        ck = jnp.exp2(_pick_rows(l8, m, False) - l8)
        qh[w] = (qe * _expand_rows(bc_ref, 3 + 2 * lvl, cq)).astype(BF16)
        kh[w] = (kf * _expand_rows(bc_ref, 4 + 2 * lvl, ck)).astype(BF16)
    b_mid = _expand_rows(bc_ref, 2, m8)
    qh[0] = (qf * jnp.exp2(b - b_mid)).astype(BF16)
    kh[0] = (kk * jnp.exp2(b_mid - b)).astype(BF16)
    dec_all = jnp.exp2(l8[7:8, :])

    gate = p_ref[rows, 3 * D_HG:4 * D_HG]
    gate = _silu(gate)
    gain = gain_ref[...]

    heads = [slice(h * HG_DK, (h + 1) * HG_DK) for h in range(HG_HEADS)]
    sts = [st_ref[h] for h in range(HG_HEADS)]
    o_inter = [_dot_nt(qh[64][:, sl], st.astype(BF16)) for sl, st in zip(heads, sts)]
    scores = []
    for sl in heads:
        sc = jnp.zeros((CHUNK, CHUNK), F32)
        for w in (32, 16, 8, 0):
            sc = jnp.where(masks[w], _dot_nt(qh[w][:, sl], kh[w][:, sl]), sc)
        scores.append(sc.astype(BF16))
    outs = [oi + jnp.dot(sc, v[:, sl], preferred_element_type=F32)
            for oi, sc, sl in zip(o_inter, scores, heads)]
    for h, (sl, st) in enumerate(zip(heads, sts)):
        st_ref[h] = dec_all[:, sl] * st + _dot_tn(v[:, sl], kh[64][:, sl])
    for o, sl in zip(outs, heads):
        ms = jnp.mean(o * o, axis=-1, keepdims=True)
        o = o * lax.rsqrt(ms + EPS) * gain[:, sl]
        o_ref[rows, sl] = (o * gate[:, sl]).astype(BF16)


def _proj_hgrn2(h, prefix, g, w_all, layer, lb, gain, first, bsz, chunks_per_step=11):
    d = h.shape[1]
    lp = h.shape[0] // bsz + (CHUNK if first else 0)
    rows = chunks_per_step * CHUNK
    const = lambda b, c: (0, 0)
    return pl.pallas_call(
        functools.partial(_proj_hgrn2_kernel, first=first),
        grid=(bsz, lp // rows),
        in_specs=[
            _row_tile_specs(first, bsz, lp, rows, d),
            pl.BlockSpec((CHUNK, d), const),
            pl.BlockSpec((1, d), const),
            pl.BlockSpec((None, d, D_IN), lambda b, c: (layer, 0, 0),
                         pipeline_mode=pl.Buffered(1)),
            pl.BlockSpec((1, D_HG), const),
            pl.BlockSpec((1, D_HG), const),
        ],
        out_specs=[pl.BlockSpec((None, rows, D_HG), lambda b, c: (b, c, 0)),
                   pl.BlockSpec((None, rows, D_S5), lambda b, c: (b, c, 0))],
        out_shape=[jax.ShapeDtypeStruct((bsz, lp, D_HG), BF16),
                   jax.ShapeDtypeStruct((bsz, lp, D_S5), F32)],
        scratch_shapes=[
            pltpu.VMEM((rows, d), BF16),
            pltpu.VMEM((rows, 4 * D_HG), F32),
            pltpu.VMEM((HG_HEADS, HG_DK, HG_DK), F32),
            pltpu.VMEM((9, D_HG // LANES, 8, LANES), F32),
        ],
        compiler_params=pltpu.CompilerParams(
            dimension_semantics=("arbitrary", "arbitrary"), vmem_limit_bytes=VMEM_LIMIT_BIG),
        name="proj_hgrn2",
    )(h, prefix, g.reshape(1, d), w_all, lb.reshape(1, D_HG), gain.reshape(1, D_HG))


def _gelu_tanh(x):
    c = 0.7978845608028654
    half = 0.5 * x
    return half + half * jnp.tanh(x * (c + (c * 0.044715) * (x * x)))


def _s5core_kernel(u_ref, t_ref, h_ref, g_ref, a_ref, y_ref, v_ref, xp_ref, *, bsz):
    nchunks = u_ref.shape[0] // bsz
    u = u_ref[...]
    v_ref[...] = jnp.dot(u, h_ref[0], preferred_element_type=F32)
    a_re = jnp.broadcast_to(a_ref[0, 0:1, :], (bsz, LANES))
    a_im = jnp.broadcast_to(a_ref[0, 1:2, :], (bsz, LANES))

    def step(n, carry):
        x_re, x_im = carry
        rows = pl.ds(pl.multiple_of(n * bsz, bsz), bsz)
        xp_ref[rows, 0:LANES] = x_re.astype(BF16)
        xp_ref[rows, LANES:2 * LANES] = x_im.astype(BF16)
        return (a_re * x_re - a_im * x_im + v_ref[rows, 0:LANES],
                a_re * x_im + a_im * x_re + v_ref[rows, LANES:2 * LANES])

    zero = jnp.zeros((bsz, LANES), F32)
    lax.fori_loop(0, nchunks, step, (zero, zero), unroll=4)

    ys = jnp.dot(xp_ref[...], g_ref[0], preferred_element_type=F32)
    for k in range(2):
        cols = slice(k * S5_K, (k + 1) * S5_K)
        y_ref[:, cols] = ys[:, cols] + jnp.dot(u[:, cols], t_ref[0, k], preferred_element_type=F32)


def _s5core(uc, params, layer, bsz):
    tmat, hmat, gmat, a2 = params
    rows = uc.shape[0]
    return pl.pallas_call(
        functools.partial(_s5core_kernel, bsz=bsz),
        grid=(S5_PAIRS,),
        in_specs=[
            pl.BlockSpec((rows, 2 * S5_K), lambda p: (0, p)),
            pl.BlockSpec((None, 1, 2, S5_K, S5_K), lambda p: (layer, p, 0, 0, 0)),
            pl.BlockSpec((None, 1, 2 * S5_K, 4 * S5_STATE), lambda p: (layer, p, 0, 0)),
            pl.BlockSpec((None, 1, 4 * S5_STATE, 2 * S5_K), lambda p: (layer, p, 0, 0)),
            pl.BlockSpec((None, 1, 2, 2 * S5_STATE), lambda p: (layer, p, 0, 0)),
        ],
        out_specs=pl.BlockSpec((rows, 2 * S5_K), lambda p: (0, p)),
        out_shape=jax.ShapeDtypeStruct((rows, S5_GROUPS * S5_K), F32),
        scratch_shapes=[
            pltpu.VMEM((rows, 4 * S5_STATE), F32),
            pltpu.VMEM((rows, 4 * S5_STATE), BF16),
        ],
        compiler_params=pltpu.CompilerParams(
            dimension_semantics=("arbitrary",), vmem_limit_bytes=VMEM_LIMIT),
        name="s5core",
    )(uc, tmat, hmat, gmat, a2)


def _atom_transpose(groups):
    lane = lax.broadcasted_iota(jnp.int32, groups[0][0].shape, 1)
    groups = [list(xs) for xs in groups]
    for s in range(3):
        d = 1 << s
        keep = ((lane >> (4 + s)) & 1) == 0
        for xs in groups:
            for i in range(8):
                if i & d:
                    continue
                lo, hi = xs[i], xs[i + d]
                xs[i] = jnp.where(keep, lo, pltpu.roll(hi, S5_GROUP * d, 1))
                xs[i + d] = jnp.where(keep, pltpu.roll(lo, LANES - S5_GROUP * d, 1), hi)
    return groups


def _to_chunks_kernel(u_ref, o_ref, rows_ref):
    bsz, tt, width = u_ref.shape
    pitch = rows_ref.shape[1] // bsz
    for v in range(width // LANES):
        for b in range(bsz):
            rows_ref[v, b * pitch:b * pitch + tt, :] = u_ref[b, :, v * LANES:(v + 1) * LANES]
    halves = [(v, n, k) for v in range(width // LANES) for n in range(tt // S5_R) for k in range(2)]
    groups = _atom_transpose(
        [[rows_ref[v, pl.ds(n * S5_R + 8 * k + j, bsz, stride=pitch), :] for j in range(8)]
         for v, n, k in halves])
    for (v, n, k), ys in zip(halves, groups):
        for g, y in enumerate(ys):
            lane0 = (8 * v + g) * S5_K + k * LANES
            o_ref[n * bsz:(n + 1) * bsz, lane0:lane0 + LANES] = y.astype(BF16)


def _to_chunks(u3, tt=176, width=512):
    bsz, lp, _ = u3.shape
    return pl.pallas_call(
        _to_chunks_kernel,
        grid=(lp // tt, D_S5 // width),
        in_specs=[pl.BlockSpec((bsz, tt, width), lambda i, v: (0, i, v))],
        out_specs=pl.BlockSpec((tt, width // S5_GROUP * S5_K), lambda i, v: (i, v)),
        out_shape=jax.ShapeDtypeStruct((lp // S5_R * bsz, S5_GROUPS * S5_K), BF16),
        scratch_shapes=[pltpu.VMEM((width // LANES, bsz * (tt + 8), LANES), F32)],
        compiler_params=pltpu.CompilerParams(
            dimension_semantics=("arbitrary", "arbitrary"), vmem_limit_bytes=VMEM_LIMIT),
        name="s5_to_chunks",
    )(u3)


def _s5glu_kernel(yc_ref, u_ref, d_ref, w_ref, b_ref, gain_ref, o_ref, yn_ref):
    bsz, tt, _ = u_ref.shape
    pitch = yn_ref.shape[1] // bsz
    for n in range(tt // S5_R):
        halves = [(v, k) for v in range(D_S5 // LANES) for k in range(2)]
        groups = _atom_transpose(
            [[yc_ref[n * bsz:(n + 1) * bsz,
                     (8 * v + g) * S5_K + k * LANES:(8 * v + g) * S5_K + (k + 1) * LANES]
              for g in range(8)] for v, k in halves])
        for (v, k), xs in zip(halves, groups):
            for j, x in enumerate(xs):
                yn_ref[v, pl.ds(n * S5_R + 8 * k + j, bsz, stride=pitch), :] = x
    y = jnp.concatenate(
        [jnp.concatenate([yn_ref[v, b * pitch:b * pitch + tt, :] for v in range(D_S5 // LANES)],
                         axis=1) for b in range(bsz)], axis=0)
    y = _gelu_tanh(y + d_ref[...] * u_ref[...].reshape(bsz * tt, D_S5))
    g = jnp.dot(y.astype(BF16), w_ref[...], preferred_element_type=F32) + b_ref[...]
    o_ref[...] = _rms(y * _sigmoid(g), gain_ref[...]).reshape(bsz, tt, D_S5).astype(BF16)


def _s5glu(yc, u3, d_skip, wglu_all, layer, bglu, gain, tt=48):
    bsz, lp, _ = u3.shape
    return pl.pallas_call(
        _s5glu_kernel,
        grid=(lp // tt,),
        in_specs=[
            pl.BlockSpec((tt // S5_R * bsz, S5_GROUPS * S5_K), lambda i: (i, 0)),
            pl.BlockSpec((bsz, tt, D_S5), lambda i: (0, i, 0)),
            pl.BlockSpec((1, D_S5), lambda i: (0, 0)),
            pl.BlockSpec((None, D_S5, D_S5), lambda i: (layer, 0, 0)),
            pl.BlockSpec((1, D_S5), lambda i: (0, 0)),
            pl.BlockSpec((1, D_S5), lambda i: (0, 0)),
        ],
        out_specs=pl.BlockSpec((bsz, tt, D_S5), lambda i: (0, i, 0)),
        out_shape=jax.ShapeDtypeStruct((bsz, lp, D_S5), BF16),
        scratch_shapes=[pltpu.VMEM((D_S5 // LANES, bsz * (tt + 8), LANES), F32)],
        compiler_params=pltpu.CompilerParams(
            dimension_semantics=("arbitrary",), vmem_limit_bytes=VMEM_LIMIT),
        name="s5glu",
    )(yc, u3, d_skip.reshape(1, D_S5), wglu_all, bglu.reshape(1, D_S5), gain.reshape(1, D_S5))


def _s5_params(lam_re, lam_im, log_step, b_re, b_im, c_re, c_im):
    hp = lax.Precision.HIGHEST
    a_re = jnp.minimum(lam_re.astype(F32), -1e-4)
    a_im = lam_im.astype(F32)
    dt = jnp.exp(log_step.astype(F32))[:, None]
    mag = jnp.exp(a_re * dt)
    ab_re = mag * jnp.cos(a_im * dt)
    ab_im = mag * jnp.sin(a_im * dt)
    den = a_re * a_re + a_im * a_im
    x_re, x_im = ab_re - 1.0, ab_im
    z_re = (x_re * a_re + x_im * a_im) / den
    z_im = (x_im * a_re - x_re * a_im) / den
    br, bi = b_re.astype(F32), b_im.astype(F32)
    bb_re = z_re[..., None] * br - z_im[..., None] * bi
    bb_im = z_re[..., None] * bi + z_im[..., None] * br
    cr, ci = c_re.astype(F32), c_im.astype(F32)

    pw_re, pw_im = [jnp.ones_like(ab_re)], [jnp.zeros_like(ab_re)]
    for _ in range(S5_R):
        r, i = pw_re[-1], pw_im[-1]
        pw_re.append(r * ab_re - i * ab_im)
        pw_im.append(r * ab_im + i * ab_re)
    p_re, p_im = jnp.stack(pw_re), jnp.stack(pw_im)

    q_re, q_im = p_re[:S5_R, :, :, None], p_im[:S5_R, :, :, None]
    m_re = q_re * bb_re - q_im * bb_im
    m_im = q_re * bb_im + q_im * bb_re
    kern = (jnp.einsum('gop,lgph->ghlo', cr, m_re, precision=hp)
            - jnp.einsum('gop,lgph->ghlo', ci, m_im, precision=hp))
    kz = jnp.concatenate([jnp.zeros((S5_GROUPS, S5_GROUP, S5_K), F32),
                          kern.reshape(S5_GROUPS, S5_GROUP, S5_K)], axis=-1)
    tmat = jnp.stack([kz[:, :, (S5_R - s) * S5_GROUP:(S5_R - s) * S5_GROUP + S5_K]
                      for s in range(S5_R)], axis=1).reshape(S5_PAIRS, 2, S5_K, S5_K)

    bt_re, bt_im = bb_re.transpose(0, 2, 1)[:, None], bb_im.transpose(0, 2, 1)[:, None]
    rp_re = jnp.moveaxis(p_re[S5_R - 1::-1], 0, 1)[:, :, None, :]
    rp_im = jnp.moveaxis(p_im[S5_R - 1::-1], 0, 1)[:, :, None, :]
    h_re = (rp_re * bt_re - rp_im * bt_im).reshape(S5_PAIRS, 2, S5_K, S5_STATE)
    h_im = (rp_re * bt_im + rp_im * bt_re).reshape(S5_PAIRS, 2, S5_K, S5_STATE)
    zh = jnp.zeros_like(h_re[:, 0])
    hmat = jnp.concatenate([
        jnp.concatenate([h_re[:, 0], zh, h_im[:, 0], zh], axis=-1),
        jnp.concatenate([zh, h_re[:, 1], zh, h_im[:, 1]], axis=-1)], axis=1)

    ct_re, ct_im = cr.transpose(0, 2, 1)[:, :, None, :], ci.transpose(0, 2, 1)[:, :, None, :]
    e_re = jnp.moveaxis(p_re[1:], 0, 2)[..., None]
    e_im = jnp.moveaxis(p_im[1:], 0, 2)[..., None]
    g_re = (ct_re * e_re - ct_im * e_im).reshape(S5_PAIRS, 2, S5_STATE, S5_K)
    g_im = -(ct_re * e_im + ct_im * e_re).reshape(S5_PAIRS, 2, S5_STATE, S5_K)
    zg = jnp.zeros_like(g_re[:, 0])
    gmat = jnp.concatenate([
        jnp.concatenate([g_re[:, 0], zg], axis=-1), jnp.concatenate([zg, g_re[:, 1]], axis=-1),
        jnp.concatenate([g_im[:, 0], zg], axis=-1), jnp.concatenate([zg, g_im[:, 1]], axis=-1)],
        axis=1)

    a2 = jnp.stack([p_re[S5_R].reshape(S5_PAIRS, 2 * S5_STATE),
                    p_im[S5_R].reshape(S5_PAIRS, 2 * S5_STATE)], axis=1)
    return tmat.astype(BF16), hmat.astype(BF16), gmat.astype(BF16), a2


def _s5(u3, params, layer, d_skip, wglu_all, bglu, gain):
    yc = _s5core(_to_chunks(u3), params, layer, u3.shape[0])
    return _s5glu(yc, u3, d_skip, wglu_all, layer, bglu, gain)


def _outproj_kernel(h_ref, pre_ref, a_ref, b_ref, w_ref, o_ref, *, first):
    y = (jnp.dot(a_ref[...], w_ref[0], preferred_element_type=F32)
         + jnp.dot(b_ref[...], w_ref[1], preferred_element_type=F32))
    if first:
        npre = pre_ref.shape[0]

        @pl.when(pl.program_id(1) == 0)
        def _():
            o_ref[0:npre, :] = y[0:npre, :] + pre_ref[...]
            o_ref[npre:, :] = y[npre:, :] + h_ref[0:h_ref.shape[0] - npre, :]

        @pl.when(pl.program_id(1) != 0)
        def _():
            o_ref[...] = y + h_ref[...]
    else:
        o_ref[...] = y + h_ref[...]


def _outproj(h, prefix, a, b, w_all, layer, first, bsz, tm=704):
    k = a.shape[1]
    d = h.shape[1]
    lp = a.shape[0] // bsz
    tiles = lp // tm
    flat = lambda b, j: (b * tiles + j, 0)
    return pl.pallas_call(
        functools.partial(_outproj_kernel, first=first),
        grid=(bsz, tiles),
        in_specs=[
            _row_tile_specs(first, bsz, lp, tm, d),
            pl.BlockSpec((CHUNK, d), lambda b, j: (0, 0)),
            pl.BlockSpec((tm, k), flat),
            pl.BlockSpec((tm, k), flat),
            pl.BlockSpec((None, 2, k, d), lambda b, j: (layer, 0, 0, 0)),
        ],
        out_specs=pl.BlockSpec((tm, d), flat),
        out_shape=jax.ShapeDtypeStruct((bsz * lp, d), F32),
        compiler_params=pltpu.CompilerParams(
            dimension_semantics=("arbitrary", "arbitrary"), vmem_limit_bytes=VMEM_LIMIT),
        name="outproj",
    )(h, prefix, a, b, w_all)


FFN_HALO = 16


def _ffn_kernel(h_ref, halo_ref, g_ref, wg_ref, wu_ref, cw_ref, cb_ref, wd_ref, fg_ref, o_ref,
                hn_ref, *, final):
    f_axis = 2 if final else 1
    f = pl.program_id(f_axis)

    @pl.when(f == 0)
    def _():
        x = h_ref[...]
        hn_ref[FFN_HALO:, :] = _rms(x, g_ref[...]).astype(BF16)
        hn_ref[0:FFN_HALO, :] = _rms(halo_ref[...], g_ref[...]).astype(BF16)
        o_ref[...] = x

    a = jnp.dot(hn_ref[...], wg_ref[...], preferred_element_type=F32)
    up = jnp.dot(hn_ref[FFN_HALO:, :], wu_ref[...], preferred_element_type=F32)
    cw = cw_ref[...]
    conv = (cb_ref[...]
            + cw[0:1, :] * pltpu.roll(a, 2, 0)[FFN_HALO:, :]
            + cw[1:2, :] * pltpu.roll(a, 1, 0)[FFN_HALO:, :]
            + cw[2:3, :] * a[FFN_HALO:, :])
    hid = (_silu(conv) * up).astype(BF16)
    o_ref[...] += jnp.dot(hid, wd_ref[...], preferred_element_type=F32)

    if final:
        @pl.when(f == pl.num_programs(f_axis) - 1)
        def _():
            o_ref[...] = _rms(o_ref[...], fg_ref[...])


def _ffn(h, g, layer, wg_all, wu_all, cw_all, cb_all, wd_all, fg, final, bsz, tm=512, tf=512):
    t, d = h.shape
    ff = wg_all.shape[2]
    hb = tm // FFN_HALO
    lp = t // bsz
    if final:
        grid = (bsz, (lp - CHUNK) // tm, ff // tf)
        row0 = lambda b, j, back: pl.multiple_of(b * lp + CHUNK + j * tm - back, FFN_HALO)
        row_specs = [
            pl.BlockSpec((pl.Element(tm), pl.Element(d)), lambda b, j, f: (row0(b, j, 0), 0)),
            pl.BlockSpec((pl.Element(FFN_HALO), pl.Element(d)),
                         lambda b, j, f: (row0(b, j, FFN_HALO), 0)),
        ]
        out_spec = pl.BlockSpec((tm, d), lambda b, j, f: (b * ((lp - CHUNK) // tm) + j, 0))
        out_rows = bsz * (lp - CHUNK)
        wmap = lambda fn: (lambda b, j, f: fn(f))
    else:
        grid = (t // tm, ff // tf)
        row_specs = [
            pl.BlockSpec((tm, d), lambda i, f: (i, 0)),
            pl.BlockSpec((FFN_HALO, d), lambda i, f: (jnp.maximum(i * hb - 1, 0), 0)),
        ]
        out_spec = pl.BlockSpec((tm, d), lambda i, f: (i, 0))
        out_rows = t
        wmap = lambda fn: (lambda i, f: fn(f))
    return pl.pallas_call(
        functools.partial(_ffn_kernel, final=final),
        grid=grid,
        in_specs=row_specs + [
            pl.BlockSpec((1, d), wmap(lambda f: (0, 0))),
            pl.BlockSpec((None, d, tf), wmap(lambda f: (layer, 0, f))),
            pl.BlockSpec((None, d, tf), wmap(lambda f: (layer, 0, f))),
            pl.BlockSpec((None, 3, tf), wmap(lambda f: (layer, 0, f))),
            pl.BlockSpec((None, 1, tf), wmap(lambda f: (layer, 0, f))),
            pl.BlockSpec((None, tf, d), wmap(lambda f: (layer, f, 0))),
            pl.BlockSpec((1, d), wmap(lambda f: (0, 0))),
        ],
        out_specs=out_spec,
        out_shape=jax.ShapeDtypeStruct((out_rows, d), F32),
        scratch_shapes=[pltpu.VMEM((tm + FFN_HALO, d), BF16)],
        compiler_params=pltpu.CompilerParams(
            dimension_semantics=("arbitrary",) * len(grid), vmem_limit_bytes=VMEM_LIMIT),
        name="ffn_final" if final else "ffn",
    )(h, h, g.reshape(1, d), wg_all, wu_all, cw_all, cb_all.reshape(cb_all.shape[0], 1, ff),
      wd_all, fg.reshape(1, d))


def kernel(x, meta_tokens, lb_logits, norm_mix, w_in, hg_norm, s5_lambda_re, s5_lambda_im,
           s5_log_step, s5_b_re, s5_b_im, s5_c_re, s5_c_im, s5_d, w_glu, b_glu, s5_norm, w_out,
           norm_ffn, w_ffn_gate, w_ffn_up, ffn_conv_w, ffn_conv_b, w_ffn_down, final_norm):
    bsz, seq, d = x.shape
    lp = seq + CHUNK
    prefix = jnp.concatenate([jnp.zeros((SEQ_PAD, d), x.dtype), meta_tokens.astype(x.dtype)], axis=0)
    h = x.reshape(bsz * seq, d)

    sm = jax.nn.softmax(lb_logits.astype(F32), axis=0)
    lb_all = jnp.cumsum(sm, axis=0) - sm[0:1]

    w_in_b = w_in.astype(BF16)
    w_out_b = w_out.astype(BF16).reshape(DEPTH, 2, D_HG, d)
    w_glu_b = w_glu.astype(BF16)
    wg_b, wu_b, wd_b = (w.astype(BF16) for w in (w_ffn_gate, w_ffn_up, w_ffn_down))
    s5_params = jax.vmap(_s5_params)(s5_lambda_re, s5_lambda_im, s5_log_step,
                                     s5_b_re, s5_b_im, s5_c_re, s5_c_im)

    for l in range(DEPTH):
        o_hg, u3 = _proj_hgrn2(h, prefix, norm_mix[l], w_in_b, l, lb_all[l], hg_norm[l],
                               first=(l == 0), bsz=bsz)
        o_s5 = _s5(u3, s5_params, l, s5_d[l], w_glu_b, b_glu[l], s5_norm[l])
        h = _outproj(h, prefix, o_hg.reshape(bsz * lp, D_HG), o_s5.reshape(bsz * lp, D_S5),
                     w_out_b, l, first=(l == 0), bsz=bsz)
        h = _ffn(h, norm_ffn[l], l, wg_b, wu_b, ffn_conv_w, ffn_conv_b, wd_b, final_norm,
                 final=(l == DEPTH - 1), bsz=bsz, tm=512 if l == DEPTH - 1 else 768)
    return h.reshape(bsz, seq, d)
```

```python
import functools

import jax
import jax.numpy as jnp
from jax import lax
from jax.experimental import pallas as pl
from jax.experimental.pallas import tpu as pltpu

F32 = jnp.float32
BF16 = jnp.bfloat16
LANES = 128

DEPTH = 2
CHUNK = 64
N_META = 16
D_HG = 1024
HG_HEADS = 8
HG_DK = 128
D_S5 = 1024
S5_GROUP = 16
S5_GROUPS = 64
S5_STATE = 64
D_IN = 4 * D_HG + D_S5
EPS = 1e-6
F_FLOOR = 1e-6

SEQ_PAD = CHUNK - N_META
VMEM_LIMIT = 56 * 1024 * 1024
VMEM_LIMIT_BIG = 62 * 1024 * 1024

S5_R = 16
S5_K = S5_R * S5_GROUP
S5_PAIRS = S5_GROUPS // 2


def _rms(x, g):
    ms = jnp.mean(x * x, axis=-1, keepdims=True)
    return x * lax.rsqrt(ms + EPS) * g


def _sigmoid(x):
    return 0.5 + 0.5 * jnp.tanh(0.5 * x)


def _silu(x):
    half = 0.5 * x
    return half + half * jnp.tanh(half)


def _normed_tile(h_ref, pre_ref, g_ref, xn_ref, first):
    if first:
        npre = pre_ref.shape[0]

        @pl.when(pl.program_id(1) == 0)
        def _():
            xn_ref[0:npre, :] = _rms(pre_ref[...], g_ref[...]).astype(BF16)
            xn_ref[npre:, :] = _rms(h_ref[0:h_ref.shape[0] - npre, :], g_ref[...]).astype(BF16)

        @pl.when(pl.program_id(1) != 0)
        def _():
            xn_ref[...] = _rms(h_ref[...], g_ref[...]).astype(BF16)
    else:
        xn_ref[...] = _rms(h_ref[...], g_ref[...]).astype(BF16)


def _row_tile_specs(first, bsz, lp, tm, d):
    tiles = lp // tm
    if first:
        def rows(b, j, *_):
            return (pl.multiple_of(b * (lp - CHUNK) + jnp.maximum(j * tm - CHUNK, 0), 8), 0)
        return pl.BlockSpec((pl.Element(tm), pl.Element(d)), rows)
    return pl.BlockSpec((tm, d), lambda b, j, *_: (b * tiles + j, 0))


def _pick_rows(x, m, first):
    j = lax.broadcasted_iota(jnp.int32, x.shape, 0)
    out = None
    for blk in range(8 // m - 1, -1, -1):
        r = blk * m + (0 if first else m - 1)
        row = jnp.broadcast_to(x[r:r + 1, :], x.shape)
        out = row if out is None else jnp.where(j < (blk + 1) * m, row, out)
    return out


def _expand_rows(ref, i, x):
    nv = x.shape[1] // LANES
    for v in range(nv):
        ref[i, v] = x[:, v * LANES:(v + 1) * LANES]
    return jnp.concatenate(
        [jnp.concatenate([ref[i, v, pl.ds(j, 8, stride=0), :] for v in range(nv)], axis=1)
         for j in range(8)], axis=0)


def _dot_nt(a, b):
    return lax.dot_general(a, b, (((1,), (1,)), ((), ())), preferred_element_type=F32)


def _dot_tn(a, b):
    return lax.dot_general(a, b, (((0,), (0,)), ((), ())), preferred_element_type=F32)


def _proj_hgrn2_kernel(h_ref, pre_ref, g_ref, w_ref, lb_ref, gain_ref, o_ref, u_ref,
                       xn_ref, p_ref, st_ref, bc_ref, *, first):
    @pl.when(pl.program_id(1) == 0)
    def _():
        st_ref[...] = jnp.zeros_like(st_ref)

    _normed_tile(h_ref, pre_ref, g_ref, xn_ref, first)
    xn = xn_ref[...]
    for k in range(4 * D_HG // D_S5):
        cols = slice(k * D_S5, (k + 1) * D_S5)
        p_ref[:, cols] = jnp.dot(xn, w_ref[:, cols], preferred_element_type=F32)
    u_ref[...] = jnp.dot(xn, w_ref[:, 4 * D_HG:], preferred_element_type=F32)

    nsel = CHUNK + 16
    row = lax.broadcasted_iota(jnp.int32, (nsel, 3 * CHUNK), 0)
    col = lax.broadcasted_iota(jnp.int32, (nsel, 3 * CHUNK), 1) & (CHUNK - 1)
    tgt = jnp.where(row < CHUNK, row,
                    jnp.where(row < CHUNK + 8, (row - CHUNK) * 8 + 7, (row - CHUNK - 8) * 8 + 3))
    tri3 = (tgt >= col).astype(BF16)
    ti = lax.broadcasted_iota(jnp.int32, (CHUNK, CHUNK), 0)
    si = lax.broadcasted_iota(jnp.int32, (CHUNK, CHUNK), 1)
    masks = {}
    for w, sh in ((32, 5), (16, 4), (8, 3)):
        masks[w] = ((ti >> sh) == (si >> sh) + 1) & ((ti >> (sh + 1)) == (si >> (sh + 1)))
    masks[0] = ((ti >> 3) == (si >> 3)) & (si <= ti)

    def chunk(c, carry):
        _hgrn2_chunk(p_ref, lb_ref, gain_ref, o_ref, st_ref, bc_ref, tri3, masks,
                     pl.ds(pl.multiple_of(c * CHUNK, CHUNK), CHUNK))
        return carry

    lax.fori_loop(0, p_ref.shape[0] // CHUNK, chunk, 0, unroll=2)


def _hgrn2_chunk(p_ref, lb_ref, gain_ref, o_ref, st_ref, bc_ref, tri3, masks, rows):
    q = p_ref[rows, 0:D_HG]
    z = p_ref[rows, D_HG:2 * D_HG]
    v = p_ref[rows, 2 * D_HG:3 * D_HG].astype(BF16)
    lb = lb_ref[...]

    th = 0.5 * jnp.tanh(0.5 * z)
    gl = jnp.log2(jnp.maximum(lb + (1.0 - lb) * (0.5 + th), F_FLOOR))
    kk = (1.0 - lb) * (0.5 - th)
    qf = _silu(q)

    hi = gl.astype(BF16)
    r1 = gl - hi.astype(F32)
    mid = r1.astype(BF16)
    lo = (r1 - mid.astype(F32)).astype(BF16)
    cum = jnp.dot(tri3, jnp.concatenate([hi, mid, lo], axis=0), preferred_element_type=F32)
    b = cum[0:CHUNK]
    l8 = cum[CHUNK:CHUNK + 8]
    m8 = cum[CHUNK + 8:CHUNK + 16]
    j8 = lax.broadcasted_iota(jnp.int32, l8.shape, 0)
    s8 = jnp.where(j8 == 0, 0.0, pltpu.roll(l8, 1, 0))

    qe = qf * jnp.exp2(b - _expand_rows(bc_ref, 0, s8))
    kf = kk * jnp.exp2(_expand_rows(bc_ref, 1, l8) - b)
    qh = {8: qe.astype(BF16)}
    kh = {8: kf.astype(BF16)}
    for lvl, (w, m) in enumerate(((16, 2), (32, 4), (64, 8))):
        cq = jnp.exp2(s8 - _pick_rows(s8, m, True))
        ck = jnp.exp2(_pick_rows(l8, m, False) - l8)
        qh[w] = (qe * _expand_rows(bc_ref, 3 + 2 * lvl, cq)).astype(BF16)
        kh[w] = (kf * _expand_rows(bc_ref, 4 + 2 * lvl, ck)).astype(BF16)
    b_mid = _expand_rows(bc_ref, 2, m8)
    qh[0] = (qf * jnp.exp2(b - b_mid)).astype(BF16)
    kh[0] = (kk * jnp.exp2(b_mid - b)).astype(BF16)
    dec_all = jnp.exp2(l8[7:8, :])

    gate = p_ref[rows, 3 * D_HG:4 * D_HG]
    gate = _silu(gate)
    gain = gain_ref[...]

    heads = [slice(h * HG_DK, (h + 1) * HG_DK) for h in range(HG_HEADS)]
    sts = [st_ref[h] for h in range(HG_HEADS)]
    o_inter = [_dot_nt(qh[64][:, sl], st.astype(BF16)) for sl, st in zip(heads, sts)]
    scores = []
    for sl in heads:
        sc = jnp.zeros((CHUNK, CHUNK), F32)
        for w in (32, 16, 8, 0):
            sc = jnp.where(masks[w], _dot_nt(qh[w][:, sl], kh[w][:, sl]), sc)
        scores.append(sc.astype(BF16))
    outs = [oi + jnp.dot(sc, v[:, sl], preferred_element_type=F32)
            for oi, sc, sl in zip(o_inter, scores, heads)]
    for h, (sl, st) in enumerate(zip(heads, sts)):
        st_ref[h] = dec_all[:, sl] * st + _dot_tn(v[:, sl], kh[64][:, sl])
    for o, sl in zip(outs, heads):
        ms = jnp.mean(o * o, axis=-1, keepdims=True)
        o = o * lax.rsqrt(ms + EPS) * gain[:, sl]
        o_ref[rows, sl] = (o * gate[:, sl]).astype(BF16)


def _proj_hgrn2(h, prefix, g, w_all, layer, lb, gain, first, bsz, chunks_per_step=11):
    d = h.shape[1]
    lp = h.shape[0] // bsz + (CHUNK if first else 0)
    rows = chunks_per_step * CHUNK
    const = lambda b, c: (0, 0)
    return pl.pallas_call(
        functools.partial(_proj_hgrn2_kernel, first=first),
        grid=(bsz, lp // rows),
        in_specs=[
            _row_tile_specs(first, bsz, lp, rows, d),
            pl.BlockSpec((CHUNK, d), const),
            pl.BlockSpec((1, d), const),
            pl.BlockSpec((None, d, D_IN), lambda b, c: (layer, 0, 0),
                         pipeline_mode=pl.Buffered(1)),
            pl.BlockSpec((1, D_HG), const),
            pl.BlockSpec((1, D_HG), const),
        ],
        out_specs=[pl.BlockSpec((None, rows, D_HG), lambda b, c: (b, c, 0)),
                   pl.BlockSpec((None, rows, D_S5), lambda b, c: (b, c, 0))],
        out_shape=[jax.ShapeDtypeStruct((bsz, lp, D_HG), BF16),
                   jax.ShapeDtypeStruct((bsz, lp, D_S5), F32)],
        scratch_shapes=[
            pltpu.VMEM((rows, d), BF16),
            pltpu.VMEM((rows, 4 * D_HG), F32),
            pltpu.VMEM((HG_HEADS, HG_DK, HG_DK), F32),
            pltpu.VMEM((9, D_HG // LANES, 8, LANES), F32),
        ],
        compiler_params=pltpu.CompilerParams(
            dimension_semantics=("arbitrary", "arbitrary"), vmem_limit_bytes=VMEM_LIMIT_BIG),
        name="proj_hgrn2",
    )(h, prefix, g.reshape(1, d), w_all, lb.reshape(1, D_HG), gain.reshape(1, D_HG))


def _gelu_tanh(x):
    c = 0.7978845608028654
    half = 0.5 * x
    return half + half * jnp.tanh(x * (c + (c * 0.044715) * (x * x)))


def _s5core_kernel(u_ref, t_ref, h_ref, g_ref, a_ref, y_ref, v_ref, xp_ref, *, bsz):
    nchunks = u_ref.shape[0] // bsz
    u = u_ref[...]
    v_ref[...] = jnp.dot(u, h_ref[0], preferred_element_type=F32)
    a_re = jnp.broadcast_to(a_ref[0, 0:1, :], (bsz, LANES))
    a_im = jnp.broadcast_to(a_ref[0, 1:2, :], (bsz, LANES))

    y_in = [jnp.dot(u[:, k * S5_K:(k + 1) * S5_K], t_ref[0, k], preferred_element_type=F32)
            for k in range(2)]

    x_re = x_im = jnp.zeros((bsz, LANES), F32)
    for n in range(nchunks):
        rows = slice(n * bsz, (n + 1) * bsz)
        xp_ref[rows, 0:LANES] = x_re.astype(BF16)
        xp_ref[rows, LANES:2 * LANES] = x_im.astype(BF16)
        x_re, x_im = (a_re * x_re - a_im * x_im + v_ref[rows, 0:LANES],
                      a_re * x_im + a_im * x_re + v_ref[rows, LANES:2 * LANES])

    ys = jnp.dot(xp_ref[...], g_ref[0], preferred_element_type=F32)
    for k in range(2):
        cols = slice(k * S5_K, (k + 1) * S5_K)
        y_ref[:, cols] = ys[:, cols] + y_in[k]


def _s5core(uc, params, layer, bsz):
    tmat, hmat, gmat, a2 = params
    rows = uc.shape[0]
    return pl.pallas_call(
        functools.partial(_s5core_kernel, bsz=bsz),
        grid=(S5_PAIRS,),
        in_specs=[
            pl.BlockSpec((rows, 2 * S5_K), lambda p: (0, p)),
            pl.BlockSpec((None, 1, 2, S5_K, S5_K), lambda p: (layer, p, 0, 0, 0)),
            pl.BlockSpec((None, 1, 2 * S5_K, 4 * S5_STATE), lambda p: (layer, p, 0, 0)),
            pl.BlockSpec((None, 1, 4 * S5_STATE, 2 * S5_K), lambda p: (layer, p, 0, 0)),
            pl.BlockSpec((None, 1, 2, 2 * S5_STATE), lambda p: (layer, p, 0, 0)),
        ],
        out_specs=pl.BlockSpec((rows, 2 * S5_K), lambda p: (0, p)),
        out_shape=jax.ShapeDtypeStruct((rows, S5_GROUPS * S5_K), F32),
        scratch_shapes=[
            pltpu.VMEM((rows, 4 * S5_STATE), F32),
            pltpu.VMEM((rows, 4 * S5_STATE), BF16),
        ],
        compiler_params=pltpu.CompilerParams(
            dimension_semantics=("arbitrary",), vmem_limit_bytes=VMEM_LIMIT),
        name="s5core",
    )(uc, tmat, hmat, gmat, a2)


def _atom_transpose(groups):
    lane = lax.broadcasted_iota(jnp.int32, groups[0][0].shape, 1)
    groups = [list(xs) for xs in groups]
    for s in range(3):
        d = 1 << s
        keep = ((lane >> (4 + s)) & 1) == 0
        for xs in groups:
            for i in range(8):
                if i & d:
                    continue
                lo, hi = xs[i], xs[i + d]
                xs[i] = jnp.where(keep, lo, pltpu.roll(hi, S5_GROUP * d, 1))
                xs[i + d] = jnp.where(keep, pltpu.roll(lo, LANES - S5_GROUP * d, 1), hi)
    return groups


def _to_chunks_kernel(u_ref, o_ref, rows_ref):
    bsz, tt, width = u_ref.shape
    pitch = rows_ref.shape[1] // bsz
    for v in range(width // LANES):
        for b in range(bsz):
            rows_ref[v, b * pitch:b * pitch + tt, :] = u_ref[b, :, v * LANES:(v + 1) * LANES]
    halves = [(v, n, k) for v in range(width // LANES) for n in range(tt // S5_R) for k in range(2)]
    groups = _atom_transpose(
        [[rows_ref[v, pl.ds(n * S5_R + 8 * k + j, bsz, stride=pitch), :] for j in range(8)]
         for v, n, k in halves])
    for (v, n, k), ys in zip(halves, groups):
        for g, y in enumerate(ys):
            lane0 = (8 * v + g) * S5_K + k * LANES
            o_ref[n * bsz:(n + 1) * bsz, lane0:lane0 + LANES] = y.astype(BF16)


def _to_chunks(u3, tt=176, width=512):
    bsz, lp, _ = u3.shape
    return pl.pallas_call(
        _to_chunks_kernel,
        grid=(lp // tt, D_S5 // width),
        in_specs=[pl.BlockSpec((bsz, tt, width), lambda i, v: (0, i, v))],
        out_specs=pl.BlockSpec((tt, width // S5_GROUP * S5_K), lambda i, v: (i, v)),
        out_shape=jax.ShapeDtypeStruct((lp // S5_R * bsz, S5_GROUPS * S5_K), BF16),
        scratch_shapes=[pltpu.VMEM((width // LANES, bsz * (tt + 8), LANES), F32)],
        compiler_params=pltpu.CompilerParams(
            dimension_semantics=("arbitrary", "arbitrary"), vmem_limit_bytes=VMEM_LIMIT),
        name="s5_to_chunks",
    )(u3)


def _s5glu_kernel(yc_ref, u_ref, d_ref, w_ref, b_ref, gain_ref, o_ref, yn_ref):
    bsz, tt, _ = u_ref.shape
    pitch = yn_ref.shape[1] // bsz
    for n in range(tt // S5_R):
        halves = [(v, k) for v in range(D_S5 // LANES) for k in range(2)]
        groups = _atom_transpose(
            [[yc_ref[n * bsz:(n + 1) * bsz,
                     (8 * v + g) * S5_K + k * LANES:(8 * v + g) * S5_K + (k + 1) * LANES]
              for g in range(8)] for v, k in halves])
        for (v, k), xs in zip(halves, groups):
            for j, x in enumerate(xs):
                yn_ref[v, pl.ds(n * S5_R + 8 * k + j, bsz, stride=pitch), :] = x
    y = jnp.concatenate(
        [jnp.concatenate([yn_ref[v, b * pitch:b * pitch + tt, :] for v in range(D_S5 // LANES)],
                         axis=1) for b in range(bsz)], axis=0)
    y = _gelu_tanh(y + d_ref[...] * u_ref[...].reshape(bsz * tt, D_S5))
    g = jnp.dot(y.astype(BF16), w_ref[...], preferred_element_type=F32) + b_ref[...]
    o_ref[...] = _rms(y * _sigmoid(g), gain_ref[...]).reshape(bsz, tt, D_S5).astype(BF16)


def _s5glu(yc, u3, d_skip, wglu_all, layer, bglu, gain, tt=48):
    bsz, lp, _ = u3.shape
    return pl.pallas_call(
        _s5glu_kernel,
        grid=(lp // tt,),
        in_specs=[
            pl.BlockSpec((tt // S5_R * bsz, S5_GROUPS * S5_K), lambda i: (i, 0)),
            pl.BlockSpec((bsz, tt, D_S5), lambda i: (0, i, 0)),
            pl.BlockSpec((1, D_S5), lambda i: (0, 0)),
            pl.BlockSpec((None, D_S5, D_S5), lambda i: (layer, 0, 0)),
            pl.BlockSpec((1, D_S5), lambda i: (0, 0)),
            pl.BlockSpec((1, D_S5), lambda i: (0, 0)),
        ],
        out_specs=pl.BlockSpec((bsz, tt, D_S5), lambda i: (0, i, 0)),
        out_shape=jax.ShapeDtypeStruct((bsz, lp, D_S5), BF16),
        scratch_shapes=[pltpu.VMEM((D_S5 // LANES, bsz * (tt + 8), LANES), F32)],
        compiler_params=pltpu.CompilerParams(
            dimension_semantics=("arbitrary",), vmem_limit_bytes=VMEM_LIMIT),
        name="s5glu",
    )(yc, u3, d_skip.reshape(1, D_S5), wglu_all, bglu.reshape(1, D_S5), gain.reshape(1, D_S5))


def _s5_params(lam_re, lam_im, log_step, b_re, b_im, c_re, c_im):
    hp = lax.Precision.HIGHEST
    a_re = jnp.minimum(lam_re.astype(F32), -1e-4)
    a_im = lam_im.astype(F32)
    dt = jnp.exp(log_step.astype(F32))[:, None]
    mag = jnp.exp(a_re * dt)
    ab_re = mag * jnp.cos(a_im * dt)
    ab_im = mag * jnp.sin(a_im * dt)
    den = a_re * a_re + a_im * a_im
    x_re, x_im = ab_re - 1.0, ab_im
    z_re = (x_re * a_re + x_im * a_im) / den
    z_im = (x_im * a_re - x_re * a_im) / den
    br, bi = b_re.astype(F32), b_im.astype(F32)
    bb_re = z_re[..., None] * br - z_im[..., None] * bi
    bb_im = z_re[..., None] * bi + z_im[..., None] * br
    cr, ci = c_re.astype(F32), c_im.astype(F32)

    pw_re, pw_im = [jnp.ones_like(ab_re)], [jnp.zeros_like(ab_re)]
    for _ in range(S5_R):
        r, i = pw_re[-1], pw_im[-1]
        pw_re.append(r * ab_re - i * ab_im)
        pw_im.append(r * ab_im + i * ab_re)
    p_re, p_im = jnp.stack(pw_re), jnp.stack(pw_im)

    q_re, q_im = p_re[:S5_R, :, :, None], p_im[:S5_R, :, :, None]
    m_re = q_re * bb_re - q_im * bb_im
    m_im = q_re * bb_im + q_im * bb_re
    kern = (jnp.einsum('gop,lgph->ghlo', cr, m_re, precision=hp)
            - jnp.einsum('gop,lgph->ghlo', ci, m_im, precision=hp))
    kz = jnp.concatenate([jnp.zeros((S5_GROUPS, S5_GROUP, S5_K), F32),
                          kern.reshape(S5_GROUPS, S5_GROUP, S5_K)], axis=-1)
    tmat = jnp.stack([kz[:, :, (S5_R - s) * S5_GROUP:(S5_R - s) * S5_GROUP + S5_K]
                      for s in range(S5_R)], axis=1).reshape(S5_PAIRS, 2, S5_K, S5_K)

    bt_re, bt_im = bb_re.transpose(0, 2, 1)[:, None], bb_im.transpose(0, 2, 1)[:, None]
    rp_re = jnp.moveaxis(p_re[S5_R - 1::-1], 0, 1)[:, :, None, :]
    rp_im = jnp.moveaxis(p_im[S5_R - 1::-1], 0, 1)[:, :, None, :]
    h_re = (rp_re * bt_re - rp_im * bt_im).reshape(S5_PAIRS, 2, S5_K, S5_STATE)
    h_im = (rp_re * bt_im + rp_im * bt_re).reshape(S5_PAIRS, 2, S5_K, S5_STATE)
    zh = jnp.zeros_like(h_re[:, 0])
    hmat = jnp.concatenate([
        jnp.concatenate([h_re[:, 0], zh, h_im[:, 0], zh], axis=-1),
        jnp.concatenate([zh, h_re[:, 1], zh, h_im[:, 1]], axis=-1)], axis=1)

    ct_re, ct_im = cr.transpose(0, 2, 1)[:, :, None, :], ci.transpose(0, 2, 1)[:, :, None, :]
    e_re = jnp.moveaxis(p_re[1:], 0, 2)[..., None]
    e_im = jnp.moveaxis(p_im[1:], 0, 2)[..., None]
    g_re = (ct_re * e_re - ct_im * e_im).reshape(S5_PAIRS, 2, S5_STATE, S5_K)
    g_im = -(ct_re * e_im + ct_im * e_re).reshape(S5_PAIRS, 2, S5_STATE, S5_K)
    zg = jnp.zeros_like(g_re[:, 0])
    gmat = jnp.concatenate([
        jnp.concatenate([g_re[:, 0], zg], axis=-1), jnp.concatenate([zg, g_re[:, 1]], axis=-1),
        jnp.concatenate([g_im[:, 0], zg], axis=-1), jnp.concatenate([zg, g_im[:, 1]], axis=-1)],
        axis=1)

    a2 = jnp.stack([p_re[S5_R].reshape(S5_PAIRS, 2 * S5_STATE),
                    p_im[S5_R].reshape(S5_PAIRS, 2 * S5_STATE)], axis=1)
    return tmat.astype(BF16), hmat.astype(BF16), gmat.astype(BF16), a2


def _s5(u3, params, layer, d_skip, wglu_all, bglu, gain):
    yc = _s5core(_to_chunks(u3), params, layer, u3.shape[0])
    return _s5glu(yc, u3, d_skip, wglu_all, layer, bglu, gain)


def _outproj_kernel(h_ref, pre_ref, a_ref, b_ref, w_ref, o_ref, *, first):
    y = (jnp.dot(a_ref[...], w_ref[0], preferred_element_type=F32)
         + jnp.dot(b_ref[...], w_ref[1], preferred_element_type=F32))
    if first:
        npre = pre_ref.shape[0]

        @pl.when(pl.program_id(1) == 0)
        def _():
            o_ref[0:npre, :] = y[0:npre, :] + pre_ref[...]
            o_ref[npre:, :] = y[npre:, :] + h_ref[0:h_ref.shape[0] - npre, :]

        @pl.when(pl.program_id(1) != 0)
        def _():
            o_ref[...] = y + h_ref[...]
    else:
        o_ref[...] = y + h_ref[...]


def _outproj(h, prefix, a, b, w_all, layer, first, bsz, tm=704):
    k = a.shape[1]
    d = h.shape[1]
    lp = a.shape[0] // bsz
    tiles = lp // tm
    flat = lambda b, j: (b * tiles + j, 0)
    return pl.pallas_call(
        functools.partial(_outproj_kernel, first=first),
        grid=(bsz, tiles),
        in_specs=[
            _row_tile_specs(first, bsz, lp, tm, d),
            pl.BlockSpec((CHUNK, d), lambda b, j: (0, 0)),
            pl.BlockSpec((tm, k), flat),
            pl.BlockSpec((tm, k), flat),
            pl.BlockSpec((None, 2, k, d), lambda b, j: (layer, 0, 0, 0)),
        ],
        out_specs=pl.BlockSpec((tm, d), flat),
        out_shape=jax.ShapeDtypeStruct((bsz * lp, d), F32),
        compiler_params=pltpu.CompilerParams(
            dimension_semantics=("arbitrary", "arbitrary"), vmem_limit_bytes=VMEM_LIMIT),
        name="outproj",
    )(h, prefix, a, b, w_all)


FFN_HALO = 16


def _ffn_kernel(h_ref, halo_ref, g_ref, wg_ref, wu_ref, cw_ref, cb_ref, wd_ref, fg_ref, o_ref,
                hn_ref, *, final):
    f_axis = 2 if final else 1
    f = pl.program_id(f_axis)

    @pl.when(f == 0)
    def _():
        x = h_ref[...]
        hn_ref[FFN_HALO:, :] = _rms(x, g_ref[...]).astype(BF16)
        hn_ref[0:FFN_HALO, :] = _rms(halo_ref[...], g_ref[...]).astype(BF16)
        o_ref[...] = x

    a = jnp.dot(hn_ref[...], wg_ref[...], preferred_element_type=F32)
    up = jnp.dot(hn_ref[FFN_HALO:, :], wu_ref[...], preferred_element_type=F32)
    cw = cw_ref[...]
    conv = (cb_ref[...]
            + cw[0:1, :] * pltpu.roll(a, 2, 0)[FFN_HALO:, :]
            + cw[1:2, :] * pltpu.roll(a, 1, 0)[FFN_HALO:, :]
            + cw[2:3, :] * a[FFN_HALO:, :])
    hid = (_silu(conv) * up).astype(BF16)
    o_ref[...] += jnp.dot(hid, wd_ref[...], preferred_element_type=F32)

    if final:
        @pl.when(f == pl.num_programs(f_axis) - 1)
        def _():
            o_ref[...] = _rms(o_ref[...], fg_ref[...])


def _ffn(h, g, layer, wg_all, wu_all, cw_all, cb_all, wd_all, fg, final, bsz, tm=512, tf=512):
    t, d = h.shape
    ff = wg_all.shape[2]
    hb = tm // FFN_HALO
    lp = t // bsz
    if final:
        grid = (bsz, (lp - CHUNK) // tm, ff // tf)
        row0 = lambda b, j, back: pl.multiple_of(b * lp + CHUNK + j * tm - back, FFN_HALO)
        row_specs = [
            pl.BlockSpec((pl.Element(tm), pl.Element(d)), lambda b, j, f: (row0(b, j, 0), 0)),
            pl.BlockSpec((pl.Element(FFN_HALO), pl.Element(d)),
                         lambda b, j, f: (row0(b, j, FFN_HALO), 0)),
        ]
        out_spec = pl.BlockSpec((tm, d), lambda b, j, f: (b * ((lp - CHUNK) // tm) + j, 0))
        out_rows = bsz * (lp - CHUNK)
        wmap = lambda fn: (lambda b, j, f: fn(f))
    else:
        grid = (t // tm, ff // tf)
        row_specs = [
            pl.BlockSpec((tm, d), lambda i, f: (i, 0)),
            pl.BlockSpec((FFN_HALO, d), lambda i, f: (jnp.maximum(i * hb - 1, 0), 0)),
        ]
        out_spec = pl.BlockSpec((tm, d), lambda i, f: (i, 0))
        out_rows = t
        wmap = lambda fn: (lambda i, f: fn(f))
    return pl.pallas_call(
        functools.partial(_ffn_kernel, final=final),
        grid=grid,
        in_specs=row_specs + [
            pl.BlockSpec((1, d), wmap(lambda f: (0, 0))),
            pl.BlockSpec((None, d, tf), wmap(lambda f: (layer, 0, f))),
            pl.BlockSpec((None, d, tf), wmap(lambda f: (layer, 0, f))),
            pl.BlockSpec((None, 3, tf), wmap(lambda f: (layer, 0, f))),
            pl.BlockSpec((None, 1, tf), wmap(lambda f: (layer, 0, f))),
            pl.BlockSpec((None, tf, d), wmap(lambda f: (layer, f, 0))),
            pl.BlockSpec((1, d), wmap(lambda f: (0, 0))),
        ],
        out_specs=out_spec,
        out_shape=jax.ShapeDtypeStruct((out_rows, d), F32),
        scratch_shapes=[pltpu.VMEM((tm + FFN_HALO, d), BF16)],
        compiler_params=pltpu.CompilerParams(
            dimension_semantics=("arbitrary",) * len(grid), vmem_limit_bytes=VMEM_LIMIT),
        name="ffn_final" if final else "ffn",
    )(h, h, g.reshape(1, d), wg_all, wu_all, cw_all, cb_all.reshape(cb_all.shape[0], 1, ff),
      wd_all, fg.reshape(1, d))


def kernel(x, meta_tokens, lb_logits, norm_mix, w_in, hg_norm, s5_lambda_re, s5_lambda_im,
           s5_log_step, s5_b_re, s5_b_im, s5_c_re, s5_c_im, s5_d, w_glu, b_glu, s5_norm, w_out,
           norm_ffn, w_ffn_gate, w_ffn_up, ffn_conv_w, ffn_conv_b, w_ffn_down, final_norm):
    bsz, seq, d = x.shape
    lp = seq + CHUNK
    prefix = jnp.concatenate([jnp.zeros((SEQ_PAD, d), x.dtype), meta_tokens.astype(x.dtype)], axis=0)
    h = x.reshape(bsz * seq, d)

    sm = jax.nn.softmax(lb_logits.astype(F32), axis=0)
    lb_all = jnp.cumsum(sm, axis=0) - sm[0:1]

    w_in_b = w_in.astype(BF16)
    w_out_b = w_out.astype(BF16).reshape(DEPTH, 2, D_HG, d)
    w_glu_b = w_glu.astype(BF16)
    wg_b, wu_b, wd_b = (w.astype(BF16) for w in (w_ffn_gate, w_ffn_up, w_ffn_down))
    s5_params = jax.vmap(_s5_params)(s5_lambda_re, s5_lambda_im, s5_log_step,
                                     s5_b_re, s5_b_im, s5_c_re, s5_c_im)

    for l in range(DEPTH):
        o_hg, u3 = _proj_hgrn2(h, prefix, norm_mix[l], w_in_b, l, lb_all[l], hg_norm[l],
                               first=(l == 0), bsz=bsz)
        o_s5 = _s5(u3, s5_params, l, s5_d[l], w_glu_b, b_glu[l], s5_norm[l])
        h = _outproj(h, prefix, o_hg.reshape(bsz * lp, D_HG), o_s5.reshape(bsz * lp, D_S5),
                     w_out_b, l, first=(l == 0), bsz=bsz)
        h = _ffn(h, norm_ffn[l], l, wg_b, wu_b, ffn_conv_w, ffn_conv_b, wd_b, final_norm,
                 final=(l == DEPTH - 1), bsz=bsz, tm=512 if l == DEPTH - 1 else 768)
    return h.reshape(bsz, seq, d)
```

```python
import functools

import jax
import jax.numpy as jnp
from jax import lax
from jax.experimental import pallas as pl
from jax.experimental.pallas import tpu as pltpu

F32 = jnp.float32
BF16 = jnp.bfloat16
LANES = 128

DEPTH = 2
CHUNK = 64
N_META = 16
D_HG = 1024
HG_HEADS = 8
HG_DK = 128
D_S5 = 1024
S5_GROUP = 16
S5_GROUPS = 64
S5_STATE = 64
D_IN = 4 * D_HG + D_S5
EPS = 1e-6
F_FLOOR = 1e-6

SEQ_PAD = CHUNK - N_META
VMEM_LIMIT = 56 * 1024 * 1024
VMEM_LIMIT_BIG = 62 * 1024 * 1024

S5_R = 16
S5_K = S5_R * S5_GROUP
S5_PAIRS = S5_GROUPS // 2


def _rms(x, g):
    ms = jnp.mean(x * x, axis=-1, keepdims=True)
    return x * lax.rsqrt(ms + EPS) * g


def _sigmoid(x):
    return 0.5 + 0.5 * jnp.tanh(0.5 * x)


def _silu(x):
    half = 0.5 * x
    return half + half * jnp.tanh(half)


def _normed_tile(h_ref, pre_ref, g_ref, xn_ref, first):
    if first:
        npre = pre_ref.shape[0]

        @pl.when(pl.program_id(1) == 0)
        def _():
            xn_ref[0:npre, :] = _rms(pre_ref[...], g_ref[...]).astype(BF16)
            xn_ref[npre:, :] = _rms(h_ref[0:h_ref.shape[0] - npre, :], g_ref[...]).astype(BF16)

        @pl.when(pl.program_id(1) != 0)
        def _():
            xn_ref[...] = _rms(h_ref[...], g_ref[...]).astype(BF16)
    else:
        xn_ref[...] = _rms(h_ref[...], g_ref[...]).astype(BF16)


def _row_tile_specs(first, bsz, lp, tm, d):
    tiles = lp // tm
    if first:
        def rows(b, j, *_):
            return (pl.multiple_of(b * (lp - CHUNK) + jnp.maximum(j * tm - CHUNK, 0), 8), 0)
        return pl.BlockSpec((pl.Element(tm), pl.Element(d)), rows)
    return pl.BlockSpec((tm, d), lambda b, j, *_: (b * tiles + j, 0))


def _pick_rows(x, m, first):
    j = lax.broadcasted_iota(jnp.int32, x.shape, 0)
    out = None
    for blk in range(8 // m - 1, -1, -1):
        r = blk * m + (0 if first else m - 1)
        row = jnp.broadcast_to(x[r:r + 1, :], x.shape)
        out = row if out is None else jnp.where(j < (blk + 1) * m, row, out)
    return out


def _expand_rows(ref, i, x):
    nv = x.shape[1] // LANES
    for v in range(nv):
        ref[i, v] = x[:, v * LANES:(v + 1) * LANES]
    return jnp.concatenate(
        [jnp.concatenate([ref[i, v, pl.ds(j, 8, stride=0), :] for v in range(nv)], axis=1)
         for j in range(8)], axis=0)


def _dot_nt(a, b):
    return lax.dot_general(a, b, (((1,), (1,)), ((), ())), preferred_element_type=F32)


def _dot_tn(a, b):
    return lax.dot_general(a, b, (((0,), (0,)), ((), ())), preferred_element_type=F32)


def _proj_hgrn2_kernel(h_ref, pre_ref, g_ref, w_ref, lb_ref, gain_ref, o_ref, u_ref,
                       xn_ref, p_ref, st_ref, bc_ref, *, first):
    @pl.when(pl.program_id(1) == 0)
    def _():
        st_ref[...] = jnp.zeros_like(st_ref)

    _normed_tile(h_ref, pre_ref, g_ref, xn_ref, first)
    xn = xn_ref[...]
    for k in range(4 * D_HG // D_S5):
        cols = slice(k * D_S5, (k + 1) * D_S5)
        p_ref[:, cols] = jnp.dot(xn, w_ref[:, cols], preferred_element_type=F32)

    nsel = CHUNK + 16
    row = lax.broadcasted_iota(jnp.int32, (nsel, 3 * CHUNK), 0)
    col = lax.broadcasted_iota(jnp.int32, (nsel, 3 * CHUNK), 1) & (CHUNK - 1)
    tgt = jnp.where(row < CHUNK, row,
                    jnp.where(row < CHUNK + 8, (row - CHUNK) * 8 + 7, (row - CHUNK - 8) * 8 + 3))
    tri3 = (tgt >= col).astype(BF16)
    ti = lax.broadcasted_iota(jnp.int32, (CHUNK, CHUNK), 0)
    si = lax.broadcasted_iota(jnp.int32, (CHUNK, CHUNK), 1)
    masks = {}
    for w, sh in ((32, 5), (16, 4), (8, 3)):
        masks[w] = ((ti >> sh) == (si >> sh) + 1) & ((ti >> (sh + 1)) == (si >> (sh + 1)))
    masks[0] = ((ti >> 3) == (si >> 3)) & (si <= ti)

    u_blocks = 4
    u_cols = D_S5 // u_blocks
    for c in range(p_ref.shape[0] // CHUNK):
        if c < u_blocks:
            cols = slice(c * u_cols, (c + 1) * u_cols)
            u_ref[:, cols] = jnp.dot(xn, w_ref[:, 4 * D_HG + c * u_cols:4 * D_HG + (c + 1) * u_cols],
                                     preferred_element_type=F32)
        _hgrn2_chunk(p_ref, lb_ref, gain_ref, o_ref, st_ref, bc_ref, tri3, masks,
                     pl.ds(c * CHUNK, CHUNK))


def _hgrn2_chunk(p_ref, lb_ref, gain_ref, o_ref, st_ref, bc_ref, tri3, masks, rows):
    q = p_ref[rows, 0:D_HG]
    z = p_ref[rows, D_HG:2 * D_HG]
    v = p_ref[rows, 2 * D_HG:3 * D_HG].astype(BF16)
    lb = lb_ref[...]

    th = 0.5 * jnp.tanh(0.5 * z)
    gl = jnp.log2(jnp.maximum(lb + (1.0 - lb) * (0.5 + th), F_FLOOR))
    kk = (1.0 - lb) * (0.5 - th)
    qf = _silu(q)

    hi = gl.astype(BF16)
    r1 = gl - hi.astype(F32)
    mid = r1.astype(BF16)
    lo = (r1 - mid.astype(F32)).astype(BF16)
    cum = jnp.dot(tri3, jnp.concatenate([hi, mid, lo], axis=0), preferred_element_type=F32)
    b = cum[0:CHUNK]
    l8 = cum[CHUNK:CHUNK + 8]
    m8 = cum[CHUNK + 8:CHUNK + 16]
    j8 = lax.broadcasted_iota(jnp.int32, l8.shape, 0)
    s8 = jnp.where(j8 == 0, 0.0, pltpu.roll(l8, 1, 0))

    qe = qf * jnp.exp2(b - _expand_rows(bc_ref, 0, s8))
    kf = kk * jnp.exp2(_expand_rows(bc_ref, 1, l8) - b)
    qh = {8: qe.astype(BF16)}
    kh = {8: kf.astype(BF16)}
    for lvl, (w, m) in enumerate(((16, 2), (32, 4), (64, 8))):
        cq = jnp.exp2(s8 - _pick_rows(s8, m, True))
        ck = jnp.exp2(_pick_rows(l8, m, False) - l8)
        qh[w] = (qe * _expand_rows(bc_ref, 3 + 2 * lvl, cq)).astype(BF16)
        kh[w] = (kf * _expand_rows(bc_ref, 4 + 2 * lvl, ck)).astype(BF16)
    b_mid = _expand_rows(bc_ref, 2, m8)
    qh[0] = (qf * jnp.exp2(b - b_mid)).astype(BF16)
    kh[0] = (kk * jnp.exp2(b_mid - b)).astype(BF16)
    dec_all = jnp.exp2(l8[7:8, :])

    gate = p_ref[rows, 3 * D_HG:4 * D_HG]
    gate = _silu(gate)
    gain = gain_ref[...]

    heads = [slice(h * HG_DK, (h + 1) * HG_DK) for h in range(HG_HEADS)]
    sts = [st_ref[h] for h in range(HG_HEADS)]
    o_inter = [_dot_nt(qh[64][:, sl], st.astype(BF16)) for sl, st in zip(heads, sts)]
    scores = []
    for sl in heads:
        sc = jnp.zeros((CHUNK, CHUNK), F32)
        for w in (32, 16, 8, 0):
            sc = jnp.where(masks[w], _dot_nt(qh[w][:, sl], kh[w][:, sl]), sc)
        scores.append(sc.astype(BF16))
    outs = [oi + jnp.dot(sc, v[:, sl], preferred_element_type=F32)
            for oi, sc, sl in zip(o_inter, scores, heads)]
    for h, (sl, st) in enumerate(zip(heads, sts)):
        st_ref[h] = dec_all[:, sl] * st + _dot_tn(v[:, sl], kh[64][:, sl])
    for o, sl in zip(outs, heads):
        ms = jnp.mean(o * o, axis=-1, keepdims=True)
        o = o * lax.rsqrt(ms + EPS) * gain[:, sl]
        o_ref[rows, sl] = (o * gate[:, sl]).astype(BF16)


def _proj_hgrn2(h, prefix, g, w_all, layer, lb, gain, first, bsz, chunks_per_step=11):
    d = h.shape[1]
    lp = h.shape[0] // bsz + (CHUNK if first else 0)
    rows = chunks_per_step * CHUNK
    const = lambda b, c: (0, 0)
    return pl.pallas_call(
        functools.partial(_proj_hgrn2_kernel, first=first),
        grid=(bsz, lp // rows),
        in_specs=[
            _row_tile_specs(first, bsz, lp, rows, d),
            pl.BlockSpec((CHUNK, d), const),
            pl.BlockSpec((1, d), const),
            pl.BlockSpec((None, d, D_IN), lambda b, c: (layer, 0, 0),
                         pipeline_mode=pl.Buffered(1)),
            pl.BlockSpec((1, D_HG), const),
            pl.BlockSpec((1, D_HG), const),
        ],
        out_specs=[pl.BlockSpec((None, rows, D_HG), lambda b, c: (b, c, 0)),
                   pl.BlockSpec((None, rows, D_S5), lambda b, c: (b, c, 0))],
        out_shape=[jax.ShapeDtypeStruct((bsz, lp, D_HG), BF16),
                   jax.ShapeDtypeStruct((bsz, lp, D_S5), F32)],
        scratch_shapes=[
            pltpu.VMEM((rows, d), BF16),
            pltpu.VMEM((rows, 4 * D_HG), F32),
            pltpu.VMEM((HG_HEADS, HG_DK, HG_DK), F32),
            pltpu.VMEM((9, D_HG // LANES, 8, LANES), F32),
        ],
        compiler_params=pltpu.CompilerParams(
            dimension_semantics=("arbitrary", "arbitrary"), vmem_limit_bytes=VMEM_LIMIT_BIG),
        name="proj_hgrn2",
    )(h, prefix, g.reshape(1, d), w_all, lb.reshape(1, D_HG), gain.reshape(1, D_HG))


def _gelu_tanh(x):
    c = 0.7978845608028654
    half = 0.5 * x
    return half + half * jnp.tanh(x * (c + (c * 0.044715) * (x * x)))


def _s5core_kernel(u_ref, t_ref, h_ref, g_ref, a_ref, y_ref, v_ref, xp_ref, *, bsz):
    nchunks = u_ref.shape[0] // bsz
    u = u_ref[...]
    v_ref[...] = jnp.dot(u, h_ref[0], preferred_element_type=F32)
    a_re = jnp.broadcast_to(a_ref[0, 0:1, :], (bsz, LANES))
    a_im = jnp.broadcast_to(a_ref[0, 1:2, :], (bsz, LANES))

    y_in = [jnp.dot(u[:, k * S5_K:(k + 1) * S5_K], t_ref[0, k], preferred_element_type=F32)
            for k in range(2)]

    x_re = x_im = jnp.zeros((bsz, LANES), F32)
    for n in range(nchunks):
        rows = slice(n * bsz, (n + 1) * bsz)
        xp_ref[rows, 0:LANES] = x_re.astype(BF16)
        xp_ref[rows, LANES:2 * LANES] = x_im.astype(BF16)
        x_re, x_im = (a_re * x_re - a_im * x_im + v_ref[rows, 0:LANES],
                      a_re * x_im + a_im * x_re + v_ref[rows, LANES:2 * LANES])

    ys = jnp.dot(xp_ref[...], g_ref[0], preferred_element_type=F32)
    for k in range(2):
        cols = slice(k * S5_K, (k + 1) * S5_K)
        y_ref[:, cols] = ys[:, cols] + y_in[k]


def _s5core(uc, params, layer, bsz):
    tmat, hmat, gmat, a2 = params
    rows = uc.shape[0]
    return pl.pallas_call(
        functools.partial(_s5core_kernel, bsz=bsz),
        grid=(S5_PAIRS,),
        in_specs=[
            pl.BlockSpec((rows, 2 * S5_K), lambda p: (0, p)),
            pl.BlockSpec((None, 1, 2, S5_K, S5_K), lambda p: (layer, p, 0, 0, 0)),
            pl.BlockSpec((None, 1, 2 * S5_K, 4 * S5_STATE), lambda p: (layer, p, 0, 0)),
            pl.BlockSpec((None, 1, 4 * S5_STATE, 2 * S5_K), lambda p: (layer, p, 0, 0)),
            pl.BlockSpec((None, 1, 2, 2 * S5_STATE), lambda p: (layer, p, 0, 0)),
        ],
        out_specs=pl.BlockSpec((rows, 2 * S5_K), lambda p: (0, p)),
        out_shape=jax.ShapeDtypeStruct((rows, S5_GROUPS * S5_K), F32),
        scratch_shapes=[
            pltpu.VMEM((rows, 4 * S5_STATE), F32),
            pltpu.VMEM((rows, 4 * S5_STATE), BF16),
        ],
        compiler_params=pltpu.CompilerParams(
            dimension_semantics=("arbitrary",), vmem_limit_bytes=VMEM_LIMIT),
        name="s5core",
    )(uc, tmat, hmat, gmat, a2)


def _atom_transpose(groups):
    lane = lax.broadcasted_iota(jnp.int32, groups[0][0].shape, 1)
    groups = [list(xs) for xs in groups]
    for s in range(3):
        d = 1 << s
        keep = ((lane >> (4 + s)) & 1) == 0
        for xs in groups:
            for i in range(8):
                if i & d:
                    continue
                lo, hi = xs[i], xs[i + d]
                xs[i] = jnp.where(keep, lo, pltpu.roll(hi, S5_GROUP * d, 1))
                xs[i + d] = jnp.where(keep, pltpu.roll(lo, LANES - S5_GROUP * d, 1), hi)
    return groups


def _to_chunks_kernel(u_ref, o_ref, rows_ref):
    bsz, tt, width = u_ref.shape
    pitch = rows_ref.shape[1] // bsz
    for v in range(width // LANES):
        for b in range(bsz):
            rows_ref[v, b * pitch:b * pitch + tt, :] = u_ref[b, :, v * LANES:(v + 1) * LANES]
    halves = [(v, n, k) for v in range(width // LANES) for n in range(tt // S5_R) for k in range(2)]
    groups = _atom_transpose(
        [[rows_ref[v, pl.ds(n * S5_R + 8 * k + j, bsz, stride=pitch), :] for j in range(8)]
         for v, n, k in halves])
    for (v, n, k), ys in zip(halves, groups):
        for g, y in enumerate(ys):
            lane0 = (8 * v + g) * S5_K + k * LANES
            o_ref[n * bsz:(n + 1) * bsz, lane0:lane0 + LANES] = y.astype(BF16)


def _to_chunks(u3, tt=176, width=512):
    bsz, lp, _ = u3.shape
    return pl.pallas_call(
        _to_chunks_kernel,
        grid=(lp // tt, D_S5 // width),
        in_specs=[pl.BlockSpec((bsz, tt, width), lambda i, v: (0, i, v))],
        out_specs=pl.BlockSpec((tt, width // S5_GROUP * S5_K), lambda i, v: (i, v)),
        out_shape=jax.ShapeDtypeStruct((lp // S5_R * bsz, S5_GROUPS * S5_K), BF16),
        scratch_shapes=[pltpu.VMEM((width // LANES, bsz * (tt + 8), LANES), F32)],
        compiler_params=pltpu.CompilerParams(
            dimension_semantics=("arbitrary", "arbitrary"), vmem_limit_bytes=VMEM_LIMIT),
        name="s5_to_chunks",
    )(u3)


def _s5glu_kernel(yc_ref, u_ref, d_ref, w_ref, b_ref, gain_ref, o_ref, yn_ref):
    bsz, tt, _ = u_ref.shape
    pitch = yn_ref.shape[1] // bsz
    for n in range(tt // S5_R):
        halves = [(v, k) for v in range(D_S5 // LANES) for k in range(2)]
        groups = _atom_transpose(
            [[yc_ref[n * bsz:(n + 1) * bsz,
                     (8 * v + g) * S5_K + k * LANES:(8 * v + g) * S5_K + (k + 1) * LANES]
              for g in range(8)] for v, k in halves])
        for (v, k), xs in zip(halves, groups):
            for j, x in enumerate(xs):
                yn_ref[v, pl.ds(n * S5_R + 8 * k + j, bsz, stride=pitch), :] = x
    y = jnp.concatenate(
        [jnp.concatenate([yn_ref[v, b * pitch:b * pitch + tt, :] for v in range(D_S5 // LANES)],
                         axis=1) for b in range(bsz)], axis=0)
    y = _gelu_tanh(y + d_ref[...] * u_ref[...].reshape(bsz * tt, D_S5))
    g = jnp.dot(y.astype(BF16), w_ref[...], preferred_element_type=F32) + b_ref[...]
    o_ref[...] = _rms(y * _sigmoid(g), gain_ref[...]).reshape(bsz, tt, D_S5).astype(BF16)


def _s5glu(yc, u3, d_skip, wglu_all, layer, bglu, gain, tt=48):
    bsz, lp, _ = u3.shape
    return pl.pallas_call(
        _s5glu_kernel,
        grid=(lp // tt,),
        in_specs=[
            pl.BlockSpec((tt // S5_R * bsz, S5_GROUPS * S5_K), lambda i: (i, 0)),
            pl.BlockSpec((bsz, tt, D_S5), lambda i: (0, i, 0)),
            pl.BlockSpec((1, D_S5), lambda i: (0, 0)),
            pl.BlockSpec((None, D_S5, D_S5), lambda i: (layer, 0, 0)),
            pl.BlockSpec((1, D_S5), lambda i: (0, 0)),
            pl.BlockSpec((1, D_S5), lambda i: (0, 0)),
        ],
        out_specs=pl.BlockSpec((bsz, tt, D_S5), lambda i: (0, i, 0)),
        out_shape=jax.ShapeDtypeStruct((bsz, lp, D_S5), BF16),
        scratch_shapes=[pltpu.VMEM((D_S5 // LANES, bsz * (tt + 8), LANES), F32)],
        compiler_params=pltpu.CompilerParams(
            dimension_semantics=("arbitrary",), vmem_limit_bytes=VMEM_LIMIT),
        name="s5glu",
    )(yc, u3, d_skip.reshape(1, D_S5), wglu_all, bglu.reshape(1, D_S5), gain.reshape(1, D_S5))


def _s5_params(lam_re, lam_im, log_step, b_re, b_im, c_re, c_im):
    hp = lax.Precision.HIGHEST
    a_re = jnp.minimum(lam_re.astype(F32), -1e-4)
    a_im = lam_im.astype(F32)
    dt = jnp.exp(log_step.astype(F32))[:, None]
    mag = jnp.exp(a_re * dt)
    ab_re = mag * jnp.cos(a_im * dt)
    ab_im = mag * jnp.sin(a_im * dt)
    den = a_re * a_re + a_im * a_im
    x_re, x_im = ab_re - 1.0, ab_im
    z_re = (x_re * a_re + x_im * a_im) / den
    z_im = (x_im * a_re - x_re * a_im) / den
    br, bi = b_re.astype(F32), b_im.astype(F32)
    bb_re = z_re[..., None] * br - z_im[..., None] * bi
    bb_im = z_re[..., None] * bi + z_im[..., None] * br
    cr, ci = c_re.astype(F32), c_im.astype(F32)

    pw_re, pw_im = [jnp.ones_like(ab_re)], [jnp.zeros_like(ab_re)]
    for _ in range(S5_R):
        r, i = pw_re[-1], pw_im[-1]
        pw_re.append(r * ab_re - i * ab_im)
        pw_im.append(r * ab_im + i * ab_re)
    p_re, p_im = jnp.stack(pw_re), jnp.stack(pw_im)

    q_re, q_im = p_re[:S5_R, :, :, None], p_im[:S5_R, :, :, None]
    m_re = q_re * bb_re - q_im * bb_im
    m_im = q_re * bb_im + q_im * bb_re
    kern = (jnp.einsum('gop,lgph->ghlo', cr, m_re, precision=hp)
            - jnp.einsum('gop,lgph->ghlo', ci, m_im, precision=hp))
    kz = jnp.concatenate([jnp.zeros((S5_GROUPS, S5_GROUP, S5_K), F32),
                          kern.reshape(S5_GROUPS, S5_GROUP, S5_K)], axis=-1)
    tmat = jnp.stack([kz[:, :, (S5_R - s) * S5_GROUP:(S5_R - s) * S5_GROUP + S5_K]
                      for s in range(S5_R)], axis=1).reshape(S5_PAIRS, 2, S5_K, S5_K)

    bt_re, bt_im = bb_re.transpose(0, 2, 1)[:, None], bb_im.transpose(0, 2, 1)[:, None]
    rp_re = jnp.moveaxis(p_re[S5_R - 1::-1], 0, 1)[:, :, None, :]
    rp_im = jnp.moveaxis(p_im[S5_R - 1::-1], 0, 1)[:, :, None, :]
    h_re = (rp_re * bt_re - rp_im * bt_im).reshape(S5_PAIRS, 2, S5_K, S5_STATE)
    h_im = (rp_re * bt_im + rp_im * bt_re).reshape(S5_PAIRS, 2, S5_K, S5_STATE)
    zh = jnp.zeros_like(h_re[:, 0])
    hmat = jnp.concatenate([
        jnp.concatenate([h_re[:, 0], zh, h_im[:, 0], zh], axis=-1),
        jnp.concatenate([zh, h_re[:, 1], zh, h_im[:, 1]], axis=-1)], axis=1)

    ct_re, ct_im = cr.transpose(0, 2, 1)[:, :, None, :], ci.transpose(0, 2, 1)[:, :, None, :]
    e_re = jnp.moveaxis(p_re[1:], 0, 2)[..., None]
    e_im = jnp.moveaxis(p_im[1:], 0, 2)[..., None]
    g_re = (ct_re * e_re - ct_im * e_im).reshape(S5_PAIRS, 2, S5_STATE, S5_K)
    g_im = -(ct_re * e_im + ct_im * e_re).reshape(S5_PAIRS, 2, S5_STATE, S5_K)
    zg = jnp.zeros_like(g_re[:, 0])
    gmat = jnp.concatenate([
        jnp.concatenate([g_re[:, 0], zg], axis=-1), jnp.concatenate([zg, g_re[:, 1]], axis=-1),
        jnp.concatenate([g_im[:, 0], zg], axis=-1), jnp.concatenate([zg, g_im[:, 1]], axis=-1)],
        axis=1)

    a2 = jnp.stack([p_re[S5_R].reshape(S5_PAIRS, 2 * S5_STATE),
                    p_im[S5_R].reshape(S5_PAIRS, 2 * S5_STATE)], axis=1)
    return tmat.astype(BF16), hmat.astype(BF16), gmat.astype(BF16), a2


def _s5(u3, params, layer, d_skip, wglu_all, bglu, gain):
    yc = _s5core(_to_chunks(u3), params, layer, u3.shape[0])
    return _s5glu(yc, u3, d_skip, wglu_all, layer, bglu, gain)


def _outproj_kernel(h_ref, pre_ref, a_ref, b_ref, w_ref, o_ref, *, first):
    y = (jnp.dot(a_ref[...], w_ref[0], preferred_element_type=F32)
         + jnp.dot(b_ref[...], w_ref[1], preferred_element_type=F32))
    if first:
        npre = pre_ref.shape[0]

        @pl.when(pl.program_id(1) == 0)
        def _():
            o_ref[0:npre, :] = y[0:npre, :] + pre_ref[...]
            o_ref[npre:, :] = y[npre:, :] + h_ref[0:h_ref.shape[0] - npre, :]

        @pl.when(pl.program_id(1) != 0)
        def _():
            o_ref[...] = y + h_ref[...]
    else:
        o_ref[...] = y + h_ref[...]


def _outproj(h, prefix, a, b, w_all, layer, first, bsz, tm=704):
    k = a.shape[1]
    d = h.shape[1]
    lp = a.shape[0] // bsz
    tiles = lp // tm
    flat = lambda b, j: (b * tiles + j, 0)
    return pl.pallas_call(
        functools.partial(_outproj_kernel, first=first),
        grid=(bsz, tiles),
        in_specs=[
            _row_tile_specs(first, bsz, lp, tm, d),
            pl.BlockSpec((CHUNK, d), lambda b, j: (0, 0)),
            pl.BlockSpec((tm, k), flat),
            pl.BlockSpec((tm, k), flat),
            pl.BlockSpec((None, 2, k, d), lambda b, j: (layer, 0, 0, 0)),
        ],
        out_specs=pl.BlockSpec((tm, d), flat),
        out_shape=jax.ShapeDtypeStruct((bsz * lp, d), F32),
        compiler_params=pltpu.CompilerParams(
            dimension_semantics=("arbitrary", "arbitrary"), vmem_limit_bytes=VMEM_LIMIT),
        name="outproj",
    )(h, prefix, a, b, w_all)


FFN_HALO = 16


def _ffn_kernel(h_ref, halo_ref, g_ref, wg_ref, wu_ref, cw_ref, cb_ref, wd_ref, fg_ref, o_ref,
                hn_ref, *, final):
    f_axis = 2 if final else 1
    f = pl.program_id(f_axis)

    @pl.when(f == 0)
    def _():
        x = h_ref[...]
        hn_ref[FFN_HALO:, :] = _rms(x, g_ref[...]).astype(BF16)
        hn_ref[0:FFN_HALO, :] = _rms(halo_ref[...], g_ref[...]).astype(BF16)
        o_ref[...] = x

    a = jnp.dot(hn_ref[...], wg_ref[...], preferred_element_type=F32)
    up = jnp.dot(hn_ref[FFN_HALO:, :], wu_ref[...], preferred_element_type=F32)
    cw = cw_ref[...]
    conv = (cb_ref[...]
            + cw[0:1, :] * pltpu.roll(a, 2, 0)[FFN_HALO:, :]
            + cw[1:2, :] * pltpu.roll(a, 1, 0)[FFN_HALO:, :]
            + cw[2:3, :] * a[FFN_HALO:, :])
    hid = (_silu(conv) * up).astype(BF16)
    o_ref[...] += jnp.dot(hid, wd_ref[...], preferred_element_type=F32)

    if final:
        @pl.when(f == pl.num_programs(f_axis) - 1)
        def _():
            o_ref[...] = _rms(o_ref[...], fg_ref[...])


def _ffn(h, g, layer, wg_all, wu_all, cw_all, cb_all, wd_all, fg, final, bsz, tm=512, tf=512):
    t, d = h.shape
    ff = wg_all.shape[2]
    hb = tm // FFN_HALO
    lp = t // bsz
    if final:
        grid = (bsz, (lp - CHUNK) // tm, ff // tf)
        row0 = lambda b, j, back: pl.multiple_of(b * lp + CHUNK + j * tm - back, FFN_HALO)
        row_specs = [
            pl.BlockSpec((pl.Element(tm), pl.Element(d)), lambda b, j, f: (row0(b, j, 0), 0)),
            pl.BlockSpec((pl.Element(FFN_HALO), pl.Element(d)),
                         lambda b, j, f: (row0(b, j, FFN_HALO), 0)),
        ]
        out_spec = pl.BlockSpec((tm, d), lambda b, j, f: (b * ((lp - CHUNK) // tm) + j, 0))
        out_rows = bsz * (lp - CHUNK)
        wmap = lambda fn: (lambda b, j, f: fn(f))
    else:
        grid = (t // tm, ff // tf)
        row_specs = [
            pl.BlockSpec((tm, d), lambda i, f: (i, 0)),
            pl.BlockSpec((FFN_HALO, d), lambda i, f: (jnp.maximum(i * hb - 1, 0), 0)),
        ]
        out_spec = pl.BlockSpec((tm, d), lambda i, f: (i, 0))
        out_rows = t
        wmap = lambda fn: (lambda i, f: fn(f))
    return pl.pallas_call(
        functools.partial(_ffn_kernel, final=final),
        grid=grid,
        in_specs=row_specs + [
            pl.BlockSpec((1, d), wmap(lambda f: (0, 0))),
            pl.BlockSpec((None, d, tf), wmap(lambda f: (layer, 0, f))),
            pl.BlockSpec((None, d, tf), wmap(lambda f: (layer, 0, f))),
            pl.BlockSpec((None, 3, tf), wmap(lambda f: (layer, 0, f))),
            pl.BlockSpec((None, 1, tf), wmap(lambda f: (layer, 0, f))),
            pl.BlockSpec((None, tf, d), wmap(lambda f: (layer, f, 0))),
            pl.BlockSpec((1, d), wmap(lambda f: (0, 0))),
        ],
        out_specs=out_spec,
        out_shape=jax.ShapeDtypeStruct((out_rows, d), F32),
        scratch_shapes=[pltpu.VMEM((tm + FFN_HALO, d), BF16)],
        compiler_params=pltpu.CompilerParams(
            dimension_semantics=("arbitrary",) * len(grid), vmem_limit_bytes=VMEM_LIMIT),
        name="ffn_final" if final else "ffn",
    )(h, h, g.reshape(1, d), wg_all, wu_all, cw_all, cb_all.reshape(cb_all.shape[0], 1, ff),
      wd_all, fg.reshape(1, d))


def kernel(x, meta_tokens, lb_logits, norm_mix, w_in, hg_norm, s5_lambda_re, s5_lambda_im,
           s5_log_step, s5_b_re, s5_b_im, s5_c_re, s5_c_im, s5_d, w_glu, b_glu, s5_norm, w_out,
           norm_ffn, w_ffn_gate, w_ffn_up, ffn_conv_w, ffn_conv_b, w_ffn_down, final_norm):
    bsz, seq, d = x.shape
    lp = seq + CHUNK
    prefix = jnp.concatenate([jnp.zeros((SEQ_PAD, d), x.dtype), meta_tokens.astype(x.dtype)], axis=0)
    h = x.reshape(bsz * seq, d)

    sm = jax.nn.softmax(lb_logits.astype(F32), axis=0)
    lb_all = jnp.cumsum(sm, axis=0) - sm[0:1]

    w_in_b = w_in.astype(BF16)
    w_out_b = w_out.astype(BF16).reshape(DEPTH, 2, D_HG, d)
    w_glu_b = w_glu.astype(BF16)
    wg_b, wu_b, wd_b = (w.astype(BF16) for w in (w_ffn_gate, w_ffn_up, w_ffn_down))
    s5_params = jax.vmap(_s5_params)(s5_lambda_re, s5_lambda_im, s5_log_step,
                                     s5_b_re, s5_b_im, s5_c_re, s5_c_im)

    for l in range(DEPTH):
        o_hg, u3 = _proj_hgrn2(h, prefix, norm_mix[l], w_in_b, l, lb_all[l], hg_norm[l],
                               first=(l == 0), bsz=bsz)
        o_s5 = _s5(u3, s5_params, l, s5_d[l], w_glu_b, b_glu[l], s5_norm[l])
        h = _outproj(h, prefix, o_hg.reshape(bsz * lp, D_HG), o_s5.reshape(bsz * lp, D_S5),
                     w_out_b, l, first=(l == 0), bsz=bsz)
        h = _ffn(h, norm_ffn[l], l, wg_b, wu_b, ffn_conv_w, ffn_conv_b, wd_b, final_norm,
                 final=(l == DEPTH - 1), bsz=bsz, tm=512 if l == DEPTH - 1 else 768)
    return h.reshape(bsz, seq, d)
```

```python
import functools

import jax
import jax.numpy as jnp
from jax import lax
from jax.experimental import pallas as pl
from jax.experimental.pallas import tpu as pltpu

F32 = jnp.float32
BF16 = jnp.bfloat16
LANES = 128

DEPTH = 2
CHUNK = 64
N_META = 16
D_HG = 1024
HG_HEADS = 8
HG_DK = 128
D_S5 = 1024
S5_GROUP = 16
S5_GROUPS = 64
S5_STATE = 64
D_IN = 4 * D_HG + D_S5
EPS = 1e-6
F_FLOOR = 1e-6

SEQ_PAD = CHUNK - N_META
VMEM_LIMIT = 56 * 1024 * 1024
VMEM_LIMIT_BIG = 62 * 1024 * 1024

S5_R = 16
S5_K = S5_R * S5_GROUP
S5_PAIRS = S5_GROUPS // 2


def _rms(x, g):
    ms = jnp.mean(x * x, axis=-1, keepdims=True)
    return x * lax.rsqrt(ms + EPS) * g


def _sigmoid(x):
    return 0.5 + 0.5 * jnp.tanh(0.5 * x)


def _silu(x):
    half = 0.5 * x
    return half + half * jnp.tanh(half)


def _normed_tile(h_ref, pre_ref, g_ref, xn_ref, first):
    if first:
        npre = pre_ref.shape[0]

        @pl.when(pl.program_id(1) == 0)
        def _():
            xn_ref[0:npre, :] = _rms(pre_ref[...], g_ref[...]).astype(BF16)
            xn_ref[npre:, :] = _rms(h_ref[0:h_ref.shape[0] - npre, :], g_ref[...]).astype(BF16)

        @pl.when(pl.program_id(1) != 0)
        def _():
            xn_ref[...] = _rms(h_ref[...], g_ref[...]).astype(BF16)
    else:
        xn_ref[...] = _rms(h_ref[...], g_ref[...]).astype(BF16)


def _row_tile_specs(first, bsz, lp, tm, d):
    tiles = lp // tm
    if first:
        def rows(b, j, *_):
            return (pl.multiple_of(b * (lp - CHUNK) + jnp.maximum(j * tm - CHUNK, 0), 8), 0)
        return pl.BlockSpec((pl.Element(tm), pl.Element(d)), rows)
    return pl.BlockSpec((tm, d), lambda b, j, *_: (b * tiles + j, 0))


def _pick_rows(x, m, first):
    j = lax.broadcasted_iota(jnp.int32, x.shape, 0)
    out = None
    for blk in range(8 // m - 1, -1, -1):
        r = blk * m + (0 if first else m - 1)
        row = jnp.broadcast_to(x[r:r + 1, :], x.shape)
        out = row if out is None else jnp.where(j < (blk + 1) * m, row, out)
    return out


def _expand_rows(ref, i, x):
    nv = x.shape[1] // LANES
    for v in range(nv):
        ref[i, v] = x[:, v * LANES:(v + 1) * LANES]
    return jnp.concatenate(
        [jnp.concatenate([ref[i, v, pl.ds(j, 8, stride=0), :] for v in range(nv)], axis=1)
         for j in range(8)], axis=0)


def _dot_nt(a, b):
    return lax.dot_general(a, b, (((1,), (1,)), ((), ())), preferred_element_type=F32)


def _dot_tn(a, b):
    return lax.dot_general(a, b, (((0,), (0,)), ((), ())), preferred_element_type=F32)


def _proj_hgrn2_kernel(h_ref, pre_ref, g_ref, w_ref, lb_ref, gain_ref, o_ref, u_ref,
                       xn_ref, p_ref, st_ref, bc_ref, *, first):
    @pl.when(pl.program_id(1) == 0)
    def _():
        st_ref[...] = jnp.zeros_like(st_ref)

    _normed_tile(h_ref, pre_ref, g_ref, xn_ref, first)
    xn = xn_ref[...]
    for k in range(4 * D_HG // D_S5):
        cols = slice(k * D_S5, (k + 1) * D_S5)
        p_ref[:, cols] = jnp.dot(xn, w_ref[:, cols], preferred_element_type=F32)

    nsel = CHUNK + 16
    row = lax.broadcasted_iota(jnp.int32, (nsel, 3 * CHUNK), 0)
    col = lax.broadcasted_iota(jnp.int32, (nsel, 3 * CHUNK), 1) & (CHUNK - 1)
    tgt = jnp.where(row < CHUNK, row,
                    jnp.where(row < CHUNK + 8, (row - CHUNK) * 8 + 7, (row - CHUNK - 8) * 8 + 3))
    tri3 = (tgt >= col).astype(BF16)
    ti = lax.broadcasted_iota(jnp.int32, (CHUNK, CHUNK), 0)
    si = lax.broadcasted_iota(jnp.int32, (CHUNK, CHUNK), 1)
    masks = {}
    for w, sh in ((32, 5), (16, 4), (8, 3)):
        masks[w] = ((ti >> sh) == (si >> sh) + 1) & ((ti >> (sh + 1)) == (si >> (sh + 1)))
    masks[0] = ((ti >> 3) == (si >> 3)) & (si <= ti)

    u_blocks = 4
    u_cols = D_S5 // u_blocks
    for c in range(p_ref.shape[0] // CHUNK):
        if c < u_blocks:
            cols = slice(c * u_cols, (c + 1) * u_cols)
            u_ref[:, cols] = jnp.dot(xn, w_ref[:, 4 * D_HG + c * u_cols:4 * D_HG + (c + 1) * u_cols],
                                     preferred_element_type=F32)
        _hgrn2_chunk(p_ref, lb_ref, gain_ref, o_ref, st_ref, bc_ref, tri3, masks,
                     pl.ds(c * CHUNK, CHUNK))


def _hgrn2_chunk(p_ref, lb_ref, gain_ref, o_ref, st_ref, bc_ref, tri3, masks, rows):
    q = p_ref[rows, 0:D_HG]
    z = p_ref[rows, D_HG:2 * D_HG]
    v = p_ref[rows, 2 * D_HG:3 * D_HG].astype(BF16)
    lb = lb_ref[...]

    th = 0.5 * jnp.tanh(0.5 * z)
    gl = jnp.log2(jnp.maximum(lb + (1.0 - lb) * (0.5 + th), F_FLOOR))
    kk = (1.0 - lb) * (0.5 - th)
    qf = _silu(q)

    hi = gl.astype(BF16)
    r1 = gl - hi.astype(F32)
    mid = r1.astype(BF16)
    lo = (r1 - mid.astype(F32)).astype(BF16)
    cum = jnp.dot(tri3, jnp.concatenate([hi, mid, lo], axis=0), preferred_element_type=F32)
    b = cum[0:CHUNK]
    l8 = cum[CHUNK:CHUNK + 8]
    m8 = cum[CHUNK + 8:CHUNK + 16]
    j8 = lax.broadcasted_iota(jnp.int32, l8.shape, 0)
    s8 = jnp.where(j8 == 0, 0.0, pltpu.roll(l8, 1, 0))

    qe = qf * jnp.exp2(b - _expand_rows(bc_ref, 0, s8))
    kf = kk * jnp.exp2(_expand_rows(bc_ref, 1, l8) - b)
    qh = {8: qe.astype(BF16)}
    kh = {8: kf.astype(BF16)}
    for lvl, (w, m) in enumerate(((16, 2), (32, 4), (64, 8))):
        cq = jnp.exp2(s8 - _pick_rows(s8, m, True))
        ck = jnp.exp2(_pick_rows(l8, m, False) - l8)
        qh[w] = (qe * _expand_rows(bc_ref, 3 + 2 * lvl, cq)).astype(BF16)
        kh[w] = (kf * _expand_rows(bc_ref, 4 + 2 * lvl, ck)).astype(BF16)
    b_mid = _expand_rows(bc_ref, 2, m8)
    qh[0] = (qf * jnp.exp2(b - b_mid)).astype(BF16)
    kh[0] = (kk * jnp.exp2(b_mid - b)).astype(BF16)
    dec_all = jnp.exp2(l8[7:8, :])

    gate = p_ref[rows, 3 * D_HG:4 * D_HG]
    gate = _silu(gate)
    gain = gain_ref[...]

    heads = [slice(h * HG_DK, (h + 1) * HG_DK) for h in range(HG_HEADS)]
    sts = [st_ref[h] for h in range(HG_HEADS)]
    o_inter = [_dot_nt(qh[64][:, sl], st.astype(BF16)) for sl, st in zip(heads, sts)]
    scores = []
    for sl in heads:
        sc = jnp.zeros((CHUNK, CHUNK), F32)
        for w in (32, 16, 8, 0):
            sc = jnp.where(masks[w], _dot_nt(qh[w][:, sl], kh[w][:, sl]), sc)
        scores.append(sc.astype(BF16))
    outs = [oi + jnp.dot(sc, v[:, sl], preferred_element_type=F32)
            for oi, sc, sl in zip(o_inter, scores, heads)]
    for h, (sl, st) in enumerate(zip(heads, sts)):
        st_ref[h] = dec_all[:, sl] * st + _dot_tn(v[:, sl], kh[64][:, sl])
    for o, sl in zip(outs, heads):
        ms = jnp.mean(o * o, axis=-1, keepdims=True)
        o = o * lax.rsqrt(ms + EPS) * gain[:, sl]
        o_ref[rows, sl] = (o * gate[:, sl]).astype(BF16)


def _proj_hgrn2(h, prefix, g, w_all, layer, lb, gain, first, bsz, chunks_per_step=11):
    d = h.shape[1]
    lp = h.shape[0] // bsz + (CHUNK if first else 0)
    rows = chunks_per_step * CHUNK
    const = lambda b, c: (0, 0)
    return pl.pallas_call(
        functools.partial(_proj_hgrn2_kernel, first=first),
        grid=(bsz, lp // rows),
        in_specs=[
            _row_tile_specs(first, bsz, lp, rows, d),
            pl.BlockSpec((CHUNK, d), const),
            pl.BlockSpec((1, d), const),
            pl.BlockSpec((None, d, D_IN), lambda b, c: (layer, 0, 0),
                         pipeline_mode=pl.Buffered(1)),
            pl.BlockSpec((1, D_HG), const),
            pl.BlockSpec((1, D_HG), const),
        ],
        out_specs=[pl.BlockSpec((None, rows, D_HG), lambda b, c: (b, c, 0)),
                   pl.BlockSpec((None, rows, D_S5), lambda b, c: (b, c, 0))],
        out_shape=[jax.ShapeDtypeStruct((bsz, lp, D_HG), BF16),
                   jax.ShapeDtypeStruct((bsz, lp, D_S5), F32)],
        scratch_shapes=[
            pltpu.VMEM((rows, d), BF16),
            pltpu.VMEM((rows, 4 * D_HG), F32),
            pltpu.VMEM((HG_HEADS, HG_DK, HG_DK), F32),
            pltpu.VMEM((9, D_HG // LANES, 8, LANES), F32),
        ],
        compiler_params=pltpu.CompilerParams(
            dimension_semantics=("arbitrary", "arbitrary"), vmem_limit_bytes=VMEM_LIMIT_BIG),
        name="proj_hgrn2",
    )(h, prefix, g.reshape(1, d), w_all, lb.reshape(1, D_HG), gain.reshape(1, D_HG))


def _gelu_tanh(x):
    c = 0.7978845608028654
    half = 0.5 * x
    return half + half * jnp.tanh(x * (c + (c * 0.044715) * (x * x)))


def _s5core_kernel(u_ref, t_ref, h_ref, g_ref, a_ref, y_ref, v_ref, xp_ref, *, bsz):
    nchunks = u_ref.shape[0] // bsz
    u = u_ref[...]
    v_ref[...] = jnp.dot(u, h_ref[0], preferred_element_type=F32)
    a_re = jnp.broadcast_to(a_ref[0, 0:1, :], (bsz, LANES))
    a_im = jnp.broadcast_to(a_ref[0, 1:2, :], (bsz, LANES))

    y_in = [jnp.dot(u[:, k * S5_K:(k + 1) * S5_K], t_ref[0, k], preferred_element_type=F32)
            for k in range(2)]

    x_re = x_im = jnp.zeros((bsz, LANES), F32)
    for n in range(nchunks):
        rows = slice(n * bsz, (n + 1) * bsz)
        xp_ref[rows, 0:LANES] = x_re.astype(BF16)
        xp_ref[rows, LANES:2 * LANES] = x_im.astype(BF16)
        x_re, x_im = (a_re * x_re - a_im * x_im + v_ref[rows, 0:LANES],
                      a_re * x_im + a_im * x_re + v_ref[rows, LANES:2 * LANES])

    ys = jnp.dot(xp_ref[...], g_ref[0], preferred_element_type=F32)
    for k in range(2):
        cols = slice(k * S5_K, (k + 1) * S5_K)
        y_ref[:, cols] = ys[:, cols] + y_in[k]


def _s5core(uc, params, layer, bsz):
    tmat, hmat, gmat, a2 = params
    rows = uc.shape[0]
    return pl.pallas_call(
        functools.partial(_s5core_kernel, bsz=bsz),
        grid=(S5_PAIRS,),
        in_specs=[
            pl.BlockSpec((rows, 2 * S5_K), lambda p: (0, p)),
            pl.BlockSpec((None, 1, 2, S5_K, S5_K), lambda p: (layer, p, 0, 0, 0)),
            pl.BlockSpec((None, 1, 2 * S5_K, 4 * S5_STATE), lambda p: (layer, p, 0, 0)),
            pl.BlockSpec((None, 1, 4 * S5_STATE, 2 * S5_K), lambda p: (layer, p, 0, 0)),
            pl.BlockSpec((None, 1, 2, 2 * S5_STATE), lambda p: (layer, p, 0, 0)),
        ],
        out_specs=pl.BlockSpec((rows, 2 * S5_K), lambda p: (0, p)),
        out_shape=jax.ShapeDtypeStruct((rows, S5_GROUPS * S5_K), F32),
        scratch_shapes=[
            pltpu.VMEM((rows, 4 * S5_STATE), F32),
            pltpu.VMEM((rows, 4 * S5_STATE), BF16),
        ],
        compiler_params=pltpu.CompilerParams(
            dimension_semantics=("arbitrary",), vmem_limit_bytes=VMEM_LIMIT),
        name="s5core",
    )(uc, tmat, hmat, gmat, a2)


def _atom_transpose(groups):
    lane = lax.broadcasted_iota(jnp.int32, groups[0][0].shape, 1)
    groups = [list(xs) for xs in groups]
    for s in range(3):
        d = 1 << s
        keep = ((lane >> (4 + s)) & 1) == 0
        for xs in groups:
            for i in range(8):
                if i & d:
                    continue
                lo, hi = xs[i], xs[i + d]
                xs[i] = jnp.where(keep, lo, pltpu.roll(hi, S5_GROUP * d, 1))
                xs[i + d] = jnp.where(keep, pltpu.roll(lo, LANES - S5_GROUP * d, 1), hi)
    return groups


def _to_chunks_kernel(u_ref, o_ref, rows_ref):
    bsz, tt, width = u_ref.shape
    pitch = rows_ref.shape[1] // bsz
    for v in range(width // LANES):
        for b in range(bsz):
            rows_ref[v, b * pitch:b * pitch + tt, :] = u_ref[b, :, v * LANES:(v + 1) * LANES]
    halves = [(v, n, k) for v in range(width // LANES) for n in range(tt // S5_R) for k in range(2)]
    groups = _atom_transpose(
        [[rows_ref[v, pl.ds(n * S5_R + 8 * k + j, bsz, stride=pitch), :] for j in range(8)]
         for v, n, k in halves])
    for (v, n, k), ys in zip(halves, groups):
        for g, y in enumerate(ys):
            lane0 = (8 * v + g) * S5_K + k * LANES
            o_ref[n * bsz:(n + 1) * bsz, lane0:lane0 + LANES] = y.astype(BF16)


def _to_chunks(u3, tt=176, width=512):
    bsz, lp, _ = u3.shape
    return pl.pallas_call(
        _to_chunks_kernel,
        grid=(lp // tt, D_S5 // width),
        in_specs=[pl.BlockSpec((bsz, tt, width), lambda i, v: (0, i, v))],
        out_specs=pl.BlockSpec((tt, width // S5_GROUP * S5_K), lambda i, v: (i, v)),
        out_shape=jax.ShapeDtypeStruct((lp // S5_R * bsz, S5_GROUPS * S5_K), BF16),
        scratch_shapes=[pltpu.VMEM((width // LANES, bsz * (tt + 8), LANES), F32)],
        compiler_params=pltpu.CompilerParams(
            dimension_semantics=("arbitrary", "arbitrary"), vmem_limit_bytes=VMEM_LIMIT),
        name="s5_to_chunks",
    )(u3)


def _s5glu_kernel(yc_ref, u_ref, d_ref, w_ref, b_ref, gain_ref, o_ref, yn_ref):
    bsz, tt, _ = u_ref.shape
    pitch = yn_ref.shape[1] // bsz
    for n in range(tt // S5_R):
        halves = [(v, k) for v in range(D_S5 // LANES) for k in range(2)]
        groups = _atom_transpose(
            [[yc_ref[n * bsz:(n + 1) * bsz,
                     (8 * v + g) * S5_K + k * LANES:(8 * v + g) * S5_K + (k + 1) * LANES]
              for g in range(8)] for v, k in halves])
        for (v, k), xs in zip(halves, groups):
            for j, x in enumerate(xs):
                yn_ref[v, pl.ds(n * S5_R + 8 * k + j, bsz, stride=pitch), :] = x
    y = jnp.concatenate(
        [jnp.concatenate([yn_ref[v, b * pitch:b * pitch + tt, :] for v in range(D_S5 // LANES)],
                         axis=1) for b in range(bsz)], axis=0)
    y = _gelu_tanh(y + d_ref[...] * u_ref[...].reshape(bsz * tt, D_S5))
    g = jnp.dot(y.astype(BF16), w_ref[...], preferred_element_type=F32) + b_ref[...]
    o_ref[...] = _rms(y * _sigmoid(g), gain_ref[...]).reshape(bsz, tt, D_S5).astype(BF16)


def _s5glu(yc, u3, d_skip, wglu_all, layer, bglu, gain, tt=48):
    bsz, lp, _ = u3.shape
    return pl.pallas_call(
        _s5glu_kernel,
        grid=(lp // tt,),
        in_specs=[
            pl.BlockSpec((tt // S5_R * bsz, S5_GROUPS * S5_K), lambda i: (i, 0)),
            pl.BlockSpec((bsz, tt, D_S5), lambda i: (0, i, 0)),
            pl.BlockSpec((1, D_S5), lambda i: (0, 0)),
            pl.BlockSpec((None, D_S5, D_S5), lambda i: (layer, 0, 0)),
            pl.BlockSpec((1, D_S5), lambda i: (0, 0)),
            pl.BlockSpec((1, D_S5), lambda i: (0, 0)),
        ],
        out_specs=pl.BlockSpec((bsz, tt, D_S5), lambda i: (0, i, 0)),
        out_shape=jax.ShapeDtypeStruct((bsz, lp, D_S5), BF16),
        scratch_shapes=[pltpu.VMEM((D_S5 // LANES, bsz * (tt + 8), LANES), F32)],
        compiler_params=pltpu.CompilerParams(
            dimension_semantics=("arbitrary",), vmem_limit_bytes=VMEM_LIMIT),
        name="s5glu",
    )(yc, u3, d_skip.reshape(1, D_S5), wglu_all, bglu.reshape(1, D_S5), gain.reshape(1, D_S5))


def _s5_params(lam_re, lam_im, log_step, b_re, b_im, c_re, c_im):
    hp = lax.Precision.HIGHEST
    a_re = jnp.minimum(lam_re.astype(F32), -1e-4)
    a_im = lam_im.astype(F32)
    dt = jnp.exp(log_step.astype(F32))[:, None]
    mag = jnp.exp(a_re * dt)
    ab_re = mag * jnp.cos(a_im * dt)
    ab_im = mag * jnp.sin(a_im * dt)
    den = a_re * a_re + a_im * a_im
    x_re, x_im = ab_re - 1.0, ab_im
    z_re = (x_re * a_re + x_im * a_im) / den
    z_im = (x_im * a_re - x_re * a_im) / den
    br, bi = b_re.astype(F32), b_im.astype(F32)
    bb_re = z_re[..., None] * br - z_im[..., None] * bi
    bb_im = z_re[..., None] * bi + z_im[..., None] * br
    cr, ci = c_re.astype(F32), c_im.astype(F32)

    pw_re, pw_im = [jnp.ones_like(ab_re)], [jnp.zeros_like(ab_re)]
    for _ in range(S5_R):
        r, i = pw_re[-1], pw_im[-1]
        pw_re.append(r * ab_re - i * ab_im)
        pw_im.append(r * ab_im + i * ab_re)
    p_re, p_im = jnp.stack(pw_re), jnp.stack(pw_im)

    q_re, q_im = p_re[:S5_R, :, :, None], p_im[:S5_R, :, :, None]
    m_re = q_re * bb_re - q_im * bb_im
    m_im = q_re * bb_im + q_im * bb_re
    kern = (jnp.einsum('gop,lgph->ghlo', cr, m_re, precision=hp)
            - jnp.einsum('gop,lgph->ghlo', ci, m_im, precision=hp))
    kz = jnp.concatenate([jnp.zeros((S5_GROUPS, S5_GROUP, S5_K), F32),
                          kern.reshape(S5_GROUPS, S5_GROUP, S5_K)], axis=-1)
    tmat = jnp.stack([kz[:, :, (S5_R - s) * S5_GROUP:(S5_R - s) * S5_GROUP + S5_K]
                      for s in range(S5_R)], axis=1).reshape(S5_PAIRS, 2, S5_K, S5_K)

    bt_re, bt_im = bb_re.transpose(0, 2, 1)[:, None], bb_im.transpose(0, 2, 1)[:, None]
    rp_re = jnp.moveaxis(p_re[S5_R - 1::-1], 0, 1)[:, :, None, :]
    rp_im = jnp.moveaxis(p_im[S5_R - 1::-1], 0, 1)[:, :, None, :]
    h_re = (rp_re * bt_re - rp_im * bt_im).reshape(S5_PAIRS, 2, S5_K, S5_STATE)
    h_im = (rp_re * bt_im + rp_im * bt_re).reshape(S5_PAIRS, 2, S5_K, S5_STATE)
    zh = jnp.zeros_like(h_re[:, 0])
    hmat = jnp.concatenate([
        jnp.concatenate([h_re[:, 0], zh, h_im[:, 0], zh], axis=-1),
        jnp.concatenate([zh, h_re[:, 1], zh, h_im[:, 1]], axis=-1)], axis=1)

    ct_re, ct_im = cr.transpose(0, 2, 1)[:, :, None, :], ci.transpose(0, 2, 1)[:, :, None, :]
    e_re = jnp.moveaxis(p_re[1:], 0, 2)[..., None]
    e_im = jnp.moveaxis(p_im[1:], 0, 2)[..., None]
    g_re = (ct_re * e_re - ct_im * e_im).reshape(S5_PAIRS, 2, S5_STATE, S5_K)
    g_im = -(ct_re * e_im + ct_im * e_re).reshape(S5_PAIRS, 2, S5_STATE, S5_K)
    zg = jnp.zeros_like(g_re[:, 0])
    gmat = jnp.concatenate([
        jnp.concatenate([g_re[:, 0], zg], axis=-1), jnp.concatenate([zg, g_re[:, 1]], axis=-1),
        jnp.concatenate([g_im[:, 0], zg], axis=-1), jnp.concatenate([zg, g_im[:, 1]], axis=-1)],
        axis=1)

    a2 = jnp.stack([p_re[S5_R].reshape(S5_PAIRS, 2 * S5_STATE),
                    p_im[S5_R].reshape(S5_PAIRS, 2 * S5_STATE)], axis=1)
    return tmat.astype(BF16), hmat.astype(BF16), gmat.astype(BF16), a2


def _s5(u3, params, layer, d_skip, wglu_all, bglu, gain):
    yc = _s5core(_to_chunks(u3), params, layer, u3.shape[0])
    return _s5glu(yc, u3, d_skip, wglu_all, layer, bglu, gain)


def _outproj_kernel(h_ref, pre_ref, a_ref, b_ref, w_ref, o_ref, *, first):
    y = (jnp.dot(a_ref[...], w_ref[0], preferred_element_type=F32)
         + jnp.dot(b_ref[...], w_ref[1], preferred_element_type=F32))
    if first:
        npre = pre_ref.shape[0]

        @pl.when(pl.program_id(1) == 0)
        def _():
            o_ref[0:npre, :] = y[0:npre, :] + pre_ref[...]
            o_ref[npre:, :] = y[npre:, :] + h_ref[0:h_ref.shape[0] - npre, :]

        @pl.when(pl.program_id(1) != 0)
        def _():
            o_ref[...] = y + h_ref[...]
    else:
        o_ref[...] = y + h_ref[...]


def _outproj(h, prefix, a, b, w_all, layer, first, bsz, tm=704):
    k = a.shape[1]
    d = h.shape[1]
    lp = a.shape[0] // bsz
    tiles = lp // tm
    flat = lambda b, j: (b * tiles + j, 0)
    return pl.pallas_call(
        functools.partial(_outproj_kernel, first=first),
        grid=(bsz, tiles),
        in_specs=[
            _row_tile_specs(first, bsz, lp, tm, d),
            pl.BlockSpec((CHUNK, d), lambda b, j: (0, 0)),
            pl.BlockSpec((tm, k), flat),
            pl.BlockSpec((tm, k), flat),
            pl.BlockSpec((None, 2, k, d), lambda b, j: (layer, 0, 0, 0)),
        ],
        out_specs=pl.BlockSpec((tm, d), flat),
        out_shape=jax.ShapeDtypeStruct((bsz * lp, d), F32),
        compiler_params=pltpu.CompilerParams(
            dimension_semantics=("arbitrary", "arbitrary"), vmem_limit_bytes=VMEM_LIMIT),
        name="outproj",
    )(h, prefix, a, b, w_all)


FFN_HALO = 16


def _ffn_kernel(h_ref, halo_ref, g_ref, wg_ref, wu_ref, cw_ref, cb_ref, wdp_ref, wdl_ref, fg_ref,
                o_ref, hn_ref, hid_ref, *, final):
    f_axis = 2 if final else 1
    f = pl.program_id(f_axis)
    last = pl.num_programs(f_axis) - 1

    def hidden():
        a = jnp.dot(hn_ref[...], wg_ref[...], preferred_element_type=F32)
        up = jnp.dot(hn_ref[FFN_HALO:, :], wu_ref[...], preferred_element_type=F32)
        cw = cw_ref[...]
        conv = (cb_ref[...]
                + cw[0:1, :] * pltpu.roll(a, 2, 0)[FFN_HALO:, :]
                + cw[1:2, :] * pltpu.roll(a, 1, 0)[FFN_HALO:, :]
                + cw[2:3, :] * a[FFN_HALO:, :])
        return (_silu(conv) * up).astype(BF16)

    @pl.when(f == 0)
    def _():
        x = h_ref[...]
        hn_ref[FFN_HALO:, :] = _rms(x, g_ref[...]).astype(BF16)
        hn_ref[0:FFN_HALO, :] = _rms(halo_ref[...], g_ref[...]).astype(BF16)
        o_ref[...] = x
        hid_ref[0] = hidden()

    @pl.when((f > 0) & (f < last))
    def _():
        o_ref[...] += jnp.dot(hid_ref[(f + 1) % 2], wdp_ref[...], preferred_element_type=F32)
        hid_ref[f % 2] = hidden()

    @pl.when(f == last)
    def _():
        o_ref[...] += jnp.dot(hid_ref[(f + 1) % 2], wdp_ref[...], preferred_element_type=F32)
        out = o_ref[...] + jnp.dot(hidden(), wdl_ref[...], preferred_element_type=F32)
        o_ref[...] = _rms(out, fg_ref[...]) if final else out


def _ffn(h, g, layer, wg_all, wu_all, cw_all, cb_all, wd_all, fg, final, bsz, tm=512, tf=512):
    t, d = h.shape
    ff = wg_all.shape[2]
    hb = tm // FFN_HALO
    lp = t // bsz
    if final:
        grid = (bsz, (lp - CHUNK) // tm, ff // tf)
        row0 = lambda b, j, back: pl.multiple_of(b * lp + CHUNK + j * tm - back, FFN_HALO)
        row_specs = [
            pl.BlockSpec((pl.Element(tm), pl.Element(d)), lambda b, j, f: (row0(b, j, 0), 0)),
            pl.BlockSpec((pl.Element(FFN_HALO), pl.Element(d)),
                         lambda b, j, f: (row0(b, j, FFN_HALO), 0)),
        ]
        out_spec = pl.BlockSpec((tm, d), lambda b, j, f: (b * ((lp - CHUNK) // tm) + j, 0))
        out_rows = bsz * (lp - CHUNK)
        wmap = lambda fn: (lambda b, j, f: fn(f))
    else:
        grid = (t // tm, ff // tf)
        row_specs = [
            pl.BlockSpec((tm, d), lambda i, f: (i, 0)),
            pl.BlockSpec((FFN_HALO, d), lambda i, f: (jnp.maximum(i * hb - 1, 0), 0)),
        ]
        out_spec = pl.BlockSpec((tm, d), lambda i, f: (i, 0))
        out_rows = t
        wmap = lambda fn: (lambda i, f: fn(f))
    return pl.pallas_call(
        functools.partial(_ffn_kernel, final=final),
        grid=grid,
        in_specs=row_specs + [
            pl.BlockSpec((1, d), wmap(lambda f: (0, 0))),
            pl.BlockSpec((None, d, tf), wmap(lambda f: (layer, 0, f))),
            pl.BlockSpec((None, d, tf), wmap(lambda f: (layer, 0, f))),
            pl.BlockSpec((None, 3, tf), wmap(lambda f: (layer, 0, f))),
            pl.BlockSpec((None, 1, tf), wmap(lambda f: (layer, 0, f))),
            pl.BlockSpec((None, tf, d), wmap(lambda f: (layer, jnp.maximum(f - 1, 0), 0))),
            pl.BlockSpec((None, tf, d), wmap(lambda f: (layer, ff // tf - 1, 0))),
            pl.BlockSpec((1, d), wmap(lambda f: (0, 0))),
        ],
        out_specs=out_spec,
        out_shape=jax.ShapeDtypeStruct((out_rows, d), F32),
        scratch_shapes=[pltpu.VMEM((tm + FFN_HALO, d), BF16),
                        pltpu.VMEM((2, tm, tf), BF16)],
        compiler_params=pltpu.CompilerParams(
            dimension_semantics=("arbitrary",) * len(grid), vmem_limit_bytes=VMEM_LIMIT),
        name="ffn_final" if final else "ffn",
    )(h, h, g.reshape(1, d), wg_all, wu_all, cw_all, cb_all.reshape(cb_all.shape[0], 1, ff),
      wd_all, wd_all, fg.reshape(1, d))


def kernel(x, meta_tokens, lb_logits, norm_mix, w_in, hg_norm, s5_lambda_re, s5_lambda_im,
           s5_log_step, s5_b_re, s5_b_im, s5_c_re, s5_c_im, s5_d, w_glu, b_glu, s5_norm, w_out,
           norm_ffn, w_ffn_gate, w_ffn_up, ffn_conv_w, ffn_conv_b, w_ffn_down, final_norm):
    bsz, seq, d = x.shape
    lp = seq + CHUNK
    prefix = jnp.concatenate([jnp.zeros((SEQ_PAD, d), x.dtype), meta_tokens.astype(x.dtype)], axis=0)
    h = x.reshape(bsz * seq, d)

    sm = jax.nn.softmax(lb_logits.astype(F32), axis=0)
    lb_all = jnp.cumsum(sm, axis=0) - sm[0:1]

    w_in_b = w_in.astype(BF16)
    w_out_b = w_out.astype(BF16).reshape(DEPTH, 2, D_HG, d)
    w_glu_b = w_glu.astype(BF16)
    wg_b, wu_b, wd_b = (w.astype(BF16) for w in (w_ffn_gate, w_ffn_up, w_ffn_down))
    s5_params = jax.vmap(_s5_params)(s5_lambda_re, s5_lambda_im, s5_log_step,
                                     s5_b_re, s5_b_im, s5_c_re, s5_c_im)

    for l in range(DEPTH):
        o_hg, u3 = _proj_hgrn2(h, prefix, norm_mix[l], w_in_b, l, lb_all[l], hg_norm[l],
                               first=(l == 0), bsz=bsz)
        o_s5 = _s5(u3, s5_params, l, s5_d[l], w_glu_b, b_glu[l], s5_norm[l])
        h = _outproj(h, prefix, o_hg.reshape(bsz * lp, D_HG), o_s5.reshape(bsz * lp, D_S5),
                     w_out_b, l, first=(l == 0), bsz=bsz)
        h = _ffn(h, norm_ffn[l], l, wg_b, wu_b, ffn_conv_w, ffn_conv_b, wd_b, final_norm,
                 final=(l == DEPTH - 1), bsz=bsz, tm=512 if l == DEPTH - 1 else 768)
    return h.reshape(bsz, seq, d)
```

```python
import functools

import jax
import jax.numpy as jnp
from jax import lax
from jax.experimental import pallas as pl
from jax.experimental.pallas import tpu as pltpu

F32 = jnp.float32
BF16 = jnp.bfloat16
LANES = 128

DEPTH = 2
CHUNK = 64
N_META = 16
D_HG = 1024
HG_HEADS = 8
HG_DK = 128
D_S5 = 1024
S5_GROUP = 16
S5_GROUPS = 64
S5_STATE = 64
D_IN = 4 * D_HG + D_S5
EPS = 1e-6
F_FLOOR = 1e-6

SEQ_PAD = CHUNK - N_META
VMEM_LIMIT = 56 * 1024 * 1024
VMEM_LIMIT_BIG = 62 * 1024 * 1024

S5_R = 16
S5_K = S5_R * S5_GROUP
S5_PAIRS = S5_GROUPS // 2


def _rms(x, g):
    ms = jnp.mean(x * x, axis=-1, keepdims=True)
    return x * lax.rsqrt(ms + EPS) * g


def _sigmoid(x):
    return 0.5 + 0.5 * jnp.tanh(0.5 * x)


def _silu(x):
    half = 0.5 * x
    return half + half * jnp.tanh(half)


def _normed_tile(h_ref, pre_ref, g_ref, xn_ref, first):
    if first:
        npre = pre_ref.shape[0]

        @pl.when(pl.program_id(1) == 0)
        def _():
            xn_ref[0:npre, :] = _rms(pre_ref[...], g_ref[...]).astype(BF16)
            xn_ref[npre:, :] = _rms(h_ref[0:h_ref.shape[0] - npre, :], g_ref[...]).astype(BF16)

        @pl.when(pl.program_id(1) != 0)
        def _():
            xn_ref[...] = _rms(h_ref[...], g_ref[...]).astype(BF16)
    else:
        xn_ref[...] = _rms(h_ref[...], g_ref[...]).astype(BF16)


def _row_tile_specs(first, bsz, lp, tm, d):
    tiles = lp // tm
    if first:
        def rows(b, j, *_):
            return (pl.multiple_of(b * (lp - CHUNK) + jnp.maximum(j * tm - CHUNK, 0), 8), 0)
        return pl.BlockSpec((pl.Element(tm), pl.Element(d)), rows)
    return pl.BlockSpec((tm, d), lambda b, j, *_: (b * tiles + j, 0))


def _pick_rows(x, m, first):
    j = lax.broadcasted_iota(jnp.int32, x.shape, 0)
    out = None
    for blk in range(8 // m - 1, -1, -1):
        r = blk * m + (0 if first else m - 1)
        row = jnp.broadcast_to(x[r:r + 1, :], x.shape)
        out = row if out is None else jnp.where(j < (blk + 1) * m, row, out)
    return out


def _expand_rows(ref, i, x):
    nv = x.shape[1] // LANES
    for v in range(nv):
        ref[i, v] = x[:, v * LANES:(v + 1) * LANES]
    return jnp.concatenate(
        [jnp.concatenate([ref[i, v, pl.ds(j, 8, stride=0), :] for v in range(nv)], axis=1)
         for j in range(8)], axis=0)


def _dot_nt(a, b):
    return lax.dot_general(a, b, (((1,), (1,)), ((), ())), preferred_element_type=F32)


def _dot_tn(a, b):
    return lax.dot_general(a, b, (((0,), (0,)), ((), ())), preferred_element_type=F32)


def _proj_hgrn2_kernel(h_ref, pre_ref, g_ref, w_ref, lb_ref, gain_ref, o_ref, u_ref,
                       xn_ref, p_ref, st_ref, bc_ref, *, first):
    @pl.when(pl.program_id(1) == 0)
    def _():
        st_ref[...] = jnp.zeros_like(st_ref)

    _normed_tile(h_ref, pre_ref, g_ref, xn_ref, first)
    xn = xn_ref[...]
    for k in range(4 * D_HG // D_S5):
        cols = slice(k * D_S5, (k + 1) * D_S5)
        p_ref[:, cols] = jnp.dot(xn, w_ref[:, cols], preferred_element_type=F32)

    nsel = CHUNK + 16
    row = lax.broadcasted_iota(jnp.int32, (nsel, 3 * CHUNK), 0)
    col = lax.broadcasted_iota(jnp.int32, (nsel, 3 * CHUNK), 1) & (CHUNK - 1)
    tgt = jnp.where(row < CHUNK, row,
                    jnp.where(row < CHUNK + 8, (row - CHUNK) * 8 + 7, (row - CHUNK - 8) * 8 + 3))
    tri3 = (tgt >= col).astype(BF16)
    ti = lax.broadcasted_iota(jnp.int32, (CHUNK, CHUNK), 0)
    si = lax.broadcasted_iota(jnp.int32, (CHUNK, CHUNK), 1)
    masks = {}
    for w, sh in ((32, 5), (16, 4), (8, 3)):
        masks[w] = ((ti >> sh) == (si >> sh) + 1) & ((ti >> (sh + 1)) == (si >> (sh + 1)))
    masks[0] = ((ti >> 3) == (si >> 3)) & (si <= ti)

    u_blocks = 4
    u_cols = D_S5 // u_blocks
    for c in range(p_ref.shape[0] // CHUNK):
        if c % 2 == 0 and c // 2 < u_blocks:
            j = c // 2
            cols = slice(j * u_cols, (j + 1) * u_cols)
            u_ref[:, cols] = jnp.dot(xn, w_ref[:, 4 * D_HG + j * u_cols:4 * D_HG + (j + 1) * u_cols],
                                     preferred_element_type=F32)
        _hgrn2_chunk(p_ref, lb_ref, gain_ref, o_ref, st_ref, bc_ref, tri3, masks,
                     pl.ds(c * CHUNK, CHUNK))


def _hgrn2_chunk(p_ref, lb_ref, gain_ref, o_ref, st_ref, bc_ref, tri3, masks, rows):
    q = p_ref[rows, 0:D_HG]
    z = p_ref[rows, D_HG:2 * D_HG]
    v = p_ref[rows, 2 * D_HG:3 * D_HG].astype(BF16)
    lb = lb_ref[...]

    th = 0.5 * jnp.tanh(0.5 * z)
    gl = jnp.log2(jnp.maximum(lb + (1.0 - lb) * (0.5 + th), F_FLOOR))
    kk = (1.0 - lb) * (0.5 - th)
    qf = _silu(q)

    hi = gl.astype(BF16)
    r1 = gl - hi.astype(F32)
    mid = r1.astype(BF16)
    lo = (r1 - mid.astype(F32)).astype(BF16)
    cum = jnp.dot(tri3, jnp.concatenate([hi, mid, lo], axis=0), preferred_element_type=F32)
    b = cum[0:CHUNK]
    l8 = cum[CHUNK:CHUNK + 8]
    m8 = cum[CHUNK + 8:CHUNK + 16]
    j8 = lax.broadcasted_iota(jnp.int32, l8.shape, 0)
    s8 = jnp.where(j8 == 0, 0.0, pltpu.roll(l8, 1, 0))

    qe = qf * jnp.exp2(b - _expand_rows(bc_ref, 0, s8))
    kf = kk * jnp.exp2(_expand_rows(bc_ref, 1, l8) - b)
    qh = {8: qe.astype(BF16)}
    kh = {8: kf.astype(BF16)}
    for lvl, (w, m) in enumerate(((16, 2), (32, 4), (64, 8))):
        cq = jnp.exp2(s8 - _pick_rows(s8, m, True))
        ck = jnp.exp2(_pick_rows(l8, m, False) - l8)
        qh[w] = (qe * _expand_rows(bc_ref, 3 + 2 * lvl, cq)).astype(BF16)
        kh[w] = (kf * _expand_rows(bc_ref, 4 + 2 * lvl, ck)).astype(BF16)
    b_mid = _expand_rows(bc_ref, 2, m8)
    qh[0] = (qf * jnp.exp2(b - b_mid)).astype(BF16)
    kh[0] = (kk * jnp.exp2(b_mid - b)).astype(BF16)
    dec_all = jnp.exp2(l8[7:8, :])

    gate = p_ref[rows, 3 * D_HG:4 * D_HG]
    gate = _silu(gate)
    gain = gain_ref[...]

    heads = [slice(h * HG_DK, (h + 1) * HG_DK) for h in range(HG_HEADS)]
    sts = [st_ref[h] for h in range(HG_HEADS)]
    o_inter = [_dot_nt(qh[64][:, sl], st.astype(BF16)) for sl, st in zip(heads, sts)]
    scores = []
    for sl in heads:
        sc = jnp.zeros((CHUNK, CHUNK), F32)
        for w in (32, 16, 8, 0):
            sc = jnp.where(masks[w], _dot_nt(qh[w][:, sl], kh[w][:, sl]), sc)
        scores.append(sc.astype(BF16))
    outs = [oi + jnp.dot(sc, v[:, sl], preferred_element_type=F32)
            for oi, sc, sl in zip(o_inter, scores, heads)]
    for h, (sl, st) in enumerate(zip(heads, sts)):
        st_ref[h] = dec_all[:, sl] * st + _dot_tn(v[:, sl], kh[64][:, sl])
    for o, sl in zip(outs, heads):
        ms = jnp.mean(o * o, axis=-1, keepdims=True)
        o = o * lax.rsqrt(ms + EPS) * gain[:, sl]
        o_ref[rows, sl] = (o * gate[:, sl]).astype(BF16)


def _proj_hgrn2(h, prefix, g, w_all, layer, lb, gain, first, bsz, chunks_per_step=11):
    d = h.shape[1]
    lp = h.shape[0] // bsz + (CHUNK if first else 0)
    rows = chunks_per_step * CHUNK
    const = lambda b, c: (0, 0)
    return pl.pallas_call(
        functools.partial(_proj_hgrn2_kernel, first=first),
        grid=(bsz, lp // rows),
        in_specs=[
            _row_tile_specs(first, bsz, lp, rows, d),
            pl.BlockSpec((CHUNK, d), const),
            pl.BlockSpec((1, d), const),
            pl.BlockSpec((None, d, D_IN), lambda b, c: (layer, 0, 0),
                         pipeline_mode=pl.Buffered(1)),
            pl.BlockSpec((1, D_HG), const),
            pl.BlockSpec((1, D_HG), const),
        ],
        out_specs=[pl.BlockSpec((None, rows, D_HG), lambda b, c: (b, c, 0)),
                   pl.BlockSpec((None, rows, D_S5), lambda b, c: (b, c, 0))],
        out_shape=[jax.ShapeDtypeStruct((bsz, lp, D_HG), BF16),
                   jax.ShapeDtypeStruct((bsz, lp, D_S5), F32)],
        scratch_shapes=[
            pltpu.VMEM((rows, d), BF16),
            pltpu.VMEM((rows, 4 * D_HG), F32),
            pltpu.VMEM((HG_HEADS, HG_DK, HG_DK), F32),
            pltpu.VMEM((9, D_HG // LANES, 8, LANES), F32),
        ],
        compiler_params=pltpu.CompilerParams(
            dimension_semantics=("arbitrary", "arbitrary"), vmem_limit_bytes=VMEM_LIMIT_BIG),
        name="proj_hgrn2",
    )(h, prefix, g.reshape(1, d), w_all, lb.reshape(1, D_HG), gain.reshape(1, D_HG))


def _gelu_tanh(x):
    c = 0.7978845608028654
    half = 0.5 * x
    return half + half * jnp.tanh(x * (c + (c * 0.044715) * (x * x)))


def _s5core_kernel(u_ref, t_ref, h_ref, g_ref, a_ref, y_ref, v_ref, xp_ref, *, bsz):
    nchunks = u_ref.shape[0] // bsz
    u = u_ref[...]
    v_ref[...] = jnp.dot(u, h_ref[0], preferred_element_type=F32)
    a_re = jnp.broadcast_to(a_ref[0, 0:1, :], (bsz, LANES))
    a_im = jnp.broadcast_to(a_ref[0, 1:2, :], (bsz, LANES))

    y_in = [jnp.dot(u[:, k * S5_K:(k + 1) * S5_K], t_ref[0, k], preferred_element_type=F32)
            for k in range(2)]

    x_re = x_im = jnp.zeros((bsz, LANES), F32)
    for n in range(nchunks):
        rows = slice(n * bsz, (n + 1) * bsz)
        xp_ref[rows, 0:LANES] = x_re.astype(BF16)
        xp_ref[rows, LANES:2 * LANES] = x_im.astype(BF16)
        x_re, x_im = (a_re * x_re - a_im * x_im + v_ref[rows, 0:LANES],
                      a_re * x_im + a_im * x_re + v_ref[rows, LANES:2 * LANES])

    ys = jnp.dot(xp_ref[...], g_ref[0], preferred_element_type=F32)
    for k in range(2):
        cols = slice(k * S5_K, (k + 1) * S5_K)
        y_ref[:, cols] = ys[:, cols] + y_in[k]


def _s5core(uc, params, layer, bsz):
    tmat, hmat, gmat, a2 = params
    rows = uc.shape[0]
    return pl.pallas_call(
        functools.partial(_s5core_kernel, bsz=bsz),
        grid=(S5_PAIRS,),
        in_specs=[
            pl.BlockSpec((rows, 2 * S5_K), lambda p: (0, p)),
            pl.BlockSpec((None, 1, 2, S5_K, S5_K), lambda p: (layer, p, 0, 0, 0)),
            pl.BlockSpec((None, 1, 2 * S5_K, 4 * S5_STATE), lambda p: (layer, p, 0, 0)),
            pl.BlockSpec((None, 1, 4 * S5_STATE, 2 * S5_K), lambda p: (layer, p, 0, 0)),
            pl.BlockSpec((None, 1, 2, 2 * S5_STATE), lambda p: (layer, p, 0, 0)),
        ],
        out_specs=pl.BlockSpec((rows, 2 * S5_K), lambda p: (0, p)),
        out_shape=jax.ShapeDtypeStruct((rows, S5_GROUPS * S5_K), F32),
        scratch_shapes=[
            pltpu.VMEM((rows, 4 * S5_STATE), F32),
            pltpu.VMEM((rows, 4 * S5_STATE), BF16),
        ],
        compiler_params=pltpu.CompilerParams(
            dimension_semantics=("arbitrary",), vmem_limit_bytes=VMEM_LIMIT),
        name="s5core",
    )(uc, tmat, hmat, gmat, a2)


def _atom_transpose(groups):
    lane = lax.broadcasted_iota(jnp.int32, groups[0][0].shape, 1)
    groups = [list(xs) for xs in groups]
    for s in range(3):
        d = 1 << s
        keep = ((lane >> (4 + s)) & 1) == 0
        for xs in groups:
            for i in range(8):
                if i & d:
                    continue
                lo, hi = xs[i], xs[i + d]
                xs[i] = jnp.where(keep, lo, pltpu.roll(hi, S5_GROUP * d, 1))
                xs[i + d] = jnp.where(keep, pltpu.roll(lo, LANES - S5_GROUP * d, 1), hi)
    return groups


def _to_chunks_kernel(u_ref, o_ref, rows_ref):
    bsz, tt, width = u_ref.shape
    pitch = rows_ref.shape[1] // bsz
    for v in range(width // LANES):
        for b in range(bsz):
            rows_ref[v, b * pitch:b * pitch + tt, :] = u_ref[b, :, v * LANES:(v + 1) * LANES]
    halves = [(v, n, k) for v in range(width // LANES) for n in range(tt // S5_R) for k in range(2)]
    groups = _atom_transpose(
        [[rows_ref[v, pl.ds(n * S5_R + 8 * k + j, bsz, stride=pitch), :] for j in range(8)]
         for v, n, k in halves])
    for (v, n, k), ys in zip(halves, groups):
        for g, y in enumerate(ys):
            lane0 = (8 * v + g) * S5_K + k * LANES
            o_ref[n * bsz:(n + 1) * bsz, lane0:lane0 + LANES] = y.astype(BF16)


def _to_chunks(u3, tt=176, width=512):
    bsz, lp, _ = u3.shape
    return pl.pallas_call(
        _to_chunks_kernel,
        grid=(lp // tt, D_S5 // width),
        in_specs=[pl.BlockSpec((bsz, tt, width), lambda i, v: (0, i, v))],
        out_specs=pl.BlockSpec((tt, width // S5_GROUP * S5_K), lambda i, v: (i, v)),
        out_shape=jax.ShapeDtypeStruct((lp // S5_R * bsz, S5_GROUPS * S5_K), BF16),
        scratch_shapes=[pltpu.VMEM((width // LANES, bsz * (tt + 8), LANES), F32)],
        compiler_params=pltpu.CompilerParams(
            dimension_semantics=("arbitrary", "arbitrary"), vmem_limit_bytes=VMEM_LIMIT),
        name="s5_to_chunks",
    )(u3)


def _s5glu_kernel(yc_ref, u_ref, d_ref, w_ref, b_ref, gain_ref, o_ref, yn_ref):
    bsz, tt, _ = u_ref.shape
    pitch = yn_ref.shape[1] // bsz
    for n in range(tt // S5_R):
        halves = [(v, k) for v in range(D_S5 // LANES) for k in range(2)]
        groups = _atom_transpose(
            [[yc_ref[n * bsz:(n + 1) * bsz,
                     (8 * v + g) * S5_K + k * LANES:(8 * v + g) * S5_K + (k + 1) * LANES]
              for g in range(8)] for v, k in halves])
        for (v, k), xs in zip(halves, groups):
            for j, x in enumerate(xs):
                yn_ref[v, pl.ds(n * S5_R + 8 * k + j, bsz, stride=pitch), :] = x
    y = jnp.concatenate(
        [jnp.concatenate([yn_ref[v, b * pitch:b * pitch + tt, :] for v in range(D_S5 // LANES)],
                         axis=1) for b in range(bsz)], axis=0)
    y = _gelu_tanh(y + d_ref[...] * u_ref[...].reshape(bsz * tt, D_S5))
    g = jnp.dot(y.astype(BF16), w_ref[...], preferred_element_type=F32) + b_ref[...]
    o_ref[...] = _rms(y * _sigmoid(g), gain_ref[...]).reshape(bsz, tt, D_S5).astype(BF16)


def _s5glu(yc, u3, d_skip, wglu_all, layer, bglu, gain, tt=48):
    bsz, lp, _ = u3.shape
    return pl.pallas_call(
        _s5glu_kernel,
        grid=(lp // tt,),
        in_specs=[
            pl.BlockSpec((tt // S5_R * bsz, S5_GROUPS * S5_K), lambda i: (i, 0)),
            pl.BlockSpec((bsz, tt, D_S5), lambda i: (0, i, 0)),
            pl.BlockSpec((1, D_S5), lambda i: (0, 0)),
            pl.BlockSpec((None, D_S5, D_S5), lambda i: (layer, 0, 0)),
            pl.BlockSpec((1, D_S5), lambda i: (0, 0)),
            pl.BlockSpec((1, D_S5), lambda i: (0, 0)),
        ],
        out_specs=pl.BlockSpec((bsz, tt, D_S5), lambda i: (0, i, 0)),
        out_shape=jax.ShapeDtypeStruct((bsz, lp, D_S5), BF16),
        scratch_shapes=[pltpu.VMEM((D_S5 // LANES, bsz * (tt + 8), LANES), F32)],
        compiler_params=pltpu.CompilerParams(
            dimension_semantics=("arbitrary",), vmem_limit_bytes=VMEM_LIMIT),
        name="s5glu",
    )(yc, u3, d_skip.reshape(1, D_S5), wglu_all, bglu.reshape(1, D_S5), gain.reshape(1, D_S5))


def _s5_params(lam_re, lam_im, log_step, b_re, b_im, c_re, c_im):
    hp = lax.Precision.HIGHEST
    a_re = jnp.minimum(lam_re.astype(F32), -1e-4)
    a_im = lam_im.astype(F32)
    dt = jnp.exp(log_step.astype(F32))[:, None]
    mag = jnp.exp(a_re * dt)
    ab_re = mag * jnp.cos(a_im * dt)
    ab_im = mag * jnp.sin(a_im * dt)
    den = a_re * a_re + a_im * a_im
    x_re, x_im = ab_re - 1.0, ab_im
    z_re = (x_re * a_re + x_im * a_im) / den
    z_im = (x_im * a_re - x_re * a_im) / den
    br, bi = b_re.astype(F32), b_im.astype(F32)
    bb_re = z_re[..., None] * br - z_im[..., None] * bi
    bb_im = z_re[..., None] * bi + z_im[..., None] * br
    cr, ci = c_re.astype(F32), c_im.astype(F32)

    pw_re, pw_im = [jnp.ones_like(ab_re)], [jnp.zeros_like(ab_re)]
    for _ in range(S5_R):
        r, i = pw_re[-1], pw_im[-1]
        pw_re.append(r * ab_re - i * ab_im)
        pw_im.append(r * ab_im + i * ab_re)
    p_re, p_im = jnp.stack(pw_re), jnp.stack(pw_im)

    q_re, q_im = p_re[:S5_R, :, :, None], p_im[:S5_R, :, :, None]
    m_re = q_re * bb_re - q_im * bb_im
    m_im = q_re * bb_im + q_im * bb_re
    kern = (jnp.einsum('gop,lgph->ghlo', cr, m_re, precision=hp)
            - jnp.einsum('gop,lgph->ghlo', ci, m_im, precision=hp))
    kz = jnp.concatenate([jnp.zeros((S5_GROUPS, S5_GROUP, S5_K), F32),
                          kern.reshape(S5_GROUPS, S5_GROUP, S5_K)], axis=-1)
    tmat = jnp.stack([kz[:, :, (S5_R - s) * S5_GROUP:(S5_R - s) * S5_GROUP + S5_K]
                      for s in range(S5_R)], axis=1).reshape(S5_PAIRS, 2, S5_K, S5_K)

    bt_re, bt_im = bb_re.transpose(0, 2, 1)[:, None], bb_im.transpose(0, 2, 1)[:, None]
    rp_re = jnp.moveaxis(p_re[S5_R - 1::-1], 0, 1)[:, :, None, :]
    rp_im = jnp.moveaxis(p_im[S5_R - 1::-1], 0, 1)[:, :, None, :]
    h_re = (rp_re * bt_re - rp_im * bt_im).reshape(S5_PAIRS, 2, S5_K, S5_STATE)
    h_im = (rp_re * bt_im + rp_im * bt_re).reshape(S5_PAIRS, 2, S5_K, S5_STATE)
    zh = jnp.zeros_like(h_re[:, 0])
    hmat = jnp.concatenate([
        jnp.concatenate([h_re[:, 0], zh, h_im[:, 0], zh], axis=-1),
        jnp.concatenate([zh, h_re[:, 1], zh, h_im[:, 1]], axis=-1)], axis=1)

    ct_re, ct_im = cr.transpose(0, 2, 1)[:, :, None, :], ci.transpose(0, 2, 1)[:, :, None, :]
    e_re = jnp.moveaxis(p_re[1:], 0, 2)[..., None]
    e_im = jnp.moveaxis(p_im[1:], 0, 2)[..., None]
    g_re = (ct_re * e_re - ct_im * e_im).reshape(S5_PAIRS, 2, S5_STATE, S5_K)
    g_im = -(ct_re * e_im + ct_im * e_re).reshape(S5_PAIRS, 2, S5_STATE, S5_K)
    zg = jnp.zeros_like(g_re[:, 0])
    gmat = jnp.concatenate([
        jnp.concatenate([g_re[:, 0], zg], axis=-1), jnp.concatenate([zg, g_re[:, 1]], axis=-1),
        jnp.concatenate([g_im[:, 0], zg], axis=-1), jnp.concatenate([zg, g_im[:, 1]], axis=-1)],
        axis=1)

    a2 = jnp.stack([p_re[S5_R].reshape(S5_PAIRS, 2 * S5_STATE),
                    p_im[S5_R].reshape(S5_PAIRS, 2 * S5_STATE)], axis=1)
    return tmat.astype(BF16), hmat.astype(BF16), gmat.astype(BF16), a2


def _s5(u3, params, layer, d_skip, wglu_all, bglu, gain):
    yc = _s5core(_to_chunks(u3), params, layer, u3.shape[0])
    return _s5glu(yc, u3, d_skip, wglu_all, layer, bglu, gain)


def _outproj_kernel(h_ref, pre_ref, a_ref, b_ref, w_ref, o_ref, *, first):
    y = (jnp.dot(a_ref[...], w_ref[0], preferred_element_type=F32)
         + jnp.dot(b_ref[...], w_ref[1], preferred_element_type=F32))
    if first:
        npre = pre_ref.shape[0]

        @pl.when(pl.program_id(1) == 0)
        def _():
            o_ref[0:npre, :] = y[0:npre, :] + pre_ref[...]
            o_ref[npre:, :] = y[npre:, :] + h_ref[0:h_ref.shape[0] - npre, :]

        @pl.when(pl.program_id(1) != 0)
        def _():
            o_ref[...] = y + h_ref[...]
    else:
        o_ref[...] = y + h_ref[...]


def _outproj(h, prefix, a, b, w_all, layer, first, bsz, tm=704):
    k = a.shape[1]
    d = h.shape[1]
    lp = a.shape[0] // bsz
    tiles = lp // tm
    flat = lambda b, j: (b * tiles + j, 0)
    return pl.pallas_call(
        functools.partial(_outproj_kernel, first=first),
        grid=(bsz, tiles),
        in_specs=[
            _row_tile_specs(first, bsz, lp, tm, d),
            pl.BlockSpec((CHUNK, d), lambda b, j: (0, 0)),
            pl.BlockSpec((tm, k), flat),
            pl.BlockSpec((tm, k), flat),
            pl.BlockSpec((None, 2, k, d), lambda b, j: (layer, 0, 0, 0)),
        ],
        out_specs=pl.BlockSpec((tm, d), flat),
        out_shape=jax.ShapeDtypeStruct((bsz * lp, d), F32),
        compiler_params=pltpu.CompilerParams(
            dimension_semantics=("arbitrary", "arbitrary"), vmem_limit_bytes=VMEM_LIMIT),
        name="outproj",
    )(h, prefix, a, b, w_all)


FFN_HALO = 16


def _ffn_kernel(h_ref, halo_ref, g_ref, wg_ref, wu_ref, cw_ref, cb_ref, wd_ref, fg_ref, o_ref,
                hn_ref, *, final):
    f_axis = 2 if final else 1
    f = pl.program_id(f_axis)

    def block():
        a = jnp.dot(hn_ref[...], wg_ref[...], preferred_element_type=F32)
        up = jnp.dot(hn_ref[FFN_HALO:, :], wu_ref[...], preferred_element_type=F32)
        cw = cw_ref[...]
        conv = (cb_ref[...]
                + cw[0:1, :] * pltpu.roll(a, 2, 0)[FFN_HALO:, :]
                + cw[1:2, :] * pltpu.roll(a, 1, 0)[FFN_HALO:, :]
                + cw[2:3, :] * a[FFN_HALO:, :])
        hid = (_silu(conv) * up).astype(BF16)
        return jnp.dot(hid, wd_ref[...], preferred_element_type=F32)

    @pl.when(f == 0)
    def _():
        x = h_ref[...]
        hn_ref[FFN_HALO:, :] = _rms(x, g_ref[...]).astype(BF16)
        hn_ref[0:FFN_HALO, :] = _rms(halo_ref[...], g_ref[...]).astype(BF16)
        o_ref[...] = x + block()

    @pl.when(f > 0)
    def _():
        o_ref[...] += block()

    if final:
        @pl.when(f == pl.num_programs(f_axis) - 1)
        def _():
            o_ref[...] = _rms(o_ref[...], fg_ref[...])


def _ffn(h, g, layer, wg_all, wu_all, cw_all, cb_all, wd_all, fg, final, bsz, tm=512, tf=512):
    t, d = h.shape
    ff = wg_all.shape[2]
    hb = tm // FFN_HALO
    lp = t // bsz
    if final:
        grid = (bsz, (lp - CHUNK) // tm, ff // tf)
        row0 = lambda b, j, back: pl.multiple_of(b * lp + CHUNK + j * tm - back, FFN_HALO)
        row_specs = [
            pl.BlockSpec((pl.Element(tm), pl.Element(d)), lambda b, j, f: (row0(b, j, 0), 0)),
            pl.BlockSpec((pl.Element(FFN_HALO), pl.Element(d)),
                         lambda b, j, f: (row0(b, j, FFN_HALO), 0)),
        ]
        out_spec = pl.BlockSpec((tm, d), lambda b, j, f: (b * ((lp - CHUNK) // tm) + j, 0))
        out_rows = bsz * (lp - CHUNK)
        wmap = lambda fn: (lambda b, j, f: fn(f))
    else:
        grid = (t // tm, ff // tf)
        row_specs = [
            pl.BlockSpec((tm, d), lambda i, f: (i, 0)),
            pl.BlockSpec((FFN_HALO, d), lambda i, f: (jnp.maximum(i * hb - 1, 0), 0)),
        ]
        out_spec = pl.BlockSpec((tm, d), lambda i, f: (i, 0))
        out_rows = t
        wmap = lambda fn: (lambda i, f: fn(f))
    return pl.pallas_call(
        functools.partial(_ffn_kernel, final=final),
        grid=grid,
        in_specs=row_specs + [
            pl.BlockSpec((1, d), wmap(lambda f: (0, 0))),
            pl.BlockSpec((None, d, tf), wmap(lambda f: (layer, 0, f))),
            pl.BlockSpec((None, d, tf), wmap(lambda f: (layer, 0, f))),
            pl.BlockSpec((None, 3, tf), wmap(lambda f: (layer, 0, f))),
            pl.BlockSpec((None, 1, tf), wmap(lambda f: (layer, 0, f))),
            pl.BlockSpec((None, tf, d), wmap(lambda f: (layer, f, 0))),
            pl.BlockSpec((1, d), wmap(lambda f: (0, 0))),
        ],
        out_specs=out_spec,
        out_shape=jax.ShapeDtypeStruct((out_rows, d), F32),
        scratch_shapes=[pltpu.VMEM((tm + FFN_HALO, d), BF16)],
        compiler_params=pltpu.CompilerParams(
            dimension_semantics=("arbitrary",) * len(grid), vmem_limit_bytes=VMEM_LIMIT),
        name="ffn_final" if final else "ffn",
    )(h, h, g.reshape(1, d), wg_all, wu_all, cw_all, cb_all.reshape(cb_all.shape[0], 1, ff),
      wd_all, fg.reshape(1, d))


def kernel(x, meta_tokens, lb_logits, norm_mix, w_in, hg_norm, s5_lambda_re, s5_lambda_im,
           s5_log_step, s5_b_re, s5_b_im, s5_c_re, s5_c_im, s5_d, w_glu, b_glu, s5_norm, w_out,
           norm_ffn, w_ffn_gate, w_ffn_up, ffn_conv_w, ffn_conv_b, w_ffn_down, final_norm):
    bsz, seq, d = x.shape
    lp = seq + CHUNK
    prefix = jnp.concatenate([jnp.zeros((SEQ_PAD, d), x.dtype), meta_tokens.astype(x.dtype)], axis=0)
    h = x.reshape(bsz * seq, d)

    sm = jax.nn.softmax(lb_logits.astype(F32), axis=0)
    lb_all = jnp.cumsum(sm, axis=0) - sm[0:1]

    w_in_b = w_in.astype(BF16)
    w_out_b = w_out.astype(BF16).reshape(DEPTH, 2, D_HG, d)
    w_glu_b = w_glu.astype(BF16)
    wg_b, wu_b, wd_b = (w.astype(BF16) for w in (w_ffn_gate, w_ffn_up, w_ffn_down))
    s5_params = jax.vmap(_s5_params)(s5_lambda_re, s5_lambda_im, s5_log_step,
                                     s5_b_re, s5_b_im, s5_c_re, s5_c_im)

    for l in range(DEPTH):
        o_hg, u3 = _proj_hgrn2(h, prefix, norm_mix[l], w_in_b, l, lb_all[l], hg_norm[l],
                               first=(l == 0), bsz=bsz)
        o_s5 = _s5(u3, s5_params, l, s5_d[l], w_glu_b, b_glu[l], s5_norm[l])
        h = _outproj(h, prefix, o_hg.reshape(bsz * lp, D_HG), o_s5.reshape(bsz * lp, D_S5),
                     w_out_b, l, first=(l == 0), bsz=bsz)
        h = _ffn(h, norm_ffn[l], l, wg_b, wu_b, ffn_conv_w, ffn_conv_b, wd_b, final_norm,
                 final=(l == DEPTH - 1), bsz=bsz, tm=512 if l == DEPTH - 1 else 768)
    return h.reshape(bsz, seq, d)
```

```python
import functools

import jax
import jax.numpy as jnp
from jax import lax
from jax.experimental import pallas as pl
from jax.experimental.pallas import tpu as pltpu

F32 = jnp.float32
BF16 = jnp.bfloat16
LANES = 128

DEPTH = 2
CHUNK = 64
N_META = 16
D_HG = 1024
HG_HEADS = 8
HG_DK = 128
D_S5 = 1024
S5_GROUP = 16
S5_GROUPS = 64
S5_STATE = 64
D_IN = 4 * D_HG + D_S5
EPS = 1e-6
F_FLOOR = 1e-6

SEQ_PAD = CHUNK - N_META
VMEM_LIMIT = 56 * 1024 * 1024
VMEM_LIMIT_BIG = 62 * 1024 * 1024

S5_R = 16
S5_K = S5_R * S5_GROUP
S5_PAIRS = S5_GROUPS // 2


def _rms(x, g):
    ms = jnp.mean(x * x, axis=-1, keepdims=True)
    return x * lax.rsqrt(ms + EPS) * g


def _sigmoid(x):
    return 0.5 + 0.5 * jnp.tanh(0.5 * x)


def _silu(x):
    half = 0.5 * x
    return half + half * jnp.tanh(half)


def _normed_tile(h_ref, pre_ref, g_ref, xn_ref, first):
    if first:
        npre = pre_ref.shape[0]

        @pl.when(pl.program_id(1) == 0)
        def _():
            xn_ref[0:npre, :] = _rms(pre_ref[...], g_ref[...]).astype(BF16)
            xn_ref[npre:, :] = _rms(h_ref[0:h_ref.shape[0] - npre, :], g_ref[...]).astype(BF16)

        @pl.when(pl.program_id(1) != 0)
        def _():
            xn_ref[...] = _rms(h_ref[...], g_ref[...]).astype(BF16)
    else:
        xn_ref[...] = _rms(h_ref[...], g_ref[...]).astype(BF16)


def _row_tile_specs(first, bsz, lp, tm, d):
    tiles = lp // tm
    if first:
        def rows(b, j, *_):
            return (pl.multiple_of(b * (lp - CHUNK) + jnp.maximum(j * tm - CHUNK, 0), 8), 0)
        return pl.BlockSpec((pl.Element(tm), pl.Element(d)), rows)
    return pl.BlockSpec((tm, d), lambda b, j, *_: (b * tiles + j, 0))


def _pick_rows(x, m, first):
    j = lax.broadcasted_iota(jnp.int32, x.shape, 0)
    out = None
    for blk in range(8 // m - 1, -1, -1):
        r = blk * m + (0 if first else m - 1)
        row = jnp.broadcast_to(x[r:r + 1, :], x.shape)
        out = row if out is None else jnp.where(j < (blk + 1) * m, row, out)
    return out


def _expand_rows(ref, i, x):
    nv = x.shape[1] // LANES
    for v in range(nv):
        ref[i, v] = x[:, v * LANES:(v + 1) * LANES]
    return jnp.concatenate(
        [jnp.concatenate([ref[i, v, pl.ds(j, 8, stride=0), :] for v in range(nv)], axis=1)
         for j in range(8)], axis=0)


def _dot_nt(a, b):
    return lax.dot_general(a, b, (((1,), (1,)), ((), ())), preferred_element_type=F32)


def _dot_tn(a, b):
    return lax.dot_general(a, b, (((0,), (0,)), ((), ())), preferred_element_type=F32)


def _proj_hgrn2_kernel(h_ref, pre_ref, g_ref, w_ref, lb_ref, gain_ref, o_ref, u_ref,
                       xn_ref, p_ref, st_ref, bc_ref, *, first):
    @pl.when(pl.program_id(1) == 0)
    def _():
        st_ref[...] = jnp.zeros_like(st_ref)

    _normed_tile(h_ref, pre_ref, g_ref, xn_ref, first)
    xn = xn_ref[...]
    for k in range(4 * D_HG // D_S5):
        cols = slice(k * D_S5, (k + 1) * D_S5)
        p_ref[:, cols] = jnp.dot(xn, w_ref[:, cols], preferred_element_type=F32)

    nsel = CHUNK + 16
    row = lax.broadcasted_iota(jnp.int32, (nsel, 3 * CHUNK), 0)
    col = lax.broadcasted_iota(jnp.int32, (nsel, 3 * CHUNK), 1) & (CHUNK - 1)
    tgt = jnp.where(row < CHUNK, row,
                    jnp.where(row < CHUNK + 8, (row - CHUNK) * 8 + 7, (row - CHUNK - 8) * 8 + 3))
    tri3 = (tgt >= col).astype(BF16)
    ti = lax.broadcasted_iota(jnp.int32, (CHUNK, CHUNK), 0)
    si = lax.broadcasted_iota(jnp.int32, (CHUNK, CHUNK), 1)
    masks = {}
    for w, sh in ((32, 5), (16, 4), (8, 3)):
        masks[w] = ((ti >> sh) == (si >> sh) + 1) & ((ti >> (sh + 1)) == (si >> (sh + 1)))
    masks[0] = ((ti >> 3) == (si >> 3)) & (si <= ti)

    u_blocks = 4
    u_cols = D_S5 // u_blocks
    for c in range(p_ref.shape[0] // CHUNK):
        if c % 2 == 0 and c // 2 < u_blocks:
            j = c // 2
            cols = slice(j * u_cols, (j + 1) * u_cols)
            u_ref[:, cols] = jnp.dot(xn, w_ref[:, 4 * D_HG + j * u_cols:4 * D_HG + (j + 1) * u_cols],
                                     preferred_element_type=F32)
        _hgrn2_chunk(p_ref, lb_ref, gain_ref, o_ref, st_ref, bc_ref, tri3, masks,
                     pl.ds(c * CHUNK, CHUNK))


def _hgrn2_chunk(p_ref, lb_ref, gain_ref, o_ref, st_ref, bc_ref, tri3, masks, rows):
    q = p_ref[rows, 0:D_HG]
    z = p_ref[rows, D_HG:2 * D_HG]
    v = p_ref[rows, 2 * D_HG:3 * D_HG].astype(BF16)
    lb = lb_ref[...]

    th = 0.5 * jnp.tanh(0.5 * z)
    gl = jnp.log2(jnp.maximum(lb + (1.0 - lb) * (0.5 + th), F_FLOOR))
    kk = (1.0 - lb) * (0.5 - th)
    qf = _silu(q)

    hi = gl.astype(BF16)
    r1 = gl - hi.astype(F32)
    mid = r1.astype(BF16)
    lo = (r1 - mid.astype(F32)).astype(BF16)
    cum = jnp.dot(tri3, jnp.concatenate([hi, mid, lo], axis=0), preferred_element_type=F32)
    b = cum[0:CHUNK]
    l8 = cum[CHUNK:CHUNK + 8]
    m8 = cum[CHUNK + 8:CHUNK + 16]
    j8 = lax.broadcasted_iota(jnp.int32, l8.shape, 0)
    s8 = jnp.where(j8 == 0, 0.0, pltpu.roll(l8, 1, 0))

    qe = qf * jnp.exp2(b - _expand_rows(bc_ref, 0, s8))
    kf = kk * jnp.exp2(_expand_rows(bc_ref, 1, l8) - b)
    qh = {8: qe.astype(BF16)}
    kh = {8: kf.astype(BF16)}
    for lvl, (w, m) in enumerate(((16, 2), (32, 4), (64, 8))):
        cq = jnp.exp2(s8 - _pick_rows(s8, m, True))
        ck = jnp.exp2(_pick_rows(l8, m, False) - l8)
        qh[w] = (qe * _expand_rows(bc_ref, 3 + 2 * lvl, cq)).astype(BF16)
        kh[w] = (kf * _expand_rows(bc_ref, 4 + 2 * lvl, ck)).astype(BF16)
    b_mid = _expand_rows(bc_ref, 2, m8)
    qh[0] = (qf * jnp.exp2(b - b_mid)).astype(BF16)
    kh[0] = (kk * jnp.exp2(b_mid - b)).astype(BF16)
    dec_all = jnp.exp2(l8[7:8, :])

    gate = p_ref[rows, 3 * D_HG:4 * D_HG]
    gate = _silu(gate)
    gain = gain_ref[...]

    heads = [slice(h * HG_DK, (h + 1) * HG_DK) for h in range(HG_HEADS)]
    sts = [st_ref[h] for h in range(HG_HEADS)]
    o_inter = [_dot_nt(qh[64][:, sl], st.astype(BF16)) for sl, st in zip(heads, sts)]
    scores = []
    for sl in heads:
        sc = jnp.zeros((CHUNK, CHUNK), F32)
        for w in (32, 16, 8, 0):
            sc = jnp.where(masks[w], _dot_nt(qh[w][:, sl], kh[w][:, sl]), sc)
        scores.append(sc.astype(BF16))
    outs = [oi + jnp.dot(sc, v[:, sl], preferred_element_type=F32)
            for oi, sc, sl in zip(o_inter, scores, heads)]
    for h, (sl, st) in enumerate(zip(heads, sts)):
        st_ref[h] = dec_all[:, sl] * st + _dot_tn(v[:, sl], kh[64][:, sl])
    for o, sl in zip(outs, heads):
        ms = jnp.mean(o * o, axis=-1, keepdims=True)
        o = o * lax.rsqrt(ms + EPS) * gain[:, sl]
        o_ref[rows, sl] = (o * gate[:, sl]).astype(BF16)


def _proj_hgrn2(h, prefix, g, w_all, layer, lb, gain, first, bsz, chunks_per_step=11):
    d = h.shape[1]
    lp = h.shape[0] // bsz + (CHUNK if first else 0)
    rows = chunks_per_step * CHUNK
    const = lambda b, c: (0, 0)
    return pl.pallas_call(
        functools.partial(_proj_hgrn2_kernel, first=first),
        grid=(bsz, lp // rows),
        in_specs=[
            _row_tile_specs(first, bsz, lp, rows, d),
            pl.BlockSpec((CHUNK, d), const),
            pl.BlockSpec((1, d), const),
            pl.BlockSpec((None, d, D_IN), lambda b, c: (layer, 0, 0),
                         pipeline_mode=pl.Buffered(1)),
            pl.BlockSpec((1, D_HG), const),
            pl.BlockSpec((1, D_HG), const),
        ],
        out_specs=[pl.BlockSpec((None, rows, D_HG), lambda b, c: (b, c, 0)),
                   pl.BlockSpec((None, rows, D_S5), lambda b, c: (b, c, 0))],
        out_shape=[jax.ShapeDtypeStruct((bsz, lp, D_HG), BF16),
                   jax.ShapeDtypeStruct((bsz, lp, D_S5), F32)],
        scratch_shapes=[
            pltpu.VMEM((rows, d), BF16),
            pltpu.VMEM((rows, 4 * D_HG), F32),
            pltpu.VMEM((HG_HEADS, HG_DK, HG_DK), F32),
            pltpu.VMEM((9, D_HG // LANES, 8, LANES), F32),
        ],
        compiler_params=pltpu.CompilerParams(
            dimension_semantics=("arbitrary", "arbitrary"), vmem_limit_bytes=VMEM_LIMIT_BIG),
        name="proj_hgrn2",
    )(h, prefix, g.reshape(1, d), w_all, lb.reshape(1, D_HG), gain.reshape(1, D_HG))


def _gelu_tanh(x):
    c = 0.7978845608028654
    half = 0.5 * x
    return half + half * jnp.tanh(x * (c + (c * 0.044715) * (x * x)))


def _s5core_kernel(u_ref, t_ref, h_ref, g_ref, a_ref, y_ref, v_ref, xp_ref, *, bsz):
    nchunks = u_ref.shape[0] // bsz
    u = u_ref[...]
    v_ref[...] = jnp.dot(u, h_ref[0], preferred_element_type=F32)
    a_re = jnp.broadcast_to(a_ref[0, 0:1, :], (bsz, LANES))
    a_im = jnp.broadcast_to(a_ref[0, 1:2, :], (bsz, LANES))

    y_in = [jnp.dot(u[:, k * S5_K:(k + 1) * S5_K], t_ref[0, k], preferred_element_type=F32)
            for k in range(2)]

    x_re = x_im = jnp.zeros((bsz, LANES), F32)
    for n in range(nchunks):
        rows = slice(n * bsz, (n + 1) * bsz)
        xp_ref[rows, 0:LANES] = x_re.astype(BF16)
        xp_ref[rows, LANES:2 * LANES] = x_im.astype(BF16)
        x_re, x_im = (a_re * x_re - a_im * x_im + v_ref[rows, 0:LANES],
                      a_re * x_im + a_im * x_re + v_ref[rows, LANES:2 * LANES])

    ys = jnp.dot(xp_ref[...], g_ref[0], preferred_element_type=F32)
    for k in range(2):
        cols = slice(k * S5_K, (k + 1) * S5_K)
        y_ref[:, cols] = ys[:, cols] + y_in[k]


def _s5core(uc, params, layer, bsz):
    tmat, hmat, gmat, a2 = params
    rows = uc.shape[0]
    return pl.pallas_call(
        functools.partial(_s5core_kernel, bsz=bsz),
        grid=(S5_PAIRS,),
        in_specs=[
            pl.BlockSpec((rows, 2 * S5_K), lambda p: (0, p)),
            pl.BlockSpec((None, 1, 2, S5_K, S5_K), lambda p: (layer, p, 0, 0, 0)),
            pl.BlockSpec((None, 1, 2 * S5_K, 4 * S5_STATE), lambda p: (layer, p, 0, 0)),
            pl.BlockSpec((None, 1, 4 * S5_STATE, 2 * S5_K), lambda p: (layer, p, 0, 0)),
            pl.BlockSpec((None, 1, 2, 2 * S5_STATE), lambda p: (layer, p, 0, 0)),
        ],
        out_specs=pl.BlockSpec((rows, 2 * S5_K), lambda p: (0, p)),
        out_shape=jax.ShapeDtypeStruct((rows, S5_GROUPS * S5_K), F32),
        scratch_shapes=[
            pltpu.VMEM((rows, 4 * S5_STATE), F32),
            pltpu.VMEM((rows, 4 * S5_STATE), BF16),
        ],
        compiler_params=pltpu.CompilerParams(
            dimension_semantics=("arbitrary",), vmem_limit_bytes=VMEM_LIMIT),
        name="s5core",
    )(uc, tmat, hmat, gmat, a2)


def _atom_transpose(groups):
    lane = lax.broadcasted_iota(jnp.int32, groups[0][0].shape, 1)
    groups = [list(xs) for xs in groups]
    for s in range(3):
        d = 1 << s
        keep = ((lane >> (4 + s)) & 1) == 0
        for xs in groups:
            for i in range(8):
                if i & d:
                    continue
                lo, hi = xs[i], xs[i + d]
                xs[i] = jnp.where(keep, lo, pltpu.roll(hi, S5_GROUP * d, 1))
                xs[i + d] = jnp.where(keep, pltpu.roll(lo, LANES - S5_GROUP * d, 1), hi)
    return groups


def _to_chunks_kernel(u_ref, o_ref, rows_ref):
    bsz, tt, width = u_ref.shape
    pitch = rows_ref.shape[1] // bsz
    for v in range(width // LANES):
        for b in range(bsz):
            rows_ref[v, b * pitch:b * pitch + tt, :] = u_ref[b, :, v * LANES:(v + 1) * LANES]
    halves = [(v, n, k) for v in range(width // LANES) for n in range(tt // S5_R) for k in range(2)]
    groups = _atom_transpose(
        [[rows_ref[v, pl.ds(n * S5_R + 8 * k + j, bsz, stride=pitch), :] for j in range(8)]
         for v, n, k in halves])
    for (v, n, k), ys in zip(halves, groups):
        for g, y in enumerate(ys):
            lane0 = (8 * v + g) * S5_K + k * LANES
            o_ref[n * bsz:(n + 1) * bsz, lane0:lane0 + LANES] = y.astype(BF16)


def _to_chunks(u3, tt=176, width=512):
    bsz, lp, _ = u3.shape
    return pl.pallas_call(
        _to_chunks_kernel,
        grid=(lp // tt, D_S5 // width),
        in_specs=[pl.BlockSpec((bsz, tt, width), lambda i, v: (0, i, v))],
        out_specs=pl.BlockSpec((tt, width // S5_GROUP * S5_K), lambda i, v: (i, v)),
        out_shape=jax.ShapeDtypeStruct((lp // S5_R * bsz, S5_GROUPS * S5_K), BF16),
        scratch_shapes=[pltpu.VMEM((width // LANES, bsz * (tt + 8), LANES), F32)],
        compiler_params=pltpu.CompilerParams(
            dimension_semantics=("arbitrary", "arbitrary"), vmem_limit_bytes=VMEM_LIMIT),
        name="s5_to_chunks",
    )(u3)


def _s5glu_kernel(yc_ref, u_ref, d_ref, w_ref, b_ref, gain_ref, o_ref, yn_ref):
    bsz, tt, _ = u_ref.shape
    pitch = yn_ref.shape[1] // bsz
    for n in range(tt // S5_R):
        halves = [(v, k) for v in range(D_S5 // LANES) for k in range(2)]
        groups = _atom_transpose(
            [[yc_ref[n * bsz:(n + 1) * bsz,
                     (8 * v + g) * S5_K + k * LANES:(8 * v + g) * S5_K + (k + 1) * LANES]
              for g in range(8)] for v, k in halves])
        for (v, k), xs in zip(halves, groups):
            for j, x in enumerate(xs):
                yn_ref[v, pl.ds(n * S5_R + 8 * k + j, bsz, stride=pitch), :] = x
    y = jnp.concatenate(
        [jnp.concatenate([yn_ref[v, b * pitch:b * pitch + tt, :] for v in range(D_S5 // LANES)],
                         axis=1) for b in range(bsz)], axis=0)
    y = _gelu_tanh(y + d_ref[...] * u_ref[...].reshape(bsz * tt, D_S5))
    g = jnp.dot(y.astype(BF16), w_ref[...], preferred_element_type=F32) + b_ref[...]
    o_ref[...] = _rms(y * _sigmoid(g), gain_ref[...]).reshape(bsz, tt, D_S5).astype(BF16)


def _s5glu(yc, u3, d_skip, wglu_all, layer, bglu, gain, tt=48):
    bsz, lp, _ = u3.shape
    return pl.pallas_call(
        _s5glu_kernel,
        grid=(lp // tt,),
        in_specs=[
            pl.BlockSpec((tt // S5_R * bsz, S5_GROUPS * S5_K), lambda i: (i, 0)),
            pl.BlockSpec((bsz, tt, D_S5), lambda i: (0, i, 0)),
            pl.BlockSpec((1, D_S5), lambda i: (0, 0)),
            pl.BlockSpec((None, D_S5, D_S5), lambda i: (layer, 0, 0)),
            pl.BlockSpec((1, D_S5), lambda i: (0, 0)),
            pl.BlockSpec((1, D_S5), lambda i: (0, 0)),
        ],
        out_specs=pl.BlockSpec((bsz, tt, D_S5), lambda i: (0, i, 0)),
        out_shape=jax.ShapeDtypeStruct((bsz, lp, D_S5), BF16),
        scratch_shapes=[pltpu.VMEM((D_S5 // LANES, bsz * (tt + 8), LANES), F32)],
        compiler_params=pltpu.CompilerParams(
            dimension_semantics=("arbitrary",), vmem_limit_bytes=VMEM_LIMIT),
        name="s5glu",
    )(yc, u3, d_skip.reshape(1, D_S5), wglu_all, bglu.reshape(1, D_S5), gain.reshape(1, D_S5))


def _s5_params(lam_re, lam_im, log_step, b_re, b_im, c_re, c_im):
    hp = lax.Precision.HIGHEST
    a_re = jnp.minimum(lam_re.astype(F32), -1e-4)
    a_im = lam_im.astype(F32)
    dt = jnp.exp(log_step.astype(F32))[:, None]
    mag = jnp.exp(a_re * dt)
    ab_re = mag * jnp.cos(a_im * dt)
    ab_im = mag * jnp.sin(a_im * dt)
    den = a_re * a_re + a_im * a_im
    x_re, x_im = ab_re - 1.0, ab_im
    z_re = (x_re * a_re + x_im * a_im) / den
    z_im = (x_im * a_re - x_re * a_im) / den
    br, bi = b_re.astype(F32), b_im.astype(F32)
    bb_re = z_re[..., None] * br - z_im[..., None] * bi
    bb_im = z_re[..., None] * bi + z_im[..., None] * br
    cr, ci = c_re.astype(F32), c_im.astype(F32)

    pw_re, pw_im = [jnp.ones_like(ab_re)], [jnp.zeros_like(ab_re)]
    for _ in range(S5_R):
        r, i = pw_re[-1], pw_im[-1]
        pw_re.append(r * ab_re - i * ab_im)
        pw_im.append(r * ab_im + i * ab_re)
    p_re, p_im = jnp.stack(pw_re), jnp.stack(pw_im)

    q_re, q_im = p_re[:S5_R, :, :, None], p_im[:S5_R, :, :, None]
    m_re = q_re * bb_re - q_im * bb_im
    m_im = q_re * bb_im + q_im * bb_re
    kern = (jnp.einsum('gop,lgph->ghlo', cr, m_re, precision=hp)
            - jnp.einsum('gop,lgph->ghlo', ci, m_im, precision=hp))
    kz = jnp.concatenate([jnp.zeros((S5_GROUPS, S5_GROUP, S5_K), F32),
                          kern.reshape(S5_GROUPS, S5_GROUP, S5_K)], axis=-1)
    tmat = jnp.stack([kz[:, :, (S5_R - s) * S5_GROUP:(S5_R - s) * S5_GROUP + S5_K]
                      for s in range(S5_R)], axis=1).reshape(S5_PAIRS, 2, S5_K, S5_K)

    bt_re, bt_im = bb_re.transpose(0, 2, 1)[:, None], bb_im.transpose(0, 2, 1)[:, None]
    rp_re = jnp.moveaxis(p_re[S5_R - 1::-1], 0, 1)[:, :, None, :]
    rp_im = jnp.moveaxis(p_im[S5_R - 1::-1], 0, 1)[:, :, None, :]
    h_re = (rp_re * bt_re - rp_im * bt_im).reshape(S5_PAIRS, 2, S5_K, S5_STATE)
    h_im = (rp_re * bt_im + rp_im * bt_re).reshape(S5_PAIRS, 2, S5_K, S5_STATE)
    zh = jnp.zeros_like(h_re[:, 0])
    hmat = jnp.concatenate([
        jnp.concatenate([h_re[:, 0], zh, h_im[:, 0], zh], axis=-1),
        jnp.concatenate([zh, h_re[:, 1], zh, h_im[:, 1]], axis=-1)], axis=1)

    ct_re, ct_im = cr.transpose(0, 2, 1)[:, :, None, :], ci.transpose(0, 2, 1)[:, :, None, :]
    e_re = jnp.moveaxis(p_re[1:], 0, 2)[..., None]
    e_im = jnp.moveaxis(p_im[1:], 0, 2)[..., None]
    g_re = (ct_re * e_re - ct_im * e_im).reshape(S5_PAIRS, 2, S5_STATE, S5_K)
    g_im = -(ct_re * e_im + ct_im * e_re).reshape(S5_PAIRS, 2, S5_STATE, S5_K)
    zg = jnp.zeros_like(g_re[:, 0])
    gmat = jnp.concatenate([
        jnp.concatenate([g_re[:, 0], zg], axis=-1), jnp.concatenate([zg, g_re[:, 1]], axis=-1),
        jnp.concatenate([g_im[:, 0], zg], axis=-1), jnp.concatenate([zg, g_im[:, 1]], axis=-1)],
        axis=1)

    a2 = jnp.stack([p_re[S5_R].reshape(S5_PAIRS, 2 * S5_STATE),
                    p_im[S5_R].reshape(S5_PAIRS, 2 * S5_STATE)], axis=1)
    return tmat.astype(BF16), hmat.astype(BF16), gmat.astype(BF16), a2


def _s5(u3, params, layer, d_skip, wglu_all, bglu, gain):
    yc = _s5core(_to_chunks(u3), params, layer, u3.shape[0])
    return _s5glu(yc, u3, d_skip, wglu_all, layer, bglu, gain)


def _outproj_kernel(h_ref, pre_ref, a_ref, b_ref, w_ref, o_ref, *, first):
    y = (jnp.dot(a_ref[...], w_ref[0], preferred_element_type=F32)
         + jnp.dot(b_ref[...], w_ref[1], preferred_element_type=F32))
    if first:
        npre = pre_ref.shape[0]

        @pl.when(pl.program_id(1) == 0)
        def _():
            o_ref[0:npre, :] = y[0:npre, :] + pre_ref[...]
            o_ref[npre:, :] = y[npre:, :] + h_ref[0:h_ref.shape[0] - npre, :]

        @pl.when(pl.program_id(1) != 0)
        def _():
            o_ref[...] = y + h_ref[...]
    else:
        o_ref[...] = y + h_ref[...]


def _outproj(h, prefix, a, b, w_all, layer, first, bsz, tm=704):
    k = a.shape[1]
    d = h.shape[1]
    lp = a.shape[0] // bsz
    tiles = lp // tm
    flat = lambda b, j: (b * tiles + j, 0)
    return pl.pallas_call(
        functools.partial(_outproj_kernel, first=first),
        grid=(bsz, tiles),
        in_specs=[
            _row_tile_specs(first, bsz, lp, tm, d),
            pl.BlockSpec((CHUNK, d), lambda b, j: (0, 0)),
            pl.BlockSpec((tm, k), flat),
            pl.BlockSpec((tm, k), flat),
            pl.BlockSpec((None, 2, k, d), lambda b, j: (layer, 0, 0, 0)),
        ],
        out_specs=pl.BlockSpec((tm, d), flat),
        out_shape=jax.ShapeDtypeStruct((bsz * lp, d), F32),
        compiler_params=pltpu.CompilerParams(
            dimension_semantics=("arbitrary", "arbitrary"), vmem_limit_bytes=VMEM_LIMIT),
        name="outproj",
    )(h, prefix, a, b, w_all)


FFN_HALO = 16


def _ffn_kernel(h_ref, halo_ref, g_ref, wg_ref, wu_ref, cw_ref, cb_ref, wd_ref, fg_ref, o_ref,
                hn_ref, *, final, sub):
    f_axis = 2 if final else 1
    f = pl.program_id(f_axis)
    tm = o_ref.shape[0]

    def block(r):
        hn = hn_ref[r:r + sub + FFN_HALO, :]
        a = jnp.dot(hn, wg_ref[...], preferred_element_type=F32)
        up = jnp.dot(hn[FFN_HALO:, :], wu_ref[...], preferred_element_type=F32)
        cw = cw_ref[...]
        conv = (cb_ref[...]
                + cw[0:1, :] * pltpu.roll(a, 2, 0)[FFN_HALO:, :]
                + cw[1:2, :] * pltpu.roll(a, 1, 0)[FFN_HALO:, :]
                + cw[2:3, :] * a[FFN_HALO:, :])
        hid = (_silu(conv) * up).astype(BF16)
        return jnp.dot(hid, wd_ref[...], preferred_element_type=F32)

    @pl.when(f == 0)
    def _():
        hn_ref[0:FFN_HALO, :] = _rms(halo_ref[...], g_ref[...]).astype(BF16)
        for r in range(0, tm, sub):
            x = h_ref[r:r + sub, :]
            hn_ref[FFN_HALO + r:FFN_HALO + r + sub, :] = _rms(x, g_ref[...]).astype(BF16)
        for r in range(0, tm, sub):
            o_ref[r:r + sub, :] = h_ref[r:r + sub, :] + block(r)

    @pl.when(f > 0)
    def _():
        for r in range(0, tm, sub):
            o_ref[r:r + sub, :] += block(r)

    if final:
        @pl.when(f == pl.num_programs(f_axis) - 1)
        def _():
            o_ref[...] = _rms(o_ref[...], fg_ref[...])


def _ffn(h, g, layer, wg_all, wu_all, cw_all, cb_all, wd_all, fg, final, bsz, tm=512, tf=512,
         sub=None):
    t, d = h.shape
    ff = wg_all.shape[2]
    hb = tm // FFN_HALO
    lp = t // bsz
    if final:
        grid = (bsz, (lp - CHUNK) // tm, ff // tf)
        row0 = lambda b, j, back: pl.multiple_of(b * lp + CHUNK + j * tm - back, FFN_HALO)
        row_specs = [
            pl.BlockSpec((pl.Element(tm), pl.Element(d)), lambda b, j, f: (row0(b, j, 0), 0)),
            pl.BlockSpec((pl.Element(FFN_HALO), pl.Element(d)),
                         lambda b, j, f: (row0(b, j, FFN_HALO), 0)),
        ]
        out_spec = pl.BlockSpec((tm, d), lambda b, j, f: (b * ((lp - CHUNK) // tm) + j, 0))
        out_rows = bsz * (lp - CHUNK)
        wmap = lambda fn: (lambda b, j, f: fn(f))
    else:
        grid = (t // tm, ff // tf)
        row_specs = [
            pl.BlockSpec((tm, d), lambda i, f: (i, 0)),
            pl.BlockSpec((FFN_HALO, d), lambda i, f: (jnp.maximum(i * hb - 1, 0), 0)),
        ]
        out_spec = pl.BlockSpec((tm, d), lambda i, f: (i, 0))
        out_rows = t
        wmap = lambda fn: (lambda i, f: fn(f))
    return pl.pallas_call(
        functools.partial(_ffn_kernel, final=final, sub=sub or tm),
        grid=grid,
        in_specs=row_specs + [
            pl.BlockSpec((1, d), wmap(lambda f: (0, 0))),
            pl.BlockSpec((None, d, tf), wmap(lambda f: (layer, 0, f))),
            pl.BlockSpec((None, d, tf), wmap(lambda f: (layer, 0, f))),
            pl.BlockSpec((None, 3, tf), wmap(lambda f: (layer, 0, f))),
            pl.BlockSpec((None, 1, tf), wmap(lambda f: (layer, 0, f))),
            pl.BlockSpec((None, tf, d), wmap(lambda f: (layer, f, 0))),
            pl.BlockSpec((1, d), wmap(lambda f: (0, 0))),
        ],
        out_specs=out_spec,
        out_shape=jax.ShapeDtypeStruct((out_rows, d), F32),
        scratch_shapes=[pltpu.VMEM((tm + FFN_HALO, d), BF16)],
        compiler_params=pltpu.CompilerParams(
            dimension_semantics=("arbitrary",) * len(grid), vmem_limit_bytes=VMEM_LIMIT),
        name="ffn_final" if final else "ffn",
    )(h, h, g.reshape(1, d), wg_all, wu_all, cw_all, cb_all.reshape(cb_all.shape[0], 1, ff),
      wd_all, fg.reshape(1, d))


def kernel(x, meta_tokens, lb_logits, norm_mix, w_in, hg_norm, s5_lambda_re, s5_lambda_im,
           s5_log_step, s5_b_re, s5_b_im, s5_c_re, s5_c_im, s5_d, w_glu, b_glu, s5_norm, w_out,
           norm_ffn, w_ffn_gate, w_ffn_up, ffn_conv_w, ffn_conv_b, w_ffn_down, final_norm):
    bsz, seq, d = x.shape
    lp = seq + CHUNK
    prefix = jnp.concatenate([jnp.zeros((SEQ_PAD, d), x.dtype), meta_tokens.astype(x.dtype)], axis=0)
    h = x.reshape(bsz * seq, d)

    sm = jax.nn.softmax(lb_logits.astype(F32), axis=0)
    lb_all = jnp.cumsum(sm, axis=0) - sm[0:1]

    w_in_b = w_in.astype(BF16)
    w_out_b = w_out.astype(BF16).reshape(DEPTH, 2, D_HG, d)
    w_glu_b = w_glu.astype(BF16)
    wg_b, wu_b, wd_b = (w.astype(BF16) for w in (w_ffn_gate, w_ffn_up, w_ffn_down))
    s5_params = jax.vmap(_s5_params)(s5_lambda_re, s5_lambda_im, s5_log_step,
                                     s5_b_re, s5_b_im, s5_c_re, s5_c_im)

    for l in range(DEPTH):
        o_hg, u3 = _proj_hgrn2(h, prefix, norm_mix[l], w_in_b, l, lb_all[l], hg_norm[l],
                               first=(l == 0), bsz=bsz)
        o_s5 = _s5(u3, s5_params, l, s5_d[l], w_glu_b, b_glu[l], s5_norm[l])
        h = _outproj(h, prefix, o_hg.reshape(bsz * lp, D_HG), o_s5.reshape(bsz * lp, D_S5),
                     w_out_b, l, first=(l == 0), bsz=bsz)
        h = _ffn(h, norm_ffn[l], l, wg_b, wu_b, ffn_conv_w, ffn_conv_b, wd_b, final_norm,
                 final=(l == DEPTH - 1), bsz=bsz, **(dict(tm=1024, sub=512) if l == DEPTH - 1 else dict(tm=768)))
    return h.reshape(bsz, seq, d)
```

```python
import functools

import jax
import jax.numpy as jnp
from jax import lax
from jax.experimental import pallas as pl
from jax.experimental.pallas import tpu as pltpu

F32 = jnp.float32
BF16 = jnp.bfloat16
LANES = 128

DEPTH = 2
CHUNK = 64
N_META = 16
D_HG = 1024
HG_HEADS = 8
HG_DK = 128
D_S5 = 1024
S5_GROUP = 16
S5_GROUPS = 64
S5_STATE = 64
D_IN = 4 * D_HG + D_S5
EPS = 1e-6
F_FLOOR = 1e-6

SEQ_PAD = CHUNK - N_META
VMEM_LIMIT = 56 * 1024 * 1024
VMEM_LIMIT_BIG = 62 * 1024 * 1024

S5_R = 16
S5_K = S5_R * S5_GROUP
S5_PAIRS = S5_GROUPS // 2


def _rms(x, g):
    ms = jnp.mean(x * x, axis=-1, keepdims=True)
    return x * lax.rsqrt(ms + EPS) * g


def _sigmoid(x):
    return 0.5 + 0.5 * jnp.tanh(0.5 * x)


def _silu(x):
    half = 0.5 * x
    return half + half * jnp.tanh(half)


def _normed_tile(h_ref, pre_ref, g_ref, xn_ref, first):
    if first:
        npre = pre_ref.shape[0]

        @pl.when(pl.program_id(1) == 0)
        def _():
            xn_ref[0:npre, :] = _rms(pre_ref[...], g_ref[...]).astype(BF16)
            xn_ref[npre:, :] = _rms(h_ref[0:h_ref.shape[0] - npre, :], g_ref[...]).astype(BF16)

        @pl.when(pl.program_id(1) != 0)
        def _():
            xn_ref[...] = _rms(h_ref[...], g_ref[...]).astype(BF16)
    else:
        xn_ref[...] = _rms(h_ref[...], g_ref[...]).astype(BF16)


def _row_tile_specs(first, bsz, lp, tm, d):
    tiles = lp // tm
    if first:
        def rows(b, j, *_):
            return (pl.multiple_of(b * (lp - CHUNK) + jnp.maximum(j * tm - CHUNK, 0), 8), 0)
        return pl.BlockSpec((pl.Element(tm), pl.Element(d)), rows)
    return pl.BlockSpec((tm, d), lambda b, j, *_: (b * tiles + j, 0))


def _pick_rows(x, m, first):
    j = lax.broadcasted_iota(jnp.int32, x.shape, 0)
    out = None
    for blk in range(8 // m - 1, -1, -1):
        r = blk * m + (0 if first else m - 1)
        row = jnp.broadcast_to(x[r:r + 1, :], x.shape)
        out = row if out is None else jnp.where(j < (blk + 1) * m, row, out)
    return out


def _expand_rows(ref, i, x):
    nv = x.shape[1] // LANES
    for v in range(nv):
        ref[i, v] = x[:, v * LANES:(v + 1) * LANES]
    return jnp.concatenate(
        [jnp.concatenate([ref[i, v, pl.ds(j, 8, stride=0), :] for v in range(nv)], axis=1)
         for j in range(8)], axis=0)


def _dot_nt(a, b):
    return lax.dot_general(a, b, (((1,), (1,)), ((), ())), preferred_element_type=F32)


def _dot_tn(a, b):
    return lax.dot_general(a, b, (((0,), (0,)), ((), ())), preferred_element_type=F32)


def _proj_hgrn2_kernel(h_ref, pre_ref, g_ref, w_ref, lb_ref, gain_ref, o_ref, u_ref,
                       xn_ref, p_ref, st_ref, bc_ref, *, first):
    @pl.when(pl.program_id(1) == 0)
    def _():
        st_ref[...] = jnp.zeros_like(st_ref)

    _normed_tile(h_ref, pre_ref, g_ref, xn_ref, first)
    xn = xn_ref[...]
    for k in range(4 * D_HG // D_S5):
        cols = slice(k * D_S5, (k + 1) * D_S5)
        p_ref[:, cols] = jnp.dot(xn, w_ref[:, cols], preferred_element_type=F32)

    nsel = CHUNK + 16
    row = lax.broadcasted_iota(jnp.int32, (nsel, 3 * CHUNK), 0)
    col = lax.broadcasted_iota(jnp.int32, (nsel, 3 * CHUNK), 1) & (CHUNK - 1)
    tgt = jnp.where(row < CHUNK, row,
                    jnp.where(row < CHUNK + 8, (row - CHUNK) * 8 + 7, (row - CHUNK - 8) * 8 + 3))
    tri3 = (tgt >= col).astype(BF16)
    ti = lax.broadcasted_iota(jnp.int32, (CHUNK, CHUNK), 0)
    si = lax.broadcasted_iota(jnp.int32, (CHUNK, CHUNK), 1)
    masks = {}
    for w, sh in ((32, 5), (16, 4), (8, 3)):
        masks[w] = ((ti >> sh) == (si >> sh) + 1) & ((ti >> (sh + 1)) == (si >> (sh + 1)))
    masks[0] = ((ti >> 3) == (si >> 3)) & (si <= ti)

    u_blocks = 4
    u_cols = D_S5 // u_blocks
    for c in range(p_ref.shape[0] // CHUNK):
        if c % 2 == 0 and c // 2 < u_blocks:
            j = c // 2
            cols = slice(j * u_cols, (j + 1) * u_cols)
            u_ref[:, cols] = jnp.dot(xn, w_ref[:, 4 * D_HG + j * u_cols:4 * D_HG + (j + 1) * u_cols],
                                     preferred_element_type=F32)
        _hgrn2_chunk(p_ref, lb_ref, gain_ref, o_ref, st_ref, bc_ref, tri3, masks,
                     pl.ds(c * CHUNK, CHUNK))


def _hgrn2_chunk(p_ref, lb_ref, gain_ref, o_ref, st_ref, bc_ref, tri3, masks, rows):
    q = p_ref[rows, 0:D_HG]
    z = p_ref[rows, D_HG:2 * D_HG]
    v = p_ref[rows, 2 * D_HG:3 * D_HG].astype(BF16)
    lb = lb_ref[...]

    th = 0.5 * jnp.tanh(0.5 * z)
    gl = jnp.log2(jnp.maximum(lb + (1.0 - lb) * (0.5 + th), F_FLOOR))
    kk = (1.0 - lb) * (0.5 - th)
    qf = _silu(q)

    hi = gl.astype(BF16)
    r1 = gl - hi.astype(F32)
    mid = r1.astype(BF16)
    lo = (r1 - mid.astype(F32)).astype(BF16)
    cum = jnp.dot(tri3, jnp.concatenate([hi, mid, lo], axis=0), preferred_element_type=F32)
    b = cum[0:CHUNK]
    l8 = cum[CHUNK:CHUNK + 8]
    m8 = cum[CHUNK + 8:CHUNK + 16]
    j8 = lax.broadcasted_iota(jnp.int32, l8.shape, 0)
    s8 = jnp.where(j8 == 0, 0.0, pltpu.roll(l8, 1, 0))

    qe = qf * jnp.exp2(b - _expand_rows(bc_ref, 0, s8))
    kf = kk * jnp.exp2(_expand_rows(bc_ref, 1, l8) - b)
    qh = {8: qe.astype(BF16)}
    kh = {8: kf.astype(BF16)}
    for lvl, (w, m) in enumerate(((16, 2), (32, 4), (64, 8))):
        cq = jnp.exp2(s8 - _pick_rows(s8, m, True))
        ck = jnp.exp2(_pick_rows(l8, m, False) - l8)
        qh[w] = (qe * _expand_rows(bc_ref, 3 + 2 * lvl, cq)).astype(BF16)
        kh[w] = (kf * _expand_rows(bc_ref, 4 + 2 * lvl, ck)).astype(BF16)
    b_mid = _expand_rows(bc_ref, 2, m8)
    qh[0] = (qf * jnp.exp2(b - b_mid)).astype(BF16)
    kh[0] = (kk * jnp.exp2(b_mid - b)).astype(BF16)
    dec_all = jnp.exp2(l8[7:8, :])

    gate = p_ref[rows, 3 * D_HG:4 * D_HG]
    gate = _silu(gate)
    gain = gain_ref[...]

    heads = [slice(h * HG_DK, (h + 1) * HG_DK) for h in range(HG_HEADS)]
    sts = [st_ref[h] for h in range(HG_HEADS)]
    o_inter = [_dot_nt(qh[64][:, sl], st.astype(BF16)) for sl, st in zip(heads, sts)]
    scores = []
    for sl in heads:
        sc = jnp.zeros((CHUNK, CHUNK), F32)
        for w in (32, 16, 8, 0):
            sc = jnp.where(masks[w], _dot_nt(qh[w][:, sl], kh[w][:, sl]), sc)
        scores.append(sc.astype(BF16))
    outs = [oi + jnp.dot(sc, v[:, sl], preferred_element_type=F32)
            for oi, sc, sl in zip(o_inter, scores, heads)]
    for h, (sl, st) in enumerate(zip(heads, sts)):
        st_ref[h] = dec_all[:, sl] * st + _dot_tn(v[:, sl], kh[64][:, sl])
    for o, sl in zip(outs, heads):
        ms = jnp.mean(o * o, axis=-1, keepdims=True)
        o = o * lax.rsqrt(ms + EPS) * gain[:, sl]
        o_ref[rows, sl] = (o * gate[:, sl]).astype(BF16)


def _proj_hgrn2(h, prefix, g, w_all, layer, lb, gain, first, bsz, chunks_per_step=11):
    d = h.shape[1]
    lp = h.shape[0] // bsz + (CHUNK if first else 0)
    rows = chunks_per_step * CHUNK
    const = lambda b, c: (0, 0)
    return pl.pallas_call(
        functools.partial(_proj_hgrn2_kernel, first=first),
        grid=(bsz, lp // rows),
        in_specs=[
            _row_tile_specs(first, bsz, lp, rows, d),
            pl.BlockSpec((CHUNK, d), const),
            pl.BlockSpec((1, d), const),
            pl.BlockSpec((None, d, D_IN), lambda b, c: (layer, 0, 0),
                         pipeline_mode=pl.Buffered(1)),
            pl.BlockSpec((1, D_HG), const),
            pl.BlockSpec((1, D_HG), const),
        ],
        out_specs=[pl.BlockSpec((None, rows, D_HG), lambda b, c: (b, c, 0)),
                   pl.BlockSpec((None, rows, D_S5), lambda b, c: (b, c, 0))],
        out_shape=[jax.ShapeDtypeStruct((bsz, lp, D_HG), BF16),
                   jax.ShapeDtypeStruct((bsz, lp, D_S5), F32)],
        scratch_shapes=[
            pltpu.VMEM((rows, d), BF16),
            pltpu.VMEM((rows, 4 * D_HG), F32),
            pltpu.VMEM((HG_HEADS, HG_DK, HG_DK), F32),
            pltpu.VMEM((9, D_HG // LANES, 8, LANES), F32),
        ],
        compiler_params=pltpu.CompilerParams(
            dimension_semantics=("arbitrary", "arbitrary"), vmem_limit_bytes=VMEM_LIMIT_BIG),
        name="proj_hgrn2",
    )(h, prefix, g.reshape(1, d), w_all, lb.reshape(1, D_HG), gain.reshape(1, D_HG))


def _gelu_tanh(x):
    c = 0.7978845608028654
    half = 0.5 * x
    return half + half * jnp.tanh(x * (c + (c * 0.044715) * (x * x)))


def _s5core_kernel(u_ref, t_ref, h_ref, g_ref, a_ref, y_ref, v_ref, xp_ref, *, bsz):
    nchunks = u_ref.shape[0] // bsz
    u = u_ref[...]
    v_ref[...] = jnp.dot(u, h_ref[0], preferred_element_type=F32)
    a_re = jnp.broadcast_to(a_ref[0, 0:1, :], (bsz, LANES))
    a_im = jnp.broadcast_to(a_ref[0, 1:2, :], (bsz, LANES))

    y_in = [jnp.dot(u[:, k * S5_K:(k + 1) * S5_K], t_ref[0, k], preferred_element_type=F32)
            for k in range(2)]

    x_re = x_im = jnp.zeros((bsz, LANES), F32)
    for n in range(nchunks):
        rows = slice(n * bsz, (n + 1) * bsz)
        xp_ref[rows, 0:LANES] = x_re.astype(BF16)
        xp_ref[rows, LANES:2 * LANES] = x_im.astype(BF16)
        x_re, x_im = (a_re * x_re - a_im * x_im + v_ref[rows, 0:LANES],
                      a_re * x_im + a_im * x_re + v_ref[rows, LANES:2 * LANES])

    ys = jnp.dot(xp_ref[...], g_ref[0], preferred_element_type=F32)
    for k in range(2):
        cols = slice(k * S5_K, (k + 1) * S5_K)
        y_ref[:, cols] = ys[:, cols] + y_in[k]


def _s5core(uc, params, layer, bsz):
    tmat, hmat, gmat, a2 = params
    rows = uc.shape[0]
    return pl.pallas_call(
        functools.partial(_s5core_kernel, bsz=bsz),
        grid=(S5_PAIRS,),
        in_specs=[
            pl.BlockSpec((rows, 2 * S5_K), lambda p: (0, p)),
            pl.BlockSpec((None, 1, 2, S5_K, S5_K), lambda p: (layer, p, 0, 0, 0)),
            pl.BlockSpec((None, 1, 2 * S5_K, 4 * S5_STATE), lambda p: (layer, p, 0, 0)),
            pl.BlockSpec((None, 1, 4 * S5_STATE, 2 * S5_K), lambda p: (layer, p, 0, 0)),
            pl.BlockSpec((None, 1, 2, 2 * S5_STATE), lambda p: (layer, p, 0, 0)),
        ],
        out_specs=pl.BlockSpec((rows, 2 * S5_K), lambda p: (0, p)),
        out_shape=jax.ShapeDtypeStruct((rows, S5_GROUPS * S5_K), F32),
        scratch_shapes=[
            pltpu.VMEM((rows, 4 * S5_STATE), F32),
            pltpu.VMEM((rows, 4 * S5_STATE), BF16),
        ],
        compiler_params=pltpu.CompilerParams(
            dimension_semantics=("arbitrary",), vmem_limit_bytes=VMEM_LIMIT),
        name="s5core",
    )(uc, tmat, hmat, gmat, a2)


def _atom_transpose(groups):
    lane = lax.broadcasted_iota(jnp.int32, groups[0][0].shape, 1)
    groups = [list(xs) for xs in groups]
    for s in range(3):
        d = 1 << s
        keep = ((lane >> (4 + s)) & 1) == 0
        for xs in groups:
            for i in range(8):
                if i & d:
                    continue
                lo, hi = xs[i], xs[i + d]
                xs[i] = jnp.where(keep, lo, pltpu.roll(hi, S5_GROUP * d, 1))
                xs[i + d] = jnp.where(keep, pltpu.roll(lo, LANES - S5_GROUP * d, 1), hi)
    return groups


def _to_chunks_kernel(u_ref, o_ref, rows_ref):
    bsz, tt, width = u_ref.shape
    pitch = rows_ref.shape[1] // bsz
    for v in range(width // LANES):
        for b in range(bsz):
            rows_ref[v, b * pitch:b * pitch + tt, :] = u_ref[b, :, v * LANES:(v + 1) * LANES]
    halves = [(v, n, k) for v in range(width // LANES) for n in range(tt // S5_R) for k in range(2)]
    groups = _atom_transpose(
        [[rows_ref[v, pl.ds(n * S5_R + 8 * k + j, bsz, stride=pitch), :] for j in range(8)]
         for v, n, k in halves])
    for (v, n, k), ys in zip(halves, groups):
        for g, y in enumerate(ys):
            lane0 = (8 * v + g) * S5_K + k * LANES
            o_ref[n * bsz:(n + 1) * bsz, lane0:lane0 + LANES] = y.astype(BF16)


def _to_chunks(u3, tt=176, width=512):
    bsz, lp, _ = u3.shape
    return pl.pallas_call(
        _to_chunks_kernel,
        grid=(lp // tt, D_S5 // width),
        in_specs=[pl.BlockSpec((bsz, tt, width), lambda i, v: (0, i, v))],
        out_specs=pl.BlockSpec((tt, width // S5_GROUP * S5_K), lambda i, v: (i, v)),
        out_shape=jax.ShapeDtypeStruct((lp // S5_R * bsz, S5_GROUPS * S5_K), BF16),
        scratch_shapes=[pltpu.VMEM((width // LANES, bsz * (tt + 8), LANES), F32)],
        compiler_params=pltpu.CompilerParams(
            dimension_semantics=("arbitrary", "arbitrary"), vmem_limit_bytes=VMEM_LIMIT),
        name="s5_to_chunks",
    )(u3)


def _s5glu_kernel(yc_ref, u_ref, d_ref, w_ref, b_ref, gain_ref, o_ref, yn_ref):
    bsz, tt, _ = u_ref.shape
    pitch = yn_ref.shape[1] // bsz
    for n in range(tt // S5_R):
        halves = [(v, k) for v in range(D_S5 // LANES) for k in range(2)]
        groups = _atom_transpose(
            [[yc_ref[n * bsz:(n + 1) * bsz,
                     (8 * v + g) * S5_K + k * LANES:(8 * v + g) * S5_K + (k + 1) * LANES]
              for g in range(8)] for v, k in halves])
        for (v, k), xs in zip(halves, groups):
            for j, x in enumerate(xs):
                yn_ref[v, pl.ds(n * S5_R + 8 * k + j, bsz, stride=pitch), :] = x
    y = jnp.concatenate(
        [jnp.concatenate([yn_ref[v, b * pitch:b * pitch + tt, :] for v in range(D_S5 // LANES)],
                         axis=1) for b in range(bsz)], axis=0)
    y = _gelu_tanh(y + d_ref[...] * u_ref[...].reshape(bsz * tt, D_S5))
    g = jnp.dot(y.astype(BF16), w_ref[...], preferred_element_type=F32) + b_ref[...]
    o_ref[...] = _rms(y * _sigmoid(g), gain_ref[...]).reshape(bsz, tt, D_S5).astype(BF16)


def _s5glu(yc, u3, d_skip, wglu_all, layer, bglu, gain, tt=48):
    bsz, lp, _ = u3.shape
    return pl.pallas_call(
        _s5glu_kernel,
        grid=(lp // tt,),
        in_specs=[
            pl.BlockSpec((tt // S5_R * bsz, S5_GROUPS * S5_K), lambda i: (i, 0)),
            pl.BlockSpec((bsz, tt, D_S5), lambda i: (0, i, 0)),
            pl.BlockSpec((1, D_S5), lambda i: (0, 0)),
            pl.BlockSpec((None, D_S5, D_S5), lambda i: (layer, 0, 0)),
            pl.BlockSpec((1, D_S5), lambda i: (0, 0)),
            pl.BlockSpec((1, D_S5), lambda i: (0, 0)),
        ],
        out_specs=pl.BlockSpec((bsz, tt, D_S5), lambda i: (0, i, 0)),
        out_shape=jax.ShapeDtypeStruct((bsz, lp, D_S5), BF16),
        scratch_shapes=[pltpu.VMEM((D_S5 // LANES, bsz * (tt + 8), LANES), F32)],
        compiler_params=pltpu.CompilerParams(
            dimension_semantics=("arbitrary",), vmem_limit_bytes=VMEM_LIMIT),
        name="s5glu",
    )(yc, u3, d_skip.reshape(1, D_S5), wglu_all, bglu.reshape(1, D_S5), gain.reshape(1, D_S5))


def _s5_params(lam_re, lam_im, log_step, b_re, b_im, c_re, c_im):
    hp = lax.Precision.HIGHEST
    a_re = jnp.minimum(lam_re.astype(F32), -1e-4)
    a_im = lam_im.astype(F32)
    dt = jnp.exp(log_step.astype(F32))[:, None]
    mag = jnp.exp(a_re * dt)
    ab_re = mag * jnp.cos(a_im * dt)
    ab_im = mag * jnp.sin(a_im * dt)
    den = a_re * a_re + a_im * a_im
    x_re, x_im = ab_re - 1.0, ab_im
    z_re = (x_re * a_re + x_im * a_im) / den
    z_im = (x_im * a_re - x_re * a_im) / den
    br, bi = b_re.astype(F32), b_im.astype(F32)
    bb_re = z_re[..., None] * br - z_im[..., None] * bi
    bb_im = z_re[..., None] * bi + z_im[..., None] * br
    cr, ci = c_re.astype(F32), c_im.astype(F32)

    pw_re, pw_im = [jnp.ones_like(ab_re)], [jnp.zeros_like(ab_re)]
    for _ in range(S5_R):
        r, i = pw_re[-1], pw_im[-1]
        pw_re.append(r * ab_re - i * ab_im)
        pw_im.append(r * ab_im + i * ab_re)
    p_re, p_im = jnp.stack(pw_re), jnp.stack(pw_im)

    q_re, q_im = p_re[:S5_R, :, :, None], p_im[:S5_R, :, :, None]
    m_re = q_re * bb_re - q_im * bb_im
    m_im = q_re * bb_im + q_im * bb_re
    kern = (jnp.einsum('gop,lgph->ghlo', cr, m_re, precision=hp)
            - jnp.einsum('gop,lgph->ghlo', ci, m_im, precision=hp))
    kz = jnp.concatenate([jnp.zeros((S5_GROUPS, S5_GROUP, S5_K), F32),
                          kern.reshape(S5_GROUPS, S5_GROUP, S5_K)], axis=-1)
    tmat = jnp.stack([kz[:, :, (S5_R - s) * S5_GROUP:(S5_R - s) * S5_GROUP + S5_K]
                      for s in range(S5_R)], axis=1).reshape(S5_PAIRS, 2, S5_K, S5_K)

    bt_re, bt_im = bb_re.transpose(0, 2, 1)[:, None], bb_im.transpose(0, 2, 1)[:, None]
    rp_re = jnp.moveaxis(p_re[S5_R - 1::-1], 0, 1)[:, :, None, :]
    rp_im = jnp.moveaxis(p_im[S5_R - 1::-1], 0, 1)[:, :, None, :]
    h_re = (rp_re * bt_re - rp_im * bt_im).reshape(S5_PAIRS, 2, S5_K, S5_STATE)
    h_im = (rp_re * bt_im + rp_im * bt_re).reshape(S5_PAIRS, 2, S5_K, S5_STATE)
    zh = jnp.zeros_like(h_re[:, 0])
    hmat = jnp.concatenate([
        jnp.concatenate([h_re[:, 0], zh, h_im[:, 0], zh], axis=-1),
        jnp.concatenate([zh, h_re[:, 1], zh, h_im[:, 1]], axis=-1)], axis=1)

    ct_re, ct_im = cr.transpose(0, 2, 1)[:, :, None, :], ci.transpose(0, 2, 1)[:, :, None, :]
    e_re = jnp.moveaxis(p_re[1:], 0, 2)[..., None]
    e_im = jnp.moveaxis(p_im[1:], 0, 2)[..., None]
    g_re = (ct_re * e_re - ct_im * e_im).reshape(S5_PAIRS, 2, S5_STATE, S5_K)
    g_im = -(ct_re * e_im + ct_im * e_re).reshape(S5_PAIRS, 2, S5_STATE, S5_K)
    zg = jnp.zeros_like(g_re[:, 0])
    gmat = jnp.concatenate([
        jnp.concatenate([g_re[:, 0], zg], axis=-1), jnp.concatenate([zg, g_re[:, 1]], axis=-1),
        jnp.concatenate([g_im[:, 0], zg], axis=-1), jnp.concatenate([zg, g_im[:, 1]], axis=-1)],
        axis=1)

    a2 = jnp.stack([p_re[S5_R].reshape(S5_PAIRS, 2 * S5_STATE),
                    p_im[S5_R].reshape(S5_PAIRS, 2 * S5_STATE)], axis=1)
    return tmat.astype(BF16), hmat.astype(BF16), gmat.astype(BF16), a2


def _s5(u3, params, layer, d_skip, wglu_all, bglu, gain):
    yc = _s5core(_to_chunks(u3), params, layer, u3.shape[0])
    return _s5glu(yc, u3, d_skip, wglu_all, layer, bglu, gain)


def _outproj_kernel(h_ref, pre_ref, a_ref, b_ref, w_ref, o_ref, *, first):
    y = (jnp.dot(a_ref[...], w_ref[0], preferred_element_type=F32)
         + jnp.dot(b_ref[...], w_ref[1], preferred_element_type=F32))
    if first:
        npre = pre_ref.shape[0]

        @pl.when(pl.program_id(1) == 0)
        def _():
            o_ref[0:npre, :] = y[0:npre, :] + pre_ref[...]
            o_ref[npre:, :] = y[npre:, :] + h_ref[0:h_ref.shape[0] - npre, :]

        @pl.when(pl.program_id(1) != 0)
        def _():
            o_ref[...] = y + h_ref[...]
    else:
        o_ref[...] = y + h_ref[...]


def _outproj(h, prefix, a, b, w_all, layer, first, bsz, tm=704):
    k = a.shape[1]
    d = h.shape[1]
    lp = a.shape[0] // bsz
    tiles = lp // tm
    flat = lambda b, j: (b * tiles + j, 0)
    return pl.pallas_call(
        functools.partial(_outproj_kernel, first=first),
        grid=(bsz, tiles),
        in_specs=[
            _row_tile_specs(first, bsz, lp, tm, d),
            pl.BlockSpec((CHUNK, d), lambda b, j: (0, 0)),
            pl.BlockSpec((tm, k), flat),
            pl.BlockSpec((tm, k), flat),
            pl.BlockSpec((None, 2, k, d), lambda b, j: (layer, 0, 0, 0)),
        ],
        out_specs=pl.BlockSpec((tm, d), flat),
        out_shape=jax.ShapeDtypeStruct((bsz * lp, d), F32),
        compiler_params=pltpu.CompilerParams(
            dimension_semantics=("arbitrary", "arbitrary"), vmem_limit_bytes=VMEM_LIMIT),
        name="outproj",
    )(h, prefix, a, b, w_all)


FFN_HALO = 16


def _ffn_kernel(h_ref, halo_ref, g_ref, wg_ref, wu_ref, cw_ref, cb_ref, wd_ref, fg_ref, o_ref,
                hn_ref, *, final, sub):
    f_axis = 2 if final else 1
    f = pl.program_id(f_axis)
    tm = o_ref.shape[0]

    def block(r):
        hn = hn_ref[r:r + sub + FFN_HALO, :]
        a = jnp.dot(hn, wg_ref[...], preferred_element_type=F32)
        up = jnp.dot(hn[FFN_HALO:, :], wu_ref[...], preferred_element_type=F32)
        cw = cw_ref[...]
        conv = (cb_ref[...]
                + cw[0:1, :] * pltpu.roll(a, 2, 0)[FFN_HALO:, :]
                + cw[1:2, :] * pltpu.roll(a, 1, 0)[FFN_HALO:, :]
                + cw[2:3, :] * a[FFN_HALO:, :])
        hid = (_silu(conv) * up).astype(BF16)
        return jnp.dot(hid, wd_ref[...], preferred_element_type=F32)

    @pl.when(f == 0)
    def _():
        hn_ref[0:FFN_HALO, :] = _rms(halo_ref[...], g_ref[...]).astype(BF16)
        for r in range(0, tm, sub):
            x = h_ref[r:r + sub, :]
            hn_ref[FFN_HALO + r:FFN_HALO + r + sub, :] = _rms(x, g_ref[...]).astype(BF16)
        for r in range(0, tm, sub):
            o_ref[r:r + sub, :] = h_ref[r:r + sub, :] + block(r)

    @pl.when(f > 0)
    def _():
        for r in range(0, tm, sub):
            o_ref[r:r + sub, :] += block(r)

    if final:
        @pl.when(f == pl.num_programs(f_axis) - 1)
        def _():
            o_ref[...] = _rms(o_ref[...], fg_ref[...])


def _ffn(h, g, layer, wg_all, wu_all, cw_all, cb_all, wd_all, fg, final, bsz, tm=512, tf=512,
         sub=None):
    t, d = h.shape
    ff = wg_all.shape[2]
    hb = tm // FFN_HALO
    lp = t // bsz
    if final:
        grid = (bsz, (lp - CHUNK) // tm, ff // tf)
        row0 = lambda b, j, back: pl.multiple_of(b * lp + CHUNK + j * tm - back, FFN_HALO)
        row_specs = [
            pl.BlockSpec((pl.Element(tm), pl.Element(d)), lambda b, j, f: (row0(b, j, 0), 0)),
            pl.BlockSpec((pl.Element(FFN_HALO), pl.Element(d)),
                         lambda b, j, f: (row0(b, j, FFN_HALO), 0)),
        ]
        out_spec = pl.BlockSpec((tm, d), lambda b, j, f: (b * ((lp - CHUNK) // tm) + j, 0))
        out_rows = bsz * (lp - CHUNK)
        wmap = lambda fn: (lambda b, j, f: fn(f))
    else:
        grid = (t // tm, ff // tf)
        row_specs = [
            pl.BlockSpec((tm, d), lambda i, f: (i, 0)),
            pl.BlockSpec((FFN_HALO, d), lambda i, f: (jnp.maximum(i * hb - 1, 0), 0)),
        ]
        out_spec = pl.BlockSpec((tm, d), lambda i, f: (i, 0))
        out_rows = t
        wmap = lambda fn: (lambda i, f: fn(f))
    return pl.pallas_call(
        functools.partial(_ffn_kernel, final=final, sub=sub or tm),
        grid=grid,
        in_specs=row_specs + [
            pl.BlockSpec((1, d), wmap(lambda f: (0, 0))),
            pl.BlockSpec((None, d, tf), wmap(lambda f: (layer, 0, f))),
            pl.BlockSpec((None, d, tf), wmap(lambda f: (layer, 0, f))),
            pl.BlockSpec((None, 3, tf), wmap(lambda f: (layer, 0, f))),
            pl.BlockSpec((None, 1, tf), wmap(lambda f: (layer, 0, f))),
            pl.BlockSpec((None, tf, d), wmap(lambda f: (layer, f, 0))),
            pl.BlockSpec((1, d), wmap(lambda f: (0, 0))),
        ],
        out_specs=out_spec,
        out_shape=jax.ShapeDtypeStruct((out_rows, d), F32),
        scratch_shapes=[pltpu.VMEM((tm + FFN_HALO, d), BF16)],
        compiler_params=pltpu.CompilerParams(
            dimension_semantics=("arbitrary",) * len(grid), vmem_limit_bytes=VMEM_LIMIT),
        name="ffn_final" if final else "ffn",
    )(h, h, g.reshape(1, d), wg_all, wu_all, cw_all, cb_all.reshape(cb_all.shape[0], 1, ff),
      wd_all, fg.reshape(1, d))


def kernel(x, meta_tokens, lb_logits, norm_mix, w_in, hg_norm, s5_lambda_re, s5_lambda_im,
           s5_log_step, s5_b_re, s5_b_im, s5_c_re, s5_c_im, s5_d, w_glu, b_glu, s5_norm, w_out,
           norm_ffn, w_ffn_gate, w_ffn_up, ffn_conv_w, ffn_conv_b, w_ffn_down, final_norm):
    bsz, seq, d = x.shape
    lp = seq + CHUNK
    prefix = jnp.concatenate([jnp.zeros((SEQ_PAD, d), x.dtype), meta_tokens.astype(x.dtype)], axis=0)
    h = x.reshape(bsz * seq, d)

    sm = jax.nn.softmax(lb_logits.astype(F32), axis=0)
    lb_all = jnp.cumsum(sm, axis=0) - sm[0:1]

    w_in_b = w_in.astype(BF16)
    w_out_b = w_out.astype(BF16).reshape(DEPTH, 2, D_HG, d)
    w_glu_b = w_glu.astype(BF16)
    wg_b, wu_b, wd_b = (w.astype(BF16) for w in (w_ffn_gate, w_ffn_up, w_ffn_down))
    s5_params = jax.vmap(_s5_params)(s5_lambda_re, s5_lambda_im, s5_log_step,
                                     s5_b_re, s5_b_im, s5_c_re, s5_c_im)

    for l in range(DEPTH):
        o_hg, u3 = _proj_hgrn2(h, prefix, norm_mix[l], w_in_b, l, lb_all[l], hg_norm[l],
                               first=(l == 0), bsz=bsz)
        o_s5 = _s5(u3, s5_params, l, s5_d[l], w_glu_b, b_glu[l], s5_norm[l])
        h = _outproj(h, prefix, o_hg.reshape(bsz * lp, D_HG), o_s5.reshape(bsz * lp, D_S5),
                     w_out_b, l, first=(l == 0), bsz=bsz)
        h = _ffn(h, norm_ffn[l], l, wg_b, wu_b, ffn_conv_w, ffn_conv_b, wd_b, final_norm,
                 final=(l == DEPTH - 1), bsz=bsz, **(dict(tm=1024, sub=512) if l == DEPTH - 1 else dict(tm=1056, sub=528)))
    return h.reshape(bsz, seq, d)
```

```python
import functools

import jax
import jax.numpy as jnp
from jax import lax
from jax.experimental import pallas as pl
from jax.experimental.pallas import tpu as pltpu

F32 = jnp.float32
BF16 = jnp.bfloat16
LANES = 128

DEPTH = 2
CHUNK = 64
N_META = 16
D_HG = 1024
HG_HEADS = 8
HG_DK = 128
D_S5 = 1024
S5_GROUP = 16
S5_GROUPS = 64
S5_STATE = 64
D_IN = 4 * D_HG + D_S5
EPS = 1e-6
F_FLOOR = 1e-6

SEQ_PAD = CHUNK - N_META
VMEM_LIMIT = 56 * 1024 * 1024
VMEM_LIMIT_BIG = 62 * 1024 * 1024

S5_R = 16
S5_K = S5_R * S5_GROUP
S5_PAIRS = S5_GROUPS // 2


def _rms(x, g):
    ms = jnp.mean(x * x, axis=-1, keepdims=True)
    return x * lax.rsqrt(ms + EPS) * g


def _sigmoid(x):
    return 0.5 + 0.5 * jnp.tanh(0.5 * x)


def _silu(x):
    half = 0.5 * x
    return half + half * jnp.tanh(half)


def _normed_tile(h_ref, pre_ref, g_ref, xn_ref, first):
    if first:
        npre = pre_ref.shape[0]

        @pl.when(pl.program_id(1) == 0)
        def _():
            xn_ref[0:npre, :] = _rms(pre_ref[...], g_ref[...]).astype(BF16)
            xn_ref[npre:, :] = _rms(h_ref[0:h_ref.shape[0] - npre, :], g_ref[...]).astype(BF16)

        @pl.when(pl.program_id(1) != 0)
        def _():
            xn_ref[...] = _rms(h_ref[...], g_ref[...]).astype(BF16)
    else:
        xn_ref[...] = _rms(h_ref[...], g_ref[...]).astype(BF16)


def _row_tile_specs(first, bsz, lp, tm, d):
    tiles = lp // tm
    if first:
        def rows(b, j, *_):
            return (pl.multiple_of(b * (lp - CHUNK) + jnp.maximum(j * tm - CHUNK, 0), 8), 0)
        return pl.BlockSpec((pl.Element(tm), pl.Element(d)), rows)
    return pl.BlockSpec((tm, d), lambda b, j, *_: (b * tiles + j, 0))


def _pick_rows(x, m, first):
    j = lax.broadcasted_iota(jnp.int32, x.shape, 0)
    out = None
    for blk in range(8 // m - 1, -1, -1):
        r = blk * m + (0 if first else m - 1)
        row = jnp.broadcast_to(x[r:r + 1, :], x.shape)
        out = row if out is None else jnp.where(j < (blk + 1) * m, row, out)
    return out


def _expand_rows(ref, i, x):
    nv = x.shape[1] // LANES
    for v in range(nv):
        ref[i, v] = x[:, v * LANES:(v + 1) * LANES]
    return jnp.concatenate(
        [jnp.concatenate([ref[i, v, pl.ds(j, 8, stride=0), :] for v in range(nv)], axis=1)
         for j in range(8)], axis=0)


def _dot_nt(a, b):
    return lax.dot_general(a, b, (((1,), (1,)), ((), ())), preferred_element_type=F32)


def _dot_tn(a, b):
    return lax.dot_general(a, b, (((0,), (0,)), ((), ())), preferred_element_type=F32)


def _proj_hgrn2_kernel(h_ref, pre_ref, g_ref, w_ref, lb_ref, gain_ref, o_ref, u_ref,
                       xn_ref, p_ref, st_ref, bc_ref, *, first):
    @pl.when(pl.program_id(1) == 0)
    def _():
        st_ref[...] = jnp.zeros_like(st_ref)

    _normed_tile(h_ref, pre_ref, g_ref, xn_ref, first)
    xn = xn_ref[...]
    for k in range(4 * D_HG // D_S5):
        cols = slice(k * D_S5, (k + 1) * D_S5)
        p_ref[:, cols] = jnp.dot(xn, w_ref[:, cols], preferred_element_type=F32)

    nsel = CHUNK + 16
    row = lax.broadcasted_iota(jnp.int32, (nsel, 3 * CHUNK), 0)
    col = lax.broadcasted_iota(jnp.int32, (nsel, 3 * CHUNK), 1) & (CHUNK - 1)
    tgt = jnp.where(row < CHUNK, row,
                    jnp.where(row < CHUNK + 8, (row - CHUNK) * 8 + 7, (row - CHUNK - 8) * 8 + 3))
    tri3 = (tgt >= col).astype(BF16)
    ti = lax.broadcasted_iota(jnp.int32, (CHUNK, CHUNK), 0)
    si = lax.broadcasted_iota(jnp.int32, (CHUNK, CHUNK), 1)
    masks = {}
    for w, sh in ((32, 5), (16, 4), (8, 3)):
        masks[w] = ((ti >> sh) == (si >> sh) + 1) & ((ti >> (sh + 1)) == (si >> (sh + 1)))
    masks[0] = ((ti >> 3) == (si >> 3)) & (si <= ti)

    u_blocks = 4
    u_cols = D_S5 // u_blocks
    for c in range(p_ref.shape[0] // CHUNK):
        if c % 2 == 0 and c // 2 < u_blocks:
            j = c // 2
            cols = slice(j * u_cols, (j + 1) * u_cols)
            u_ref[:, cols] = jnp.dot(xn, w_ref[:, 4 * D_HG + j * u_cols:4 * D_HG + (j + 1) * u_cols],
                                     preferred_element_type=F32)
        _hgrn2_chunk(p_ref, lb_ref, gain_ref, o_ref, st_ref, bc_ref, tri3, masks,
                     pl.ds(c * CHUNK, CHUNK))


def _hgrn2_chunk(p_ref, lb_ref, gain_ref, o_ref, st_ref, bc_ref, tri3, masks, rows):
    q = p_ref[rows, 0:D_HG]
    z = p_ref[rows, D_HG:2 * D_HG]
    v = p_ref[rows, 2 * D_HG:3 * D_HG].astype(BF16)
    lb = lb_ref[...]

    th = 0.5 * jnp.tanh(0.5 * z)
    gl = jnp.log2(jnp.maximum(lb + (1.0 - lb) * (0.5 + th), F_FLOOR))
    kk = (1.0 - lb) * (0.5 - th)
    qf = _silu(q)

    hi = gl.astype(BF16)
    r1 = gl - hi.astype(F32)
    mid = r1.astype(BF16)
    lo = (r1 - mid.astype(F32)).astype(BF16)
    cum = jnp.dot(tri3, jnp.concatenate([hi, mid, lo], axis=0), preferred_element_type=F32)
    b = cum[0:CHUNK]
    l8 = cum[CHUNK:CHUNK + 8]
    m8 = cum[CHUNK + 8:CHUNK + 16]
    j8 = lax.broadcasted_iota(jnp.int32, l8.shape, 0)
    s8 = jnp.where(j8 == 0, 0.0, pltpu.roll(l8, 1, 0))

    qe = qf * jnp.exp2(b - _expand_rows(bc_ref, 0, s8))
    kf = kk * jnp.exp2(_expand_rows(bc_ref, 1, l8) - b)
    qh = {8: qe.astype(BF16)}
    kh = {8: kf.astype(BF16)}
    for lvl, (w, m) in enumerate(((16, 2), (32, 4), (64, 8))):
        cq = jnp.exp2(s8 - _pick_rows(s8, m, True))
        ck = jnp.exp2(_pick_rows(l8, m, False) - l8)
        qh[w] = (qe * _expand_rows(bc_ref, 3 + 2 * lvl, cq)).astype(BF16)
        kh[w] = (kf * _expand_rows(bc_ref, 4 + 2 * lvl, ck)).astype(BF16)
    b_mid = _expand_rows(bc_ref, 2, m8)
    qh[0] = (qf * jnp.exp2(b - b_mid)).astype(BF16)
    kh[0] = (kk * jnp.exp2(b_mid - b)).astype(BF16)
    dec_all = jnp.exp2(l8[7:8, :])

    gate = p_ref[rows, 3 * D_HG:4 * D_HG]
    gate = _silu(gate)
    gain = gain_ref[...]

    heads = [slice(h * HG_DK, (h + 1) * HG_DK) for h in range(HG_HEADS)]
    sts = [st_ref[h] for h in range(HG_HEADS)]
    o_inter = [_dot_nt(qh[64][:, sl], st.astype(BF16)) for sl, st in zip(heads, sts)]
    scores = []
    for sl in heads:
        sc = jnp.zeros((CHUNK, CHUNK), F32)
        for w in (32, 16, 8, 0):
            sc = jnp.where(masks[w], _dot_nt(qh[w][:, sl], kh[w][:, sl]), sc)
        scores.append(sc.astype(BF16))
    outs = [oi + jnp.dot(sc, v[:, sl], preferred_element_type=F32)
            for oi, sc, sl in zip(o_inter, scores, heads)]
    for h, (sl, st) in enumerate(zip(heads, sts)):
        st_ref[h] = dec_all[:, sl] * st + _dot_tn(v[:, sl], kh[64][:, sl])
    for o, sl in zip(outs, heads):
        ms = jnp.mean(o * o, axis=-1, keepdims=True)
        o = o * lax.rsqrt(ms + EPS) * gain[:, sl]
        o_ref[rows, sl] = (o * gate[:, sl]).astype(BF16)


def _proj_hgrn2(h, prefix, g, w_all, layer, lb, gain, first, bsz, chunks_per_step=11):
    d = h.shape[1]
    lp = h.shape[0] // bsz + (CHUNK if first else 0)
    rows = chunks_per_step * CHUNK
    const = lambda b, c: (0, 0)
    return pl.pallas_call(
        functools.partial(_proj_hgrn2_kernel, first=first),
        grid=(bsz, lp // rows),
        in_specs=[
            _row_tile_specs(first, bsz, lp, rows, d),
            pl.BlockSpec((CHUNK, d), const),
            pl.BlockSpec((1, d), const),
            pl.BlockSpec((None, d, D_IN), lambda b, c: (layer, 0, 0),
                         pipeline_mode=pl.Buffered(1)),
            pl.BlockSpec((1, D_HG), const),
            pl.BlockSpec((1, D_HG), const),
        ],
        out_specs=[pl.BlockSpec((None, rows, D_HG), lambda b, c: (b, c, 0)),
                   pl.BlockSpec((None, rows, D_S5), lambda b, c: (b, c, 0))],
        out_shape=[jax.ShapeDtypeStruct((bsz, lp, D_HG), BF16),
                   jax.ShapeDtypeStruct((bsz, lp, D_S5), F32)],
        scratch_shapes=[
            pltpu.VMEM((rows, d), BF16),
            pltpu.VMEM((rows, 4 * D_HG), F32),
            pltpu.VMEM((HG_HEADS, HG_DK, HG_DK), F32),
            pltpu.VMEM((9, D_HG // LANES, 8, LANES), F32),
        ],
        compiler_params=pltpu.CompilerParams(
            dimension_semantics=("arbitrary", "arbitrary"), vmem_limit_bytes=VMEM_LIMIT_BIG),
        name="proj_hgrn2",
    )(h, prefix, g.reshape(1, d), w_all, lb.reshape(1, D_HG), gain.reshape(1, D_HG))


def _gelu_tanh(x):
    c = 0.7978845608028654
    half = 0.5 * x
    return half + half * jnp.tanh(x * (c + (c * 0.044715) * (x * x)))


def _s5core_kernel(u_ref, t_ref, h_ref, g_ref, a_ref, y_ref, v_ref, xp_ref, *, bsz):
    nchunks = u_ref.shape[0] // bsz
    npair = t_ref.shape[0]
    width = 2 * S5_K
    us = [u_ref[:, q * width:(q + 1) * width] for q in range(npair)]
    for q in range(npair):
        v_ref[q] = jnp.dot(us[q], h_ref[q], preferred_element_type=F32)
    a_re = [jnp.broadcast_to(a_ref[q, 0:1, :], (bsz, LANES)) for q in range(npair)]
    a_im = [jnp.broadcast_to(a_ref[q, 1:2, :], (bsz, LANES)) for q in range(npair)]

    y_in = [[jnp.dot(us[q][:, k * S5_K:(k + 1) * S5_K], t_ref[q, k], preferred_element_type=F32)
             for k in range(2)] for q in range(npair)]

    x_re = [jnp.zeros((bsz, LANES), F32)] * npair
    x_im = [jnp.zeros((bsz, LANES), F32)] * npair
    for n in range(nchunks):
        rows = slice(n * bsz, (n + 1) * bsz)
        for q in range(npair):
            xp_ref[q, rows, 0:LANES] = x_re[q].astype(BF16)
            xp_ref[q, rows, LANES:2 * LANES] = x_im[q].astype(BF16)
            x_re[q], x_im[q] = (
                a_re[q] * x_re[q] - a_im[q] * x_im[q] + v_ref[q, rows, 0:LANES],
                a_re[q] * x_im[q] + a_im[q] * x_re[q] + v_ref[q, rows, LANES:2 * LANES])

    for q in range(npair):
        ys = jnp.dot(xp_ref[q], g_ref[q], preferred_element_type=F32)
        for k in range(2):
            cols = slice(k * S5_K, (k + 1) * S5_K)
            y_ref[:, q * width + k * S5_K:q * width + (k + 1) * S5_K] = ys[:, cols] + y_in[q][k]


def _s5core(uc, params, layer, bsz, npair=2):
    tmat, hmat, gmat, a2 = params
    rows = uc.shape[0]
    return pl.pallas_call(
        functools.partial(_s5core_kernel, bsz=bsz),
        grid=(S5_PAIRS // npair,),
        in_specs=[
            pl.BlockSpec((rows, npair * 2 * S5_K), lambda p: (0, p)),
            pl.BlockSpec((None, npair, 2, S5_K, S5_K), lambda p: (layer, p, 0, 0, 0)),
            pl.BlockSpec((None, npair, 2 * S5_K, 4 * S5_STATE), lambda p: (layer, p, 0, 0)),
            pl.BlockSpec((None, npair, 4 * S5_STATE, 2 * S5_K), lambda p: (layer, p, 0, 0)),
            pl.BlockSpec((None, npair, 2, 2 * S5_STATE), lambda p: (layer, p, 0, 0)),
        ],
        out_specs=pl.BlockSpec((rows, npair * 2 * S5_K), lambda p: (0, p)),
        out_shape=jax.ShapeDtypeStruct((rows, S5_GROUPS * S5_K), F32),
        scratch_shapes=[
            pltpu.VMEM((npair, rows, 4 * S5_STATE), F32),
            pltpu.VMEM((npair, rows, 4 * S5_STATE), BF16),
        ],
        compiler_params=pltpu.CompilerParams(
            dimension_semantics=("arbitrary",), vmem_limit_bytes=VMEM_LIMIT),
        name="s5core",
    )(uc, tmat, hmat, gmat, a2)


def _atom_transpose(groups):
    lane = lax.broadcasted_iota(jnp.int32, groups[0][0].shape, 1)
    groups = [list(xs) for xs in groups]
    for s in range(3):
        d = 1 << s
        keep = ((lane >> (4 + s)) & 1) == 0
        for xs in groups:
            for i in range(8):
                if i & d:
                    continue
                lo, hi = xs[i], xs[i + d]
                xs[i] = jnp.where(keep, lo, pltpu.roll(hi, S5_GROUP * d, 1))
                xs[i + d] = jnp.where(keep, pltpu.roll(lo, LANES - S5_GROUP * d, 1), hi)
    return groups


def _to_chunks_kernel(u_ref, o_ref, rows_ref):
    bsz, tt, width = u_ref.shape
    pitch = rows_ref.shape[1] // bsz
    for v in range(width // LANES):
        for b in range(bsz):
            rows_ref[v, b * pitch:b * pitch + tt, :] = u_ref[b, :, v * LANES:(v + 1) * LANES]
    halves = [(v, n, k) for v in range(width // LANES) for n in range(tt // S5_R) for k in range(2)]
    groups = _atom_transpose(
        [[rows_ref[v, pl.ds(n * S5_R + 8 * k + j, bsz, stride=pitch), :] for j in range(8)]
         for v, n, k in halves])
    for (v, n, k), ys in zip(halves, groups):
        for g, y in enumerate(ys):
            lane0 = (8 * v + g) * S5_K + k * LANES
            o_ref[n * bsz:(n + 1) * bsz, lane0:lane0 + LANES] = y.astype(BF16)


def _to_chunks(u3, tt=176, width=512):
    bsz, lp, _ = u3.shape
    return pl.pallas_call(
        _to_chunks_kernel,
        grid=(lp // tt, D_S5 // width),
        in_specs=[pl.BlockSpec((bsz, tt, width), lambda i, v: (0, i, v))],
        out_specs=pl.BlockSpec((tt, width // S5_GROUP * S5_K), lambda i, v: (i, v)),
        out_shape=jax.ShapeDtypeStruct((lp // S5_R * bsz, S5_GROUPS * S5_K), BF16),
        scratch_shapes=[pltpu.VMEM((width // LANES, bsz * (tt + 8), LANES), F32)],
        compiler_params=pltpu.CompilerParams(
            dimension_semantics=("arbitrary", "arbitrary"), vmem_limit_bytes=VMEM_LIMIT),
        name="s5_to_chunks",
    )(u3)


def _s5glu_kernel(yc_ref, u_ref, d_ref, w_ref, b_ref, gain_ref, o_ref, yn_ref):
    bsz, tt, _ = u_ref.shape
    pitch = yn_ref.shape[1] // bsz
    for n in range(tt // S5_R):
        halves = [(v, k) for v in range(D_S5 // LANES) for k in range(2)]
        groups = _atom_transpose(
            [[yc_ref[n * bsz:(n + 1) * bsz,
                     (8 * v + g) * S5_K + k * LANES:(8 * v + g) * S5_K + (k + 1) * LANES]
              for g in range(8)] for v, k in halves])
        for (v, k), xs in zip(halves, groups):
            for j, x in enumerate(xs):
                yn_ref[v, pl.ds(n * S5_R + 8 * k + j, bsz, stride=pitch), :] = x
    y = jnp.concatenate(
        [jnp.concatenate([yn_ref[v, b * pitch:b * pitch + tt, :] for v in range(D_S5 // LANES)],
                         axis=1) for b in range(bsz)], axis=0)
    y = _gelu_tanh(y + d_ref[...] * u_ref[...].reshape(bsz * tt, D_S5))
    g = jnp.dot(y.astype(BF16), w_ref[...], preferred_element_type=F32) + b_ref[...]
    o_ref[...] = _rms(y * _sigmoid(g), gain_ref[...]).reshape(bsz, tt, D_S5).astype(BF16)


def _s5glu(yc, u3, d_skip, wglu_all, layer, bglu, gain, tt=48):
    bsz, lp, _ = u3.shape
    return pl.pallas_call(
        _s5glu_kernel,
        grid=(lp // tt,),
        in_specs=[
            pl.BlockSpec((tt // S5_R * bsz, S5_GROUPS * S5_K), lambda i: (i, 0)),
            pl.BlockSpec((bsz, tt, D_S5), lambda i: (0, i, 0)),
            pl.BlockSpec((1, D_S5), lambda i: (0, 0)),
            pl.BlockSpec((None, D_S5, D_S5), lambda i: (layer, 0, 0)),
            pl.BlockSpec((1, D_S5), lambda i: (0, 0)),
            pl.BlockSpec((1, D_S5), lambda i: (0, 0)),
        ],
        out_specs=pl.BlockSpec((bsz, tt, D_S5), lambda i: (0, i, 0)),
        out_shape=jax.ShapeDtypeStruct((bsz, lp, D_S5), BF16),
        scratch_shapes=[pltpu.VMEM((D_S5 // LANES, bsz * (tt + 8), LANES), F32)],
        compiler_params=pltpu.CompilerParams(
            dimension_semantics=("arbitrary",), vmem_limit_bytes=VMEM_LIMIT),
        name="s5glu",
    )(yc, u3, d_skip.reshape(1, D_S5), wglu_all, bglu.reshape(1, D_S5), gain.reshape(1, D_S5))


def _s5_params(lam_re, lam_im, log_step, b_re, b_im, c_re, c_im):
    hp = lax.Precision.HIGHEST
    a_re = jnp.minimum(lam_re.astype(F32), -1e-4)
    a_im = lam_im.astype(F32)
    dt = jnp.exp(log_step.astype(F32))[:, None]
    mag = jnp.exp(a_re * dt)
    ab_re = mag * jnp.cos(a_im * dt)
    ab_im = mag * jnp.sin(a_im * dt)
    den = a_re * a_re + a_im * a_im
    x_re, x_im = ab_re - 1.0, ab_im
    z_re = (x_re * a_re + x_im * a_im) / den
    z_im = (x_im * a_re - x_re * a_im) / den
    br, bi = b_re.astype(F32), b_im.astype(F32)
    bb_re = z_re[..., None] * br - z_im[..., None] * bi
    bb_im = z_re[..., None] * bi + z_im[..., None] * br
    cr, ci = c_re.astype(F32), c_im.astype(F32)

    pw_re, pw_im = [jnp.ones_like(ab_re)], [jnp.zeros_like(ab_re)]
    for _ in range(S5_R):
        r, i = pw_re[-1], pw_im[-1]
        pw_re.append(r * ab_re - i * ab_im)
        pw_im.append(r * ab_im + i * ab_re)
    p_re, p_im = jnp.stack(pw_re), jnp.stack(pw_im)

    q_re, q_im = p_re[:S5_R, :, :, None], p_im[:S5_R, :, :, None]
    m_re = q_re * bb_re - q_im * bb_im
    m_im = q_re * bb_im + q_im * bb_re
    kern = (jnp.einsum('gop,lgph->ghlo', cr, m_re, precision=hp)
            - jnp.einsum('gop,lgph->ghlo', ci, m_im, precision=hp))
    kz = jnp.concatenate([jnp.zeros((S5_GROUPS, S5_GROUP, S5_K), F32),
                          kern.reshape(S5_GROUPS, S5_GROUP, S5_K)], axis=-1)
    tmat = jnp.stack([kz[:, :, (S5_R - s) * S5_GROUP:(S5_R - s) * S5_GROUP + S5_K]
                      for s in range(S5_R)], axis=1).reshape(S5_PAIRS, 2, S5_K, S5_K)

    bt_re, bt_im = bb_re.transpose(0, 2, 1)[:, None], bb_im.transpose(0, 2, 1)[:, None]
    rp_re = jnp.moveaxis(p_re[S5_R - 1::-1], 0, 1)[:, :, None, :]
    rp_im = jnp.moveaxis(p_im[S5_R - 1::-1], 0, 1)[:, :, None, :]
    h_re = (rp_re * bt_re - rp_im * bt_im).reshape(S5_PAIRS, 2, S5_K, S5_STATE)
    h_im = (rp_re * bt_im + rp_im * bt_re).reshape(S5_PAIRS, 2, S5_K, S5_STATE)
    zh = jnp.zeros_like(h_re[:, 0])
    hmat = jnp.concatenate([
        jnp.concatenate([h_re[:, 0], zh, h_im[:, 0], zh], axis=-1),
        jnp.concatenate([zh, h_re[:, 1], zh, h_im[:, 1]], axis=-1)], axis=1)

    ct_re, ct_im = cr.transpose(0, 2, 1)[:, :, None, :], ci.transpose(0, 2, 1)[:, :, None, :]
    e_re = jnp.moveaxis(p_re[1:], 0, 2)[..., None]
    e_im = jnp.moveaxis(p_im[1:], 0, 2)[..., None]
    g_re = (ct_re * e_re - ct_im * e_im).reshape(S5_PAIRS, 2, S5_STATE, S5_K)
    g_im = -(ct_re * e_im + ct_im * e_re).reshape(S5_PAIRS, 2, S5_STATE, S5_K)
    zg = jnp.zeros_like(g_re[:, 0])
    gmat = jnp.concatenate([
        jnp.concatenate([g_re[:, 0], zg], axis=-1), jnp.concatenate([zg, g_re[:, 1]], axis=-1),
        jnp.concatenate([g_im[:, 0], zg], axis=-1), jnp.concatenate([zg, g_im[:, 1]], axis=-1)],
        axis=1)

    a2 = jnp.stack([p_re[S5_R].reshape(S5_PAIRS, 2 * S5_STATE),
                    p_im[S5_R].reshape(S5_PAIRS, 2 * S5_STATE)], axis=1)
    return tmat.astype(BF16), hmat.astype(BF16), gmat.astype(BF16), a2


def _s5(u3, params, layer, d_skip, wglu_all, bglu, gain):
    yc = _s5core(_to_chunks(u3), params, layer, u3.shape[0])
    return _s5glu(yc, u3, d_skip, wglu_all, layer, bglu, gain)


def _outproj_kernel(h_ref, pre_ref, a_ref, b_ref, w_ref, o_ref, *, first):
    y = (jnp.dot(a_ref[...], w_ref[0], preferred_element_type=F32)
         + jnp.dot(b_ref[...], w_ref[1], preferred_element_type=F32))
    if first:
        npre = pre_ref.shape[0]

        @pl.when(pl.program_id(1) == 0)
        def _():
            o_ref[0:npre, :] = y[0:npre, :] + pre_ref[...]
            o_ref[npre:, :] = y[npre:, :] + h_ref[0:h_ref.shape[0] - npre, :]

        @pl.when(pl.program_id(1) != 0)
        def _():
            o_ref[...] = y + h_ref[...]
    else:
        o_ref[...] = y + h_ref[...]


def _outproj(h, prefix, a, b, w_all, layer, first, bsz, tm=704):
    k = a.shape[1]
    d = h.shape[1]
    lp = a.shape[0] // bsz
    tiles = lp // tm
    flat = lambda b, j: (b * tiles + j, 0)
    return pl.pallas_call(
        functools.partial(_outproj_kernel, first=first),
        grid=(bsz, tiles),
        in_specs=[
            _row_tile_specs(first, bsz, lp, tm, d),
            pl.BlockSpec((CHUNK, d), lambda b, j: (0, 0)),
            pl.BlockSpec((tm, k), flat),
            pl.BlockSpec((tm, k), flat),
            pl.BlockSpec((None, 2, k, d), lambda b, j: (layer, 0, 0, 0)),
        ],
        out_specs=pl.BlockSpec((tm, d), flat),
        out_shape=jax.ShapeDtypeStruct((bsz * lp, d), F32),
        compiler_params=pltpu.CompilerParams(
            dimension_semantics=("arbitrary", "arbitrary"), vmem_limit_bytes=VMEM_LIMIT),
        name="outproj",
    )(h, prefix, a, b, w_all)


FFN_HALO = 16


def _ffn_kernel(h_ref, halo_ref, g_ref, wg_ref, wu_ref, cw_ref, cb_ref, wd_ref, fg_ref, o_ref,
                hn_ref, *, final, sub):
    f_axis = 2 if final else 1
    f = pl.program_id(f_axis)
    tm = o_ref.shape[0]

    def block(r):
        hn = hn_ref[r:r + sub + FFN_HALO, :]
        a = jnp.dot(hn, wg_ref[...], preferred_element_type=F32)
        up = jnp.dot(hn[FFN_HALO:, :], wu_ref[...], preferred_element_type=F32)
        cw = cw_ref[...]
        conv = (cb_ref[...]
                + cw[0:1, :] * pltpu.roll(a, 2, 0)[FFN_HALO:, :]
                + cw[1:2, :] * pltpu.roll(a, 1, 0)[FFN_HALO:, :]
                + cw[2:3, :] * a[FFN_HALO:, :])
        hid = (_silu(conv) * up).astype(BF16)
        return jnp.dot(hid, wd_ref[...], preferred_element_type=F32)

    @pl.when(f == 0)
    def _():
        hn_ref[0:FFN_HALO, :] = _rms(halo_ref[...], g_ref[...]).astype(BF16)
        for r in range(0, tm, sub):
            x = h_ref[r:r + sub, :]
            hn_ref[FFN_HALO + r:FFN_HALO + r + sub, :] = _rms(x, g_ref[...]).astype(BF16)
        for r in range(0, tm, sub):
            o_ref[r:r + sub, :] = h_ref[r:r + sub, :] + block(r)

    @pl.when(f > 0)
    def _():
        for r in range(0, tm, sub):
            o_ref[r:r + sub, :] += block(r)

    if final:
        @pl.when(f == pl.num_programs(f_axis) - 1)
        def _():
            o_ref[...] = _rms(o_ref[...], fg_ref[...])


def _ffn(h, g, layer, wg_all, wu_all, cw_all, cb_all, wd_all, fg, final, bsz, tm=512, tf=512,
         sub=None):
    t, d = h.shape
    ff = wg_all.shape[2]
    hb = tm // FFN_HALO
    lp = t // bsz
    if final:
        grid = (bsz, (lp - CHUNK) // tm, ff // tf)
        row0 = lambda b, j, back: pl.multiple_of(b * lp + CHUNK + j * tm - back, FFN_HALO)
        row_specs = [
            pl.BlockSpec((pl.Element(tm), pl.Element(d)), lambda b, j, f: (row0(b, j, 0), 0)),
            pl.BlockSpec((pl.Element(FFN_HALO), pl.Element(d)),
                         lambda b, j, f: (row0(b, j, FFN_HALO), 0)),
        ]
        out_spec = pl.BlockSpec((tm, d), lambda b, j, f: (b * ((lp - CHUNK) // tm) + j, 0))
        out_rows = bsz * (lp - CHUNK)
        wmap = lambda fn: (lambda b, j, f: fn(f))
    else:
        grid = (t // tm, ff // tf)
        row_specs = [
            pl.BlockSpec((tm, d), lambda i, f: (i, 0)),
            pl.BlockSpec((FFN_HALO, d), lambda i, f: (jnp.maximum(i * hb - 1, 0), 0)),
        ]
        out_spec = pl.BlockSpec((tm, d), lambda i, f: (i, 0))
        out_rows = t
        wmap = lambda fn: (lambda i, f: fn(f))
    return pl.pallas_call(
        functools.partial(_ffn_kernel, final=final, sub=sub or tm),
        grid=grid,
        in_specs=row_specs + [
            pl.BlockSpec((1, d), wmap(lambda f: (0, 0))),
            pl.BlockSpec((None, d, tf), wmap(lambda f: (layer, 0, f))),
            pl.BlockSpec((None, d, tf), wmap(lambda f: (layer, 0, f))),
            pl.BlockSpec((None, 3, tf), wmap(lambda f: (layer, 0, f))),
            pl.BlockSpec((None, 1, tf), wmap(lambda f: (layer, 0, f))),
            pl.BlockSpec((None, tf, d), wmap(lambda f: (layer, f, 0))),
            pl.BlockSpec((1, d), wmap(lambda f: (0, 0))),
        ],
        out_specs=out_spec,
        out_shape=jax.ShapeDtypeStruct((out_rows, d), F32),
        scratch_shapes=[pltpu.VMEM((tm + FFN_HALO, d), BF16)],
        compiler_params=pltpu.CompilerParams(
            dimension_semantics=("arbitrary",) * len(grid), vmem_limit_bytes=VMEM_LIMIT),
        name="ffn_final" if final else "ffn",
    )(h, h, g.reshape(1, d), wg_all, wu_all, cw_all, cb_all.reshape(cb_all.shape[0], 1, ff),
      wd_all, fg.reshape(1, d))


def kernel(x, meta_tokens, lb_logits, norm_mix, w_in, hg_norm, s5_lambda_re, s5_lambda_im,
           s5_log_step, s5_b_re, s5_b_im, s5_c_re, s5_c_im, s5_d, w_glu, b_glu, s5_norm, w_out,
           norm_ffn, w_ffn_gate, w_ffn_up, ffn_conv_w, ffn_conv_b, w_ffn_down, final_norm):
    bsz, seq, d = x.shape
    lp = seq + CHUNK
    prefix = jnp.concatenate([jnp.zeros((SEQ_PAD, d), x.dtype), meta_tokens.astype(x.dtype)], axis=0)
    h = x.reshape(bsz * seq, d)

    sm = jax.nn.softmax(lb_logits.astype(F32), axis=0)
    lb_all = jnp.cumsum(sm, axis=0) - sm[0:1]

    w_in_b = w_in.astype(BF16)
    w_out_b = w_out.astype(BF16).reshape(DEPTH, 2, D_HG, d)
    w_glu_b = w_glu.astype(BF16)
    wg_b, wu_b, wd_b = (w.astype(BF16) for w in (w_ffn_gate, w_ffn_up, w_ffn_down))
    s5_params = jax.vmap(_s5_params)(s5_lambda_re, s5_lambda_im, s5_log_step,
                                     s5_b_re, s5_b_im, s5_c_re, s5_c_im)

    for l in range(DEPTH):
        o_hg, u3 = _proj_hgrn2(h, prefix, norm_mix[l], w_in_b, l, lb_all[l], hg_norm[l],
                               first=(l == 0), bsz=bsz)
        o_s5 = _s5(u3, s5_params, l, s5_d[l], w_glu_b, b_glu[l], s5_norm[l])
        h = _outproj(h, prefix, o_hg.reshape(bsz * lp, D_HG), o_s5.reshape(bsz * lp, D_S5),
                     w_out_b, l, first=(l == 0), bsz=bsz)
        h = _ffn(h, norm_ffn[l], l, wg_b, wu_b, ffn_conv_w, ffn_conv_b, wd_b, final_norm,
                 final=(l == DEPTH - 1), bsz=bsz, **(dict(tm=1024, sub=512) if l == DEPTH - 1 else dict(tm=1056, sub=528)))
    return h.reshape(bsz, seq, d)
```
